```python
import jax, jax.numpy as jnp
from jax import lax
import numpy as np

D_MODEL = 4096
BATCH = 1
SEQ = 8192
DEPTH = 2
DEC_BATCH = 16
DEC_SEQ = 16
PAST_LEN = 1024

CHUNK = 64
Q_BLOCK = 128
HEAD_DIM = 128
A_HEADS = D_MODEL // 2 // HEAD_DIM
A_DK = 128
A_DV = HEAD_DIM
A_QK = A_HEADS * A_DK
A_WIDTH = A_HEADS * A_DV
B_HEADS = D_MODEL // 2 // HEAD_DIM
B_WIDTH = B_HEADS * HEAD_DIM
C_HEADS = D_MODEL // HEAD_DIM
C_WIDTH = C_HEADS * HEAD_DIM
N_EVEN = (DEPTH + 1) // 2
N_ODD = DEPTH // 2
EVEN_SPLITS = (A_QK, A_QK, A_WIDTH, A_WIDTH, B_WIDTH, B_WIDTH, B_WIDTH, B_WIDTH)
ODD_SPLITS = (C_WIDTH, C_WIDTH, C_WIDTH, C_WIDTH, C_HEADS)
EVEN_IN = sum(EVEN_SPLITS)
ODD_IN = sum(ODD_SPLITS)
EPS = 1e-6

kernel_name = 'hgrn2_stickbreak_fox_streaming_step'


def _split(a, sizes):
    idx = [int(i) for i in np.cumsum(sizes)[:-1]]
    return jnp.split(a, idx, axis=-1)


def rms_norm(x, w):
    xf = x.astype(jnp.float32)
    y = xf * lax.rsqrt(jnp.mean(xf * xf, axis=-1, keepdims=True) + EPS)
    return (y * w.astype(jnp.float32)).astype(x.dtype)


def hgrn2_chunkwise(q, k, v, g, s0):
    B, T, H, DK = q.shape
    DV = v.shape[-1]
    c = min(CHUNK, T)
    n = T // c

    def to_chunks(a):
        return a.astype(jnp.float32).reshape(B, n, c, H, a.shape[-1]).transpose(1, 0, 3, 2, 4)

    qc, kc, vc, gc = to_chunks(q), to_chunks(k), to_chunks(v), to_chunks(g)
    causal = jnp.tril(jnp.ones((c, c), dtype=bool))

    def step(S, inp):
        qi, ki, vi, gi = inp
        G = jnp.cumsum(gi, axis=2)
        o_inter = jnp.einsum('bhtk,bhkv->bhtv', qi * jnp.exp(G), S)
        diff = jnp.where(causal[None, None, :, :, None], G[:, :, :, None, :] - G[:, :, None, :, :], -jnp.inf)
        att = jnp.einsum('bhtsk,bhsk->bhts', qi[:, :, :, None, :] * jnp.exp(diff), ki)
        o_intra = jnp.einsum('bhts,bhsv->bhtv', att, vi)
        G_last = G[:, :, -1:, :]
        S_new = S * jnp.exp(G_last[:, :, 0, :, None]) + jnp.einsum('bhsk,bhsv->bhkv', ki * jnp.exp(G_last - G), vi)
        return S_new, o_inter + o_intra

    S, o = lax.scan(step, s0.astype(jnp.float32), (qc, kc, vc, gc))
    o = o.transpose(1, 0, 3, 2, 4).reshape(B, T, H, DV)
    return o, S


def stick_breaking_attention(q, k, v):
    B, Tq, H, d = q.shape
    Tk = k.shape[1]
    off = Tk - Tq
    blk = min(Q_BLOCK, Tq)
    nb = Tq // blk
    qb = q.reshape(B, nb, blk, H, d).transpose(1, 0, 2, 3, 4)
    kf = k.astype(jnp.float32)
    vf = v.astype(jnp.float32)
    kpos = jnp.arange(Tk)
    scale = d ** -0.5

    def one_block(args):
        qi, bi = args
        z = jnp.einsum('bthd,bshd->bhts', qi.astype(jnp.float32), kf) * scale
        qpos = off + bi * blk + jnp.arange(blk)
        mask = kpos[None, :] < qpos[:, None]
        log_1m = jnp.where(mask, jax.nn.log_sigmoid(-z), 0.0)
        tail = lax.cumsum(log_1m, axis=3, reverse=True) - log_1m
        w = jnp.where(mask, jnp.exp(jax.nn.log_sigmoid(z) + tail), 0.0)
        return jnp.einsum('bhts,bshd->bthd', w, vf)

    o = lax.map(one_block, (qb, jnp.arange(nb)))
    return o.transpose(1, 0, 2, 3, 4).reshape(B, Tq, H, d)


def forgetting_attention(q, k, v, logf):
    B, Tq, H, d = q.shape
    Tk = k.shape[1]
    off = Tk - Tq
    blk = min(Q_BLOCK, Tq)
    nb = Tq // blk
    F = jnp.cumsum(logf.astype(jnp.float32), axis=1)
    Fk = F.transpose(0, 2, 1)
    Fqb = F[:, off:].reshape(B, nb, blk, H).transpose(1, 0, 3, 2)
    qb = q.reshape(B, nb, blk, H, d).transpose(1, 0, 2, 3, 4)
    kf = k.astype(jnp.float32)
    vf = v.astype(jnp.float32)
    kpos = jnp.arange(Tk)
    scale = d ** -0.5

    def one_block(args):
        qi, Fi, bi = args
        s = jnp.einsum('bthd,bshd->bhts', qi.astype(jnp.float32), kf) * scale + Fi[..., None] - Fk[:, :, None, :]
        qpos = off + bi * blk + jnp.arange(blk)
        s = jnp.where(kpos[None, :] <= qpos[:, None], s, -jnp.inf)
        p = jax.nn.softmax(s, axis=-1)
        return jnp.einsum('bhts,bshd->bthd', p, vf)

    o = lax.map(one_block, (qb, Fqb, jnp.arange(nb)))
    return o.transpose(1, 0, 2, 3, 4).reshape(B, Tq, H, d)


def even_layer_mixer(h, w_in, w_out, lb, a_norm_w, s0, past_k, past_v):
    B, T, _ = h.shape
    f32 = jnp.float32
    qa, fa, ia, ga, qb, kb, vb, gb = _split(h @ w_in, EVEN_SPLITS)
    lbh = lb.astype(f32).reshape(A_HEADS, A_DK)
    z = fa.astype(f32).reshape(B, T, A_HEADS, A_DK)
    f = lbh + (1.0 - lbh) * jax.nn.sigmoid(z)
    ka = (1.0 - lbh) * jax.nn.sigmoid(-z)
    qa = jax.nn.silu(qa.astype(f32)).reshape(B, T, A_HEADS, A_DK)
    oa, s_new = hgrn2_chunkwise(qa, ka, ia.astype(f32).reshape(B, T, A_HEADS, A_DV), jnp.log(f), s0)
    oa = rms_norm(oa, a_norm_w).reshape(B, T, A_WIDTH) * jax.nn.silu(ga.astype(f32))
    qb = qb.reshape(B, T, B_HEADS, HEAD_DIM)
    kb = kb.reshape(B, T, B_HEADS, HEAD_DIM)
    vb = vb.reshape(B, T, B_HEADS, HEAD_DIM)
    if past_k is None:
        k_all, v_all = kb, vb
    else:
        k_all = jnp.concatenate([past_k.astype(kb.dtype), kb], axis=1)
        v_all = jnp.concatenate([past_v.astype(vb.dtype), vb], axis=1)
    ob = stick_breaking_attention(qb, k_all, v_all).reshape(B, T, B_WIDTH) * jax.nn.silu(gb.astype(f32))
    y = jnp.concatenate([oa, ob], axis=-1).astype(h.dtype) @ w_out
    return y, s_new, kb, vb


def odd_layer_mixer(h, w_in, b_forget, w_out, past_k, past_v, past_logf):
    B, T, _ = h.shape
    f32 = jnp.float32
    q, k, v, gate, fl = _split(h @ w_in, ODD_SPLITS)
    q = q.reshape(B, T, C_HEADS, HEAD_DIM)
    k = k.reshape(B, T, C_HEADS, HEAD_DIM)
    v = v.reshape(B, T, C_HEADS, HEAD_DIM)
    logf = jax.nn.log_sigmoid(fl.astype(f32) + b_forget.astype(f32))
    if past_k is None:
        k_all, v_all, lf_all = k, v, logf
    else:
        k_all = jnp.concatenate([past_k.astype(k.dtype), k], axis=1)
        v_all = jnp.concatenate([past_v.astype(v.dtype), v], axis=1)
        lf_all = jnp.concatenate([past_logf.astype(f32), logf], axis=1)
    o = forgetting_attention(q, k_all, v_all, lf_all).reshape(B, T, C_WIDTH) * jax.nn.silu(gate.astype(f32))
    y = o.astype(h.dtype) @ w_out
    return y, k, v, logf


def setup_inputs(seed: int = 0) -> dict:
    key = jax.random.key(seed)
    ks = jax.random.split(key, 20)
    f32 = jnp.float32

    def nrm(k, shape, scale):
        return jax.random.normal(k, shape, f32) * scale

    return {
        'x_prompt': nrm(ks[0], (BATCH, SEQ, D_MODEL), 1.0),
        'x_sample': nrm(ks[1], (DEC_BATCH, DEC_SEQ, D_MODEL), 1.0),
        'state_a_hgrn': nrm(ks[2], (N_EVEN, DEC_BATCH, A_HEADS, A_DK, A_DV), 0.3),
        'cache_b_k': nrm(ks[3], (N_EVEN, DEC_BATCH, PAST_LEN, B_HEADS, HEAD_DIM), 1.0),
        'cache_b_v': nrm(ks[4], (N_EVEN, DEC_BATCH, PAST_LEN, B_HEADS, HEAD_DIM), 1.0),
        'cache_c_k': nrm(ks[5], (N_ODD, DEC_BATCH, PAST_LEN, C_HEADS, HEAD_DIM), 1.0),
        'cache_c_v': nrm(ks[6], (N_ODD, DEC_BATCH, PAST_LEN, C_HEADS, HEAD_DIM), 1.0),
        'cache_c_logf': jax.nn.log_sigmoid(nrm(ks[7], (N_ODD, DEC_BATCH, PAST_LEN, C_HEADS), 1.0) + 3.0),
        'norm_w': 1.0 + nrm(ks[8], (DEPTH, D_MODEL), 0.02),
        'final_norm_w': 1.0 + nrm(ks[9], (D_MODEL,), 0.02),
        'w_in_even': nrm(ks[10], (N_EVEN, D_MODEL, EVEN_IN), D_MODEL ** -0.5),
        'w_out_even': nrm(ks[11], (N_EVEN, A_WIDTH + B_WIDTH, D_MODEL), (A_WIDTH + B_WIDTH) ** -0.5),
        'lb_logits': nrm(ks[12], (N_EVEN + 1, A_QK), 0.5),
        'a_norm_w': 1.0 + nrm(ks[13], (N_EVEN, A_DV), 0.02),
        'w_in_odd': nrm(ks[14], (N_ODD, D_MODEL, ODD_IN), D_MODEL ** -0.5),
        'b_forget': jax.random.uniform(ks[15], (N_ODD, C_HEADS), f32, 1.0, 5.0),
        'w_out_odd': nrm(ks[16], (N_ODD, C_WIDTH, D_MODEL), C_WIDTH ** -0.5),
    }


def reference(x_prompt, x_sample, state_a_hgrn, cache_b_k, cache_b_v, cache_c_k, cache_c_v, cache_c_logf,
              norm_w, final_norm_w, w_in_even, w_out_even, lb_logits, a_norm_w, w_in_odd, b_forget, w_out_odd):
    lb_all = jnp.cumsum(jax.nn.softmax(lb_logits.astype(jnp.float32), axis=0), axis=0)[:N_EVEN]
    hp, hs = x_prompt, x_sample
    sa_p, sa_s, bk_p, bv_p, bk_s, bv_s = [], [], [], [], [], []
    ck_p, cv_p, cf_p, ck_s, cv_s, cf_s = [], [], [], [], [], []
    for layer in range(DEPTH):
        j = layer // 2
        n_p = rms_norm(hp, norm_w[layer])
        n_s = rms_norm(hs, norm_w[layer])
        if layer % 2 == 0:
            s0 = jnp.zeros((hp.shape[0], A_HEADS, A_DK, A_DV), jnp.float32)
            yp, sp, kp, vp = even_layer_mixer(n_p, w_in_even[j], w_out_even[j], lb_all[j], a_norm_w[j], s0, None, None)
            ys, ss, ksm, vsm = even_layer_mixer(n_s, w_in_even[j], w_out_even[j], lb_all[j], a_norm_w[j],
                                                state_a_hgrn[j], cache_b_k[j], cache_b_v[j])
            sa_p.append(sp); sa_s.append(ss)
            bk_p.append(kp); bv_p.append(vp); bk_s.append(ksm); bv_s.append(vsm)
        else:
            yp, kp, vp, fp = odd_layer_mixer(n_p, w_in_odd[j], b_forget[j], w_out_odd[j], None, None, None)
            ys, ksm, vsm, fsm = odd_layer_mixer(n_s, w_in_odd[j], b_forget[j], w_out_odd[j],
                                                cache_c_k[j], cache_c_v[j], cache_c_logf[j])
            ck_p.append(kp); cv_p.append(vp); cf_p.append(fp)
            ck_s.append(ksm); cv_s.append(vsm); cf_s.append(fsm)
        hp = hp + yp.astype(hp.dtype)
        hs = hs + ys.astype(hs.dtype)
    y_prompt = rms_norm(hp, final_norm_w)
    y_sample = rms_norm(hs, final_norm_w)
    new_state_a_prompt = jnp.stack(sa_p)
    new_state_a_sample = jnp.stack(sa_s)
    new_b_k_prompt = jnp.stack(bk_p)
    new_b_v_prompt = jnp.stack(bv_p)
    new_b_k_sample = jnp.stack(bk_s)
    new_b_v_sample = jnp.stack(bv_s)
    new_c_k_prompt = jnp.stack(ck_p)
    new_c_v_prompt = jnp.stack(cv_p)
    new_c_logf_prompt = jnp.stack(cf_p)
    new_c_k_sample = jnp.stack(ck_s)
    new_c_v_sample = jnp.stack(cv_s)
    new_c_logf_sample = jnp.stack(cf_s)
    return (y_prompt, y_sample, new_state_a_prompt, new_state_a_sample,
            new_b_k_prompt, new_b_v_prompt, new_b_k_sample, new_b_v_sample,
            new_c_k_prompt, new_c_v_prompt, new_c_logf_prompt,
            new_c_k_sample, new_c_v_sample, new_c_logf_sample)
```

```python
import functools

import numpy as np
import jax
import jax.numpy as jnp
from jax import lax
from jax.experimental import pallas as pl
from jax.experimental.pallas import tpu as pltpu

F32 = jnp.float32
BF16 = jnp.bfloat16

EPS = 1e-6
HEAD_DIM = 128
HGRN_CHUNK = 64
VMEM_LIMIT_BYTES = 56 * 1024 * 1024

_NT = (((1,), (1,)), ((), ()))
_TN = (((0,), (0,)), ((), ()))


def _params(*sem):
    return pltpu.CompilerParams(dimension_semantics=sem, vmem_limit_bytes=VMEM_LIMIT_BYTES)


def _dot(a, b):
    return jnp.dot(a, b, preferred_element_type=F32)


def _dot_nt(a, b):
    return lax.dot_general(a, b, _NT, preferred_element_type=F32)


def _dot_tn(a, b):
    return lax.dot_general(a, b, _TN, preferred_element_type=F32)


def _split3(x):
    hi = x.astype(BF16)
    r1 = x - hi.astype(F32)
    mid = r1.astype(BF16)
    lo = (r1 - mid.astype(F32)).astype(BF16)
    return hi, mid, lo


def _dot_exact_lhs01(a01, x):
    hi, mid, lo = _split3(x)
    return _dot(a01, hi) + _dot(a01, mid) + _dot(a01, lo)


def _sigmoid_pair(z):
    e = jnp.exp(-jnp.abs(z))
    r = 1.0 / (1.0 + e)
    er = e * r
    pos = z >= 0
    return jnp.where(pos, r, er), jnp.where(pos, er, r)


def _silu(x):
    return x * _sigmoid_pair(x)[0]


def _log_sigmoid(x):
    return jnp.minimum(x, 0.0) - jnp.log(1.0 + jnp.exp(-jnp.abs(x)))


def _rmsnorm_body(x_ref, w_ref, o_ref):
    x = x_ref[...]
    ms = jnp.mean(x * x, axis=-1, keepdims=True)
    o_ref[...] = (x * lax.rsqrt(ms + EPS) * w_ref[...]).astype(o_ref.dtype)


def rmsnorm(x, w, out_dtype, tm):
    m, d = x.shape
    return pl.pallas_call(
        _rmsnorm_body,
        grid=(m // tm,),
        in_specs=[pl.BlockSpec((tm, d), lambda i: (i, 0)), pl.BlockSpec((1, d), lambda i: (0, 0))],
        out_specs=pl.BlockSpec((tm, d), lambda i: (i, 0)),
        out_shape=jax.ShapeDtypeStruct((m, d), out_dtype),
        compiler_params=_params("parallel"),
        name="rmsnorm",
    )(x, w.reshape(1, d))


def _mm_body(a_ref, w_ref, o_ref):
    o_ref[...] = _dot(a_ref[...], w_ref[...]).astype(o_ref.dtype)


def _mm_res_body(a_ref, w_ref, r_ref, o_ref):
    o_ref[...] = (r_ref[...] + _dot(a_ref[...], w_ref[...])).astype(o_ref.dtype)


def matmul(a, w, col0, ncols, tm, tn, out_dtype, residual=None):
    m, k = a.shape
    assert m % tm == 0 and ncols % tn == 0 and col0 % tn == 0
    cb = col0 // tn
    in_specs = [pl.BlockSpec((tm, k), lambda i, j: (i, 0)),
                pl.BlockSpec((k, tn), lambda i, j: (0, cb + j))]
    args = [a, w]
    body = _mm_body
    if residual is not None:
        in_specs.append(pl.BlockSpec((tm, tn), lambda i, j: (i, j)))
        args.append(residual)
        body = _mm_res_body
    return pl.pallas_call(
        body,
        grid=(m // tm, ncols // tn),
        in_specs=in_specs,
        out_specs=pl.BlockSpec((tm, tn), lambda i, j: (i, j)),
        out_shape=jax.ShapeDtypeStruct((m, ncols), out_dtype),
        compiler_params=_params("parallel", "parallel"),
        name="proj",
    )(*args)


def _hgrn_maps(c):
    levels = int(np.log2(c))
    assert 2 ** levels == c
    t = np.arange(c)[:, None]
    s = np.arange(c)[None, :]
    mats = [(s <= t), (s > t)]
    for l in range(levels):
        b = 2 ** l
        start = (t // (2 * b)) * (2 * b)
        upper = (t // b) % 2 == 1
        mats.append(upper & (s >= start + b) & (s <= t))
        mats.append((~upper) & (s > t) & (s <= start + b - 1))
    return np.concatenate(mats, axis=0).astype(np.float32), levels


def _hgrn_body(q_ref, z_ref, v_ref, g_ref, lb_ref, nw_ref, a_ref, s0_ref, o_ref, sout_ref, st_scr,
               *, c, n_chunks, levels):
    tb = pl.program_id(2)

    @pl.when(tb == 0)
    def _():
        st_scr[...] = s0_ref[0, 0].T

    lb = lb_ref[...]
    one_m_lb = 1.0 - lb
    nw = nw_ref[...]
    amat = a_ref[...]
    row = lax.broadcasted_iota(jnp.int32, (c, c), 0)
    col = lax.broadcasted_iota(jnp.int32, (c, c), 1)
    xor = row ^ col

    def chunk(ci, carry):
        r0 = pl.multiple_of(ci * c, c)
        q = _silu(q_ref[pl.ds(r0, c), :])
        sig, sig_neg = _sigmoid_pair(z_ref[pl.ds(r0, c), :])
        g = jnp.log(lb + one_m_lb * sig)
        k = one_m_lb * sig_neg
        v = v_ref[pl.ds(r0, c), :]
        vb = v.astype(BF16)

        e = jnp.exp(_dot_exact_lhs01(amat, g))
        st = st_scr[...]
        o = _dot_nt((q * e[0:c]).astype(BF16), st.astype(BF16))

        att = None
        for l in range(levels - 1, -1, -1):
            base = (2 + 2 * l) * c
            ql = (q * e[base:base + c]).astype(BF16)
            kl = (k * e[base + c:base + 2 * c]).astype(BF16)
            al = _dot_nt(ql, kl)
            att = al if att is None else jnp.where(xor < 2 ** (l + 1), al, att)
        att = jnp.where(row > col, att, 0.0)
        o = o + _dot(att.astype(BF16), vb) + jnp.sum(q * k, axis=-1, keepdims=True) * v

        kd = (k * e[c:2 * c]).astype(BF16)
        g_last = e[c - 1:c]
        st_scr[...] = st * g_last + _dot_tn(vb, kd)

        ms = jnp.mean(o * o, axis=-1, keepdims=True)
        y = o * lax.rsqrt(ms + EPS) * nw
        o_ref[pl.ds(r0, c), :] = (y * _silu(g_ref[pl.ds(r0, c), :])).astype(o_ref.dtype)
        return carry

    lax.fori_loop(0, n_chunks, chunk, 0)

    @pl.when(tb == pl.num_programs(2) - 1)
    def _():
        sout_ref[0, 0] = st_scr[...].T


def hgrn2(qa, fa, ia, ga, lb, a_norm_w, s0, batch, seq, rows_per_step, c, out_dtype):
    n, width = qa.shape
    h = width // HEAD_DIM
    assert n == batch * seq and seq % rows_per_step == 0 and rows_per_step % c == 0
    nb = seq // rows_per_step
    amat_np, levels = _hgrn_maps(c)
    amat = jnp.asarray(amat_np, BF16)
    row_spec = pl.BlockSpec((rows_per_step, HEAD_DIM), lambda b, hh, t: (b * nb + t, hh))
    vec_spec = pl.BlockSpec((1, HEAD_DIM), lambda b, hh, t: (0, hh))
    state_spec = pl.BlockSpec((1, 1, HEAD_DIM, HEAD_DIM), lambda b, hh, t: (b, hh, 0, 0))
    body = functools.partial(_hgrn_body, c=c, n_chunks=rows_per_step // c, levels=levels)
    return pl.pallas_call(
        body,
        grid=(batch, h, nb),
        in_specs=[row_spec, row_spec, row_spec, row_spec, vec_spec,
                  pl.BlockSpec((1, HEAD_DIM), lambda b, hh, t: (0, 0)),
                  pl.BlockSpec(amat.shape, lambda b, hh, t: (0, 0)),
                  state_spec],
        out_specs=[row_spec, state_spec],
        out_shape=[jax.ShapeDtypeStruct((n, width), out_dtype),
                   jax.ShapeDtypeStruct((batch, h, HEAD_DIM, HEAD_DIM), F32)],
        scratch_shapes=[pltpu.VMEM((HEAD_DIM, HEAD_DIM), F32)],
        compiler_params=_params("parallel", "parallel", "arbitrary"),
        name="hgrn2",
    )(qa, fa, ia, ga, lb.reshape(1, width), a_norm_w.reshape(1, HEAD_DIM), amat, s0)


def _sb_prompt_body(q_ref, k_ref, v_ref, g_ref, u_ref, o_ref, *, blk):
    i = pl.program_id(1)
    q = (q_ref[...] * (HEAD_DIM ** -0.5)).astype(BF16)
    umat = u_ref[...]
    row = lax.broadcasted_iota(jnp.int32, (blk, blk), 0)
    col = lax.broadcasted_iota(jnp.int32, (blk, blk), 1)
    valid = col < row

    def tile(j, carry, masked):
        run, acc = carry
        k0 = pl.multiple_of(j * blk, blk)
        kt = k_ref[pl.ds(k0, blk), :].astype(BF16)
        vt = v_ref[pl.ds(k0, blk), :].astype(BF16)
        z = _dot_nt(q, kt)
        ls_neg = -(jnp.maximum(z, 0.0) + jnp.log(1.0 + jnp.exp(-jnp.abs(z))))
        lm = jnp.where(valid, ls_neg, 0.0) if masked else ls_neg
        hi = lm.astype(BF16)
        lo = (lm - hi.astype(F32)).astype(BF16)
        tail = _dot(hi, umat) + _dot(lo, umat)
        w = jnp.exp(z + ls_neg + tail + run)
        if masked:
            w = jnp.where(valid, w, 0.0)
        acc = acc + _dot(w.astype(BF16), vt)
        run = run + jnp.sum(lm, axis=-1, keepdims=True)
        return run, acc

    carry = (jnp.zeros((blk, 1), F32), jnp.zeros((blk, HEAD_DIM), F32))
    carry = tile(i, carry, True)
    carry = lax.fori_loop(0, i, lambda jj, cr: tile(i - 1 - jj, cr, False), carry)
    o_ref[...] = (carry[1] * _silu(g_ref[...])).astype(o_ref.dtype)


def _tail_matrix(n):
    sp = np.arange(n)[:, None]
    s = np.arange(n)[None, :]
    return (sp > s).astype(np.float32)


def sb_prompt(q, k, v, gate, blk, out_dtype):
    t, width = q.shape
    h = width // HEAD_DIM
    umat = jnp.asarray(_tail_matrix(blk), BF16)
    q_spec = pl.BlockSpec((blk, HEAD_DIM), lambda hh, i: (i, hh))
    kv_spec = pl.BlockSpec((t, HEAD_DIM), lambda hh, i: (0, hh))
    return pl.pallas_call(
        functools.partial(_sb_prompt_body, blk=blk),
        grid=(h, t // blk),
        in_specs=[q_spec, kv_spec, kv_spec, q_spec, pl.BlockSpec((blk, blk), lambda hh, i: (0, 0))],
        out_specs=q_spec,
        out_shape=jax.ShapeDtypeStruct((t, width), out_dtype),
        compiler_params=_params("parallel", "parallel"),
        name="sb_prompt",
    )(q, k, v, gate, umat)


def _logf_cumsum_body(fl_ref, b_ref, tri_ref, lf_ref, f_ref, carry_scr):
    @pl.when(pl.program_id(0) == 0)
    def _():
        carry_scr[...] = jnp.zeros_like(carry_scr)

    lf = _log_sigmoid(fl_ref[...] + b_ref[...])
    lf_ref[...] = lf
    f = carry_scr[...] + _dot_exact_lhs01(tri_ref[...], lf)
    f_ref[...] = f
    carry_scr[...] = f[f.shape[0] - 1:, :]


def logf_cumsum(fl, b_forget, blk):
    t, h = fl.shape
    tri = jnp.asarray(np.tril(np.ones((blk, blk), np.float32)), BF16)
    spec = pl.BlockSpec((blk, h), lambda i: (i, 0))
    return pl.pallas_call(
        _logf_cumsum_body,
        grid=(t // blk,),
        in_specs=[spec, pl.BlockSpec((1, h), lambda i: (0, 0)), pl.BlockSpec((blk, blk), lambda i: (0, 0))],
        out_specs=[spec, spec],
        out_shape=[jax.ShapeDtypeStruct((t, h), F32), jax.ShapeDtypeStruct((t, h), F32)],
        scratch_shapes=[pltpu.VMEM((1, h), F32)],
        compiler_params=_params("arbitrary"),
        name="logf_cumsum",
    )(fl, b_forget.reshape(1, h), tri)


def _logf_body(fl_ref, b_ref, lf_ref):
    lf_ref[...] = _log_sigmoid(fl_ref[...] + b_ref[...])


def logf_only(fl, b_forget):
    t, h = fl.shape
    return pl.pallas_call(
        _logf_body,
        out_shape=jax.ShapeDtypeStruct((t, h), F32),
        name="logf",
    )(fl, b_forget.reshape(1, h))


def _fox_prompt_body(q_ref, k_ref, v_ref, g_ref, fq_ref, fk_ref, o_ref, *, blk):
    i = pl.program_id(1)
    q = (q_ref[...] * (HEAD_DIM ** -0.5)).astype(BF16)
    fq = fq_ref[0]
    row = lax.broadcasted_iota(jnp.int32, (blk, blk), 0)
    col = lax.broadcasted_iota(jnp.int32, (blk, blk), 1)
    valid = col <= row

    def tile(j, carry, masked):
        m, l, acc = carry
        k0 = pl.multiple_of(j * blk, blk)
        kt = k_ref[pl.ds(k0, blk), :].astype(BF16)
        vt = v_ref[pl.ds(k0, blk), :].astype(BF16)
        s = _dot_nt(q, kt) + (fq - fk_ref[0, pl.ds(j, 1), :])
        if masked:
            s = jnp.where(valid, s, -jnp.inf)
        m_new = jnp.maximum(m, jnp.max(s, axis=-1, keepdims=True))
        alpha = jnp.exp(m - m_new)
        p = jnp.exp(s - m_new)
        l = alpha * l + jnp.sum(p, axis=-1, keepdims=True)
        acc = alpha * acc + _dot(p.astype(BF16), vt)
        return m_new, l, acc

    carry = (jnp.full((blk, 1), -jnp.inf, F32), jnp.zeros((blk, 1), F32), jnp.zeros((blk, HEAD_DIM), F32))
    carry = tile(i, carry, True)
    carry = lax.fori_loop(0, i, lambda jj, cr: tile(jj, cr, False), carry)
    _, l, acc = carry
    o_ref[...] = (acc / l * _silu(g_ref[...])).astype(o_ref.dtype)


def fox_prompt(q, k, v, gate, f_cum, blk, out_dtype):
    t, width = q.shape
    h = width // HEAD_DIM
    f_t = f_cum.T
    fq = f_t.reshape(h, t, 1)
    fk = f_t.reshape(h, t // blk, blk)
    q_spec = pl.BlockSpec((blk, HEAD_DIM), lambda hh, i: (i, hh))
    kv_spec = pl.BlockSpec((t, HEAD_DIM), lambda hh, i: (0, hh))
    return pl.pallas_call(
        functools.partial(_fox_prompt_body, blk=blk),
        grid=(h, t // blk),
        in_specs=[q_spec, kv_spec, kv_spec, q_spec,
                  pl.BlockSpec((1, blk, 1), lambda hh, i: (hh, i, 0)),
                  pl.BlockSpec((1, t // blk, blk), lambda hh, i: (hh, 0, 0))],
        out_specs=q_spec,
        out_shape=jax.ShapeDtypeStruct((t, width), out_dtype),
        compiler_params=_params("parallel", "parallel"),
        name="fox_prompt",
    )(q, k, v, gate, fq, fk)


HEADS_PER_GROUP = 8


def _expand_queries(q, tq):
    gw = q.shape[1]
    rep = jnp.concatenate([q] * HEADS_PER_GROUP, axis=0)
    r = lax.broadcasted_iota(jnp.int32, (HEADS_PER_GROUP * tq, gw), 0) // tq
    cidx = lax.broadcasted_iota(jnp.int32, (HEADS_PER_GROUP * tq, gw), 1) // HEAD_DIM
    return jnp.where(r == cidx, rep, 0.0)


def _collect_heads(full, tq):
    return jnp.concatenate(
        [full[hh * tq:(hh + 1) * tq, hh * HEAD_DIM:(hh + 1) * HEAD_DIM] for hh in range(HEADS_PER_GROUP)],
        axis=1)


def _sb_decode_body(q_ref, kn_ref, vn_ref, kc_ref, vc_ref, g_ref, u_ref, o_ref, k_scr, v_scr, *, past, tq):
    tk = past + tq
    k_scr[0:past, :] = kc_ref[0].astype(BF16)
    k_scr[past:tk, :] = kn_ref[...].astype(BF16)
    v_scr[0:past, :] = vc_ref[0].astype(BF16)
    v_scr[past:tk, :] = vn_ref[...].astype(BF16)

    lanes = HEADS_PER_GROUP * tq
    qx = _expand_queries(q_ref[...] * (HEAD_DIM ** -0.5), tq).astype(BF16)
    z = _dot_nt(k_scr[...], qx)
    kpos = lax.broadcasted_iota(jnp.int32, (tk, lanes), 0)
    qpos = past + lax.broadcasted_iota(jnp.int32, (tk, lanes), 1) % tq
    valid = kpos < qpos
    ls_neg = -(jnp.maximum(z, 0.0) + jnp.log(1.0 + jnp.exp(-jnp.abs(z))))
    lm = jnp.where(valid, ls_neg, 0.0)
    hi = lm.astype(BF16)
    lo = (lm - hi.astype(F32)).astype(BF16)
    umat = u_ref[...]
    tail = _dot(umat, hi) + _dot(umat, lo)
    w = jnp.where(valid, jnp.exp(z + ls_neg + tail), 0.0)
    full = _dot_tn(w.astype(BF16), v_scr[...])
    o_ref[...] = (_collect_heads(full, tq) * _silu(g_ref[...])).astype(o_ref.dtype)


def sb_decode(q, k_new, v_new, k_cache, v_cache, gate, batch, tq, out_dtype):
    n, width = q.shape
    past = k_cache.shape[1]
    tk = past + tq
    gw = HEADS_PER_GROUP * HEAD_DIM
    groups = width // gw
    umat = jnp.asarray(_tail_matrix(tk).T, BF16)
    row_spec = pl.BlockSpec((tq, gw), lambda b, g: (b, g))
    cache_spec = pl.BlockSpec((1, past, gw), lambda b, g: (b, 0, g))
    return pl.pallas_call(
        functools.partial(_sb_decode_body, past=past, tq=tq),
        grid=(batch, groups),
        in_specs=[row_spec, row_spec, row_spec, cache_spec, cache_spec, row_spec,
                  pl.BlockSpec((tk, tk), lambda b, g: (0, 0))],
        out_specs=row_spec,
        out_shape=jax.ShapeDtypeStruct((n, width), out_dtype),
        scratch_shapes=[pltpu.VMEM((tk, gw), BF16), pltpu.VMEM((tk, gw), BF16)],
        compiler_params=_params("parallel", "parallel"),
        name="sb_decode",
    )(q, k_new, v_new, k_cache, v_cache, gate, umat)


def _fox_decode_body(q_ref, kn_ref, vn_ref, kc_ref, vc_ref, g_ref, lfn_ref, lfc_ref, ex_ref, tri_ref, o_ref,
                     k_scr, v_scr, lf_scr, *, past, tq):
    tk = past + tq
    k_scr[0:past, :] = kc_ref[0].astype(BF16)
    k_scr[past:tk, :] = kn_ref[...].astype(BF16)
    v_scr[0:past, :] = vc_ref[0].astype(BF16)
    v_scr[past:tk, :] = vn_ref[...].astype(BF16)
    lf_scr[0:past, :] = lfc_ref[0]
    lf_scr[past:tk, :] = lfn_ref[...]

    lanes = HEADS_PER_GROUP * tq
    hi, mid, lo = _split3(lf_scr[...])
    ex = ex_ref[0]
    lfx = _dot(hi, ex) + _dot(mid, ex) + _dot(lo, ex)
    f_key = _dot_exact_lhs01(tri_ref[...], lfx)
    kpos = lax.broadcasted_iota(jnp.int32, (tk, lanes), 0)
    qpos = past + lax.broadcasted_iota(jnp.int32, (tk, lanes), 1) % tq
    f_query = jnp.sum(jnp.where(kpos == qpos, f_key, 0.0), axis=0, keepdims=True)

    qx = _expand_queries(q_ref[...] * (HEAD_DIM ** -0.5), tq).astype(BF16)
    s = _dot_nt(k_scr[...], qx) + (f_query - f_key)
    s = jnp.where(kpos <= qpos, s, -jnp.inf)
    p = jnp.exp(s - jnp.max(s, axis=0, keepdims=True))
    p = p / jnp.sum(p, axis=0, keepdims=True)
    full = _dot_tn(p.astype(BF16), v_scr[...])
    o_ref[...] = (_collect_heads(full, tq) * _silu(g_ref[...])).astype(o_ref.dtype)


def fox_decode(q, k_new, v_new, k_cache, v_cache, gate, lf_new, lf_cache, batch, tq, out_dtype):
    n, width = q.shape
    h = width // HEAD_DIM
    past = k_cache.shape[1]
    tk = past + tq
    gw = HEADS_PER_GROUP * HEAD_DIM
    groups = width // gw
    lanes = HEADS_PER_GROUP * tq
    tri = jnp.asarray(np.tril(np.ones((tk, tk), np.float32)), BF16)
    head_of_lane = np.arange(lanes)[None, None, :] // tq + HEADS_PER_GROUP * np.arange(groups)[:, None, None]
    expand = jnp.asarray((np.arange(h)[None, :, None] == head_of_lane).astype(np.float32), BF16)
    row_spec = pl.BlockSpec((tq, gw), lambda b, g: (b, g))
    cache_spec = pl.BlockSpec((1, past, gw), lambda b, g: (b, 0, g))
    return pl.pallas_call(
        functools.partial(_fox_decode_body, past=past, tq=tq),
        grid=(batch, groups),
        in_specs=[row_spec, row_spec, row_spec, cache_spec, cache_spec, row_spec,
                  pl.BlockSpec((tq, h), lambda b, g: (b, 0)),
                  pl.BlockSpec((1, past, h), lambda b, g: (b, 0, 0)),
                  pl.BlockSpec((1, h, lanes), lambda b, g: (g, 0, 0)),
                  pl.BlockSpec((tk, tk), lambda b, g: (0, 0))],
        out_specs=row_spec,
        out_shape=jax.ShapeDtypeStruct((n, width), out_dtype),
        scratch_shapes=[pltpu.VMEM((tk, gw), BF16), pltpu.VMEM((tk, gw), BF16), pltpu.VMEM((tk, h), F32)],
        compiler_params=_params("parallel", "parallel"),
        name="fox_decode",
    )(q, k_new, v_new, k_cache, v_cache, gate, lf_new, lf_cache, expand, tri)


def _tile(m, pref):
    return pref if m % pref == 0 else m


def _even_layer(x, norm_w, w_in, w_out, lb, a_norm_w, s0, caches, batch, seq, prompt):
    n, d = x.shape
    half = d // 2
    tm = _tile(n, 1024)
    hn = rmsnorm(x, norm_w, BF16, _tile(n, 256))
    qa, fa, ia, ga, qb, kb, vb, gb = [matmul(hn, w_in, g * half, half, tm, 1024, F32) for g in range(8)]
    if prompt:
        oa, s_new = hgrn2(qa, fa, ia, ga, lb, a_norm_w, s0, batch, seq, 512, HGRN_CHUNK, BF16)
        ob = sb_prompt(qb, kb, vb, gb, 256, BF16)
    else:
        oa, s_new = hgrn2(qa, fa, ia, ga, lb, a_norm_w, s0, batch, seq, seq, seq, BF16)
        ob = sb_decode(qb, kb, vb, caches[0], caches[1], gb, batch, seq, BF16)
    mixed = jnp.concatenate([oa, ob], axis=1)
    y = matmul(mixed, w_out, 0, d, tm, 1024, F32, residual=x)
    return y, s_new, kb, vb


def _odd_layer(x, norm_w, w_in, w_fl, b_forget, w_out, caches, batch, seq, prompt):
    n, d = x.shape
    heads = d // HEAD_DIM
    tm = _tile(n, 1024)
    hn = rmsnorm(x, norm_w, BF16, _tile(n, 256))
    q, k, v, gate = [matmul(hn, w_in, g * d, d, tm, 1024, F32) for g in range(4)]
    fl = matmul(hn, w_fl, 0, heads, tm, heads, F32)
    if prompt:
        logf, f_cum = logf_cumsum(fl, b_forget, 512)
        o = fox_prompt(q, k, v, gate, f_cum, 256, BF16)
    else:
        logf = logf_only(fl, b_forget)
        o = fox_decode(q, k, v, caches[0], caches[1], gate, logf, caches[2], batch, seq, BF16)
    y = matmul(o, w_out, 0, d, tm, 1024, F32, residual=x)
    return y, k, v, logf


def kernel(x_prompt, x_sample, state_a_hgrn, cache_b_k, cache_b_v, cache_c_k, cache_c_v, cache_c_logf,
           norm_w, final_norm_w, w_in_even, w_out_even, lb_logits, a_norm_w, w_in_odd, b_forget, w_out_odd):
    bp, tp, d = x_prompt.shape
    bs, ts, _ = x_sample.shape
    depth = norm_w.shape[0]
    n_even = w_in_even.shape[0]
    past = cache_b_k.shape[2]
    lb_all = jnp.cumsum(jax.nn.softmax(lb_logits.astype(F32), axis=0), axis=0)[:n_even]

    hp = x_prompt.reshape(bp * tp, d)
    hs = x_sample.reshape(bs * ts, d)
    outs = {name: [] for name in ("sa_p", "sa_s", "bk_p", "bv_p", "bk_s", "bv_s",
                                  "ck_p", "cv_p", "cf_p", "ck_s", "cv_s", "cf_s")}
    for layer in range(depth):
        j = layer // 2
        if layer % 2 == 0:
            w_in = w_in_even[j].astype(BF16)
            w_out = w_out_even[j].astype(BF16)
            a_heads = state_a_hgrn.shape[2]
            zeros = jnp.zeros((bp, a_heads) + state_a_hgrn.shape[3:], F32)
            hp, sp, kp, vp = _even_layer(hp, norm_w[layer], w_in, w_out, lb_all[j], a_norm_w[j], zeros,
                                         None, bp, tp, True)
            ck = cache_b_k[j].reshape(bs, past, -1)
            cv = cache_b_v[j].reshape(bs, past, -1)
            hs, ss, ksm, vsm = _even_layer(hs, norm_w[layer], w_in, w_out, lb_all[j], a_norm_w[j],
                                           state_a_hgrn[j], (ck, cv), bs, ts, False)
            b_heads = cache_b_k.shape[3]
            outs["sa_p"].append(sp); outs["sa_s"].append(ss)
            outs["bk_p"].append(kp.reshape(bp, tp, b_heads, HEAD_DIM))
            outs["bv_p"].append(vp.reshape(bp, tp, b_heads, HEAD_DIM))
            outs["bk_s"].append(ksm.reshape(bs, ts, b_heads, HEAD_DIM))
            outs["bv_s"].append(vsm.reshape(bs, ts, b_heads, HEAD_DIM))
        else:
            c_heads = cache_c_k.shape[3]
            c_width = c_heads * HEAD_DIM
            w_in = w_in_odd[j, :, :4 * c_width].astype(BF16)
            w_fl = w_in_odd[j, :, 4 * c_width:].astype(BF16)
            w_out = w_out_odd[j].astype(BF16)
            hp, kp, vp, fp = _odd_layer(hp, norm_w[layer], w_in, w_fl, b_forget[j], w_out, None, bp, tp, True)
            caches = (cache_c_k[j].reshape(bs, past, -1), cache_c_v[j].reshape(bs, past, -1), cache_c_logf[j])
            hs, ksm, vsm, fsm = _odd_layer(hs, norm_w[layer], w_in, w_fl, b_forget[j], w_out, caches, bs, ts, False)
            outs["ck_p"].append(kp.reshape(bp, tp, c_heads, HEAD_DIM))
            outs["cv_p"].append(vp.reshape(bp, tp, c_heads, HEAD_DIM))
            outs["cf_p"].append(fp.reshape(bp, tp, c_heads))
            outs["ck_s"].append(ksm.reshape(bs, ts, c_heads, HEAD_DIM))
            outs["cv_s"].append(vsm.reshape(bs, ts, c_heads, HEAD_DIM))
            outs["cf_s"].append(fsm.reshape(bs, ts, c_heads))
    y_prompt = rmsnorm(hp, final_norm_w, F32, 256).reshape(bp, tp, d)
    y_sample = rmsnorm(hs, final_norm_w, F32, 256).reshape(bs, ts, d)
    st = {k: jnp.stack(v) for k, v in outs.items()}
    return (y_prompt, y_sample, st["sa_p"], st["sa_s"], st["bk_p"], st["bv_p"], st["bk_s"], st["bv_s"],
            st["ck_p"], st["cv_p"], st["cf_p"], st["ck_s"], st["cv_s"], st["cf_s"])
```

```python
import functools

import numpy as np
import jax
import jax.numpy as jnp
from jax import lax
from jax.experimental import pallas as pl
from jax.experimental.pallas import tpu as pltpu

F32 = jnp.float32
BF16 = jnp.bfloat16

EPS = 1e-6
HEAD_DIM = 128
LANES = 128
LOG2E = 1.4426950408889634
HGRN_CHUNK = 64
VMEM_LIMIT_BYTES = 56 * 1024 * 1024

_NT = (((1,), (1,)), ((), ()))
_TN = (((0,), (0,)), ((), ()))


def _params(*sem):
    return pltpu.CompilerParams(dimension_semantics=sem, vmem_limit_bytes=VMEM_LIMIT_BYTES)


def _dot(a, b):
    return jnp.dot(a, b, preferred_element_type=F32)


def _dot_nt(a, b):
    return lax.dot_general(a, b, _NT, preferred_element_type=F32)


def _dot_tn(a, b):
    return lax.dot_general(a, b, _TN, preferred_element_type=F32)


def _split3(x):
    hi = x.astype(BF16)
    r1 = x - hi.astype(F32)
    mid = r1.astype(BF16)
    lo = (r1 - mid.astype(F32)).astype(BF16)
    return hi, mid, lo


def _dot_exact_lhs01(a01, x):
    hi, mid, lo = _split3(x)
    return _dot(a01, hi) + _dot(a01, mid) + _dot(a01, lo)


def _sigmoid_pair(z):
    e = jnp.exp(-jnp.abs(z))
    r = 1.0 / (1.0 + e)
    er = e * r
    pos = z >= 0
    return jnp.where(pos, r, er), jnp.where(pos, er, r)


def _silu(x):
    return x * _sigmoid_pair(x)[0]


def _log_sigmoid(x):
    return jnp.minimum(x, 0.0) - jnp.log(1.0 + jnp.exp(-jnp.abs(x)))


def _rmsnorm_body(x_ref, w_ref, o_ref):
    x = x_ref[...]
    ms = jnp.mean(x * x, axis=-1, keepdims=True)
    o_ref[...] = (x * lax.rsqrt(ms + EPS) * w_ref[...]).astype(o_ref.dtype)


def rmsnorm(x, w, out_dtype, tm):
    m, d = x.shape
    return pl.pallas_call(
        _rmsnorm_body,
        grid=(m // tm,),
        in_specs=[pl.BlockSpec((tm, d), lambda i: (i, 0)), pl.BlockSpec((1, d), lambda i: (0, 0))],
        out_specs=pl.BlockSpec((tm, d), lambda i: (i, 0)),
        out_shape=jax.ShapeDtypeStruct((m, d), out_dtype),
        compiler_params=_params("parallel"),
        name="rmsnorm",
    )(x, w.reshape(1, d))


def _proj_body(*refs, n_in, has_residual, outs, cast_weights):
    a_refs = refs[:n_in]
    w_refs = refs[n_in:2 * n_in]
    pos = 2 * n_in
    r_ref = refs[pos] if has_residual else None
    pos += int(has_residual)
    o_refs = refs[pos:pos + len(outs)]
    if cast_weights:
        wb_refs = refs[pos + len(outs):]

        @pl.when(pl.program_id(1) == 0)
        def _():
            for w_ref, wb_ref in zip(w_refs, wb_refs):
                wb_ref[...] = w_ref[...].astype(BF16)
    else:
        wb_refs = w_refs

    acc = None
    for a_ref, wb_ref in zip(a_refs, wb_refs):
        d = _dot(a_ref[...], wb_ref[...])
        acc = d if acc is None else acc + d
    if has_residual:
        acc = r_ref[...] + acc
    for o_ref, (by_head, _, scale) in zip(o_refs, outs):
        val = acc if scale == 1.0 else acc * scale
        if by_head:
            for hh in range(o_ref.shape[1]):
                o_ref[:, hh, :] = val[:, hh * HEAD_DIM:(hh + 1) * HEAD_DIM].astype(o_ref.dtype)
        else:
            o_ref[...] = val.astype(o_ref.dtype)


def proj(a_list, w, col0, ncols, tm, tn, outs, residual=None):
    m = a_list[0].shape[0]
    kk = a_list[0].shape[1]
    assert all(a.shape == (m, kk) for a in a_list) and w.shape[0] == kk * len(a_list)
    assert m % tm == 0 and ncols % tn == 0 and col0 % tn == 0
    cb = col0 // tn
    cast_weights = w.dtype != BF16
    in_specs = [pl.BlockSpec((tm, kk), lambda j, i: (i, 0)) for _ in a_list]
    in_specs += [pl.BlockSpec((kk, tn), functools.partial(lambda j, i, r: (r, cb + j), r=r)) for r in range(len(a_list))]
    args = list(a_list) + [w] * len(a_list)
    if residual is not None:
        in_specs.append(pl.BlockSpec((tm, tn), lambda j, i: (i, j)))
        args.append(residual)
    out_specs, out_shape = [], []
    for by_head, dtype, _ in outs:
        if by_head:
            out_specs.append(pl.BlockSpec((tm, tn // HEAD_DIM, HEAD_DIM), lambda j, i: (i, j, 0)))
            out_shape.append(jax.ShapeDtypeStruct((m, ncols // HEAD_DIM, HEAD_DIM), dtype))
        else:
            out_specs.append(pl.BlockSpec((tm, tn), lambda j, i: (i, j)))
            out_shape.append(jax.ShapeDtypeStruct((m, ncols), dtype))
    return pl.pallas_call(
        functools.partial(_proj_body, n_in=len(a_list), has_residual=residual is not None, outs=tuple(outs),
                          cast_weights=cast_weights),
        grid=(ncols // tn, m // tm),
        in_specs=in_specs,
        out_specs=out_specs,
        out_shape=out_shape,
        scratch_shapes=[pltpu.VMEM((kk, tn), BF16) for _ in a_list] if cast_weights else [],
        compiler_params=_params("parallel", "arbitrary"),
        name="proj",
    )(*args)


def _hgrn_maps(c):
    levels = int(np.log2(c))
    assert 2 ** levels == c
    t = np.arange(c)[:, None]
    s = np.arange(c)[None, :]
    mats = [(s <= t), (s > t)]
    for l in range(levels):
        b = 2 ** l
        start = (t // (2 * b)) * (2 * b)
        upper = (t // b) % 2 == 1
        mats.append((upper & (s >= start + b) & (s <= t)) | ((~upper) & (s > t) & (s <= start + b - 1)))
    return np.concatenate(mats, axis=0).astype(np.float32), levels


def _hgrn_body(q_ref, z_ref, v_ref, g_ref, lb_ref, nw_ref, a_ref, s0_ref, o_ref, sout_ref, st_scr,
               *, c, n_chunks, levels, heads):
    tb = pl.program_id(2)

    @pl.when(tb == 0)
    def _():
        for hh in range(heads):
            st_scr[hh] = s0_ref[0, hh].T

    nw = nw_ref[...]
    amat = a_ref[...]
    row = lax.broadcasted_iota(jnp.int32, (c, c), 0)
    col = lax.broadcasted_iota(jnp.int32, (c, c), 1)
    xor = row ^ col

    units = [(slice(ci * c, (ci + 1) * c), slice(hh * HEAD_DIM, (hh + 1) * HEAD_DIM))
             for hh in range(heads) for ci in range(n_chunks)]
    nu = len(units)
    lbs = [lb_ref[:, cols] for _, cols in units]
    qs = [_silu(q_ref[rows, cols]) for rows, cols in units]
    sigs = [_sigmoid_pair(z_ref[rows, cols]) for rows, cols in units]
    gs = [jnp.log(lbs[u] + (1.0 - lbs[u]) * sigs[u][0]) for u in range(nu)]
    ks = [(1.0 - lbs[u]) * sigs[u][1] for u in range(nu)]
    vs = [v_ref[rows, cols] for rows, cols in units]
    vbs = [v.astype(BF16) for v in vs]
    es = [jnp.exp(_dot_exact_lhs01(amat, g)) for g in gs]
    atts = [None] * nu
    for l in range(levels - 1, -1, -1):
        for u in range(nu):
            el = es[u][(2 + l) * c:(3 + l) * c]
            al = _dot_nt((qs[u] * el).astype(BF16), (ks[u] * el).astype(BF16))
            atts[u] = al if atts[u] is None else jnp.where(xor < 2 ** (l + 1), al, atts[u])
    atts = [jnp.where(row > col, att, 0.0).astype(BF16) for att in atts]
    o_intras = [_dot(atts[u], vbs[u]) + jnp.sum(qs[u] * ks[u], axis=-1, keepdims=True) * vs[u] for u in range(nu)]
    upds = [_dot_tn(vbs[u], (ks[u] * es[u][c:2 * c]).astype(BF16)) for u in range(nu)]
    qgs = [(qs[u] * es[u][0:c]).astype(BF16) for u in range(nu)]

    for hh in range(heads):
        st = st_scr[hh]
        for ci in range(n_chunks):
            u = hh * n_chunks + ci
            rows, cols = units[u]
            o = _dot_nt(qgs[u], st.astype(BF16)) + o_intras[u]
            st = st * es[u][c - 1:c] + upds[u]
            ms = jnp.mean(o * o, axis=-1, keepdims=True)
            y = o * lax.rsqrt(ms + EPS) * nw
            o_ref[rows, cols] = (y * _silu(g_ref[rows, cols])).astype(o_ref.dtype)
        st_scr[hh] = st

    @pl.when(tb == pl.num_programs(2) - 1)
    def _():
        for hh in range(heads):
            sout_ref[0, hh] = st_scr[hh].T


def hgrn2(qa, fa, ia, ga, lb, a_norm_w, s0, batch, seq, rows_per_step, c, heads_per_step, out_dtype):
    n, width = qa.shape
    h = width // HEAD_DIM
    assert n == batch * seq and seq % rows_per_step == 0 and rows_per_step % c == 0 and h % heads_per_step == 0
    nb = seq // rows_per_step
    gw = heads_per_step * HEAD_DIM
    amat_np, levels = _hgrn_maps(c)
    amat = jnp.asarray(amat_np, BF16)
    row_spec = pl.BlockSpec((rows_per_step, gw), lambda b, hh, t: (b * nb + t, hh))
    state_spec = pl.BlockSpec((1, heads_per_step, HEAD_DIM, HEAD_DIM), lambda b, hh, t: (b, hh, 0, 0))
    body = functools.partial(_hgrn_body, c=c, n_chunks=rows_per_step // c, levels=levels, heads=heads_per_step)
    return pl.pallas_call(
        body,
        grid=(batch, h // heads_per_step, nb),
        in_specs=[row_spec, row_spec, row_spec, row_spec,
                  pl.BlockSpec((1, gw), lambda b, hh, t: (0, hh)),
                  pl.BlockSpec((1, HEAD_DIM), lambda b, hh, t: (0, 0)),
                  pl.BlockSpec(amat.shape, lambda b, hh, t: (0, 0)),
                  state_spec],
        out_specs=[row_spec, state_spec],
        out_shape=[jax.ShapeDtypeStruct((n, width), out_dtype),
                   jax.ShapeDtypeStruct((batch, h, HEAD_DIM, HEAD_DIM), F32)],
        scratch_shapes=[pltpu.VMEM((heads_per_step, HEAD_DIM, HEAD_DIM), F32)],
        compiler_params=_params("parallel", "parallel", "arbitrary"),
        name="hgrn2",
    )(qa, fa, ia, ga, lb.reshape(1, width), a_norm_w.reshape(1, HEAD_DIM), amat, s0)


def _neg_abs(x):
    return pltpu.bitcast(pltpu.bitcast(x, jnp.uint32) | jnp.uint32(0x80000000), F32)


def _sb_prompt_body(q_ref, k_ref, v_ref, g_ref, uu_ref, o_ref, run_scr, acc_scr, w_scr, *, tq, tk, rc):
    i = pl.program_id(1)
    nd = tq // tk
    nl = tk // LANES
    q0 = pl.multiple_of(i * tq, tq)
    run_scr[...] = jnp.zeros(run_scr.shape, F32)
    acc_scr[...] = jnp.zeros(acc_scr.shape, F32)
    w_scr[1] = jnp.zeros(w_scr.shape[1:], BF16)

    def apply_pending(slot, k_prev):
        vt = v_ref[pl.ds(pl.multiple_of(k_prev, tk), tk), :]
        for r0 in range(0, tq, rc):
            rows = slice(r0, r0 + rc)
            acc_scr[rows, :] = acc_scr[rows, :] + _dot(w_scr[slot, rows, :], vt)

    def score(slot, k0, masked):
        kt = k_ref[pl.ds(pl.multiple_of(k0, tk), tk), :]
        for r0 in range(0, tq, rc):
            rows = slice(r0, r0 + rc)
            z = _dot_nt(q_ref[rows, :], kt)
            if masked:
                qpos = lax.broadcasted_iota(jnp.int32, (rc, LANES), 0) + (q0 + r0)
                kpos = lax.broadcasted_iota(jnp.int32, (rc, LANES), 1) + k0
            zs, sps, his, los, valids = [], [], [], [], []
            rowsum = None
            for c in range(nl):
                zc = z[:, c * LANES:(c + 1) * LANES]
                sp = jnp.maximum(zc, 0.0) + jnp.log2(1.0 + jnp.exp2(_neg_abs(zc)))
                if masked:
                    valid = (kpos + c * LANES) < qpos
                    sp = jnp.where(valid, sp, 0.0)
                    valids.append(valid)
                hi = sp.astype(BF16)
                lo = (sp - hi.astype(F32)).astype(BF16)
                zs.append(zc); sps.append(sp); his.append(hi); los.append(lo)
                rowsum = sp if rowsum is None else rowsum + sp
            tail = _dot(jnp.concatenate(his + los, axis=1), uu_ref[...])
            run = run_scr[rows, :]
            for c in range(nl):
                w = jnp.exp2(zs[c] - (sps[c] + tail[:, c * LANES:(c + 1) * LANES] + run))
                if masked:
                    w = jnp.where(valids[c], w, 0.0)
                w_scr[slot, rows, c * LANES:(c + 1) * LANES] = w.astype(BF16)
            run_scr[rows, :] = run + jnp.sum(rowsum, axis=-1, keepdims=True)

    def pair(k_pending, k_a, k_b, masked):
        score(0, k_a, masked)
        apply_pending(1, k_pending)
        score(1, k_b, masked)
        apply_pending(0, k_a)

    assert nd % 2 == 0
    k_pending = q0
    for d in range(nd - 1, 0, -2):
        pair(k_pending, q0 + d * tk, q0 + (d - 1) * tk, True)
        k_pending = q0 + (d - 1) * tk
    n_pairs = (i * nd) // 2

    def before(jj, kp):
        k_a = q0 - (2 * jj + 1) * tk
        pair(kp, k_a, k_a - tk, False)
        return k_a - tk

    k_pending = lax.fori_loop(0, n_pairs, before, k_pending)
    apply_pending(1, k_pending)
    o_ref[...] = (acc_scr[...] * _silu(g_ref[...])).astype(o_ref.dtype)


def _tail_matrix(n):
    sp = np.arange(n)[:, None]
    s = np.arange(n)[None, :]
    return (sp > s).astype(np.float32)


def sb_prompt(q, k, v, gate, tq, tk, rc, out_dtype):
    t, width = q.shape
    h = width // HEAD_DIM
    u = _tail_matrix(tk)
    uu = jnp.asarray(np.concatenate([u, u], axis=0), BF16)
    q_spec = pl.BlockSpec((tq, HEAD_DIM), lambda hh, i: (i, hh))
    kv_spec = pl.BlockSpec((t, HEAD_DIM), lambda hh, i: (0, hh))
    return pl.pallas_call(
        functools.partial(_sb_prompt_body, tq=tq, tk=tk, rc=rc),
        grid=(h, t // tq),
        in_specs=[q_spec, kv_spec, kv_spec, q_spec, pl.BlockSpec(uu.shape, lambda hh, i: (0, 0))],
        out_specs=q_spec,
        out_shape=jax.ShapeDtypeStruct((t, width), out_dtype),
        scratch_shapes=[pltpu.VMEM((tq, LANES), F32), pltpu.VMEM((tq, HEAD_DIM), F32),
                        pltpu.VMEM((2, tq, tk), BF16)],
        compiler_params=_params("parallel", "arbitrary"),
        name="sb_prompt",
    )(q, k, v, gate, uu)


def _logf_cumsum_body(fl_ref, b_ref, tri_ref, lf_ref, f2_ref, carry_scr):
    @pl.when(pl.program_id(0) == 0)
    def _():
        carry_scr[...] = jnp.zeros_like(carry_scr)

    lf = _log_sigmoid(fl_ref[...] + b_ref[...])
    lf_ref[...] = lf
    f = carry_scr[...] + _dot_exact_lhs01(tri_ref[...], lf)
    f2_ref[...] = f * LOG2E
    carry_scr[...] = f[f.shape[0] - 1:, :]


def logf_cumsum(fl, b_forget, blk):
    t, h = fl.shape
    tri = jnp.asarray(np.tril(np.ones((blk, blk), np.float32)), BF16)
    spec = pl.BlockSpec((blk, h), lambda i: (i, 0))
    return pl.pallas_call(
        _logf_cumsum_body,
        grid=(t // blk,),
        in_specs=[spec, pl.BlockSpec((1, h), lambda i: (0, 0)), pl.BlockSpec((blk, blk), lambda i: (0, 0))],
        out_specs=[spec, spec],
        out_shape=[jax.ShapeDtypeStruct((t, h), F32), jax.ShapeDtypeStruct((t, h), F32)],
        scratch_shapes=[pltpu.VMEM((1, h), F32)],
        compiler_params=_params("arbitrary"),
        name="logf_cumsum",
    )(fl, b_forget.reshape(1, h), tri)


def _logf_body(fl_ref, b_ref, lf_ref):
    lf_ref[...] = _log_sigmoid(fl_ref[...] + b_ref[...])


def logf_only(fl, b_forget):
    t, h = fl.shape
    return pl.pallas_call(
        _logf_body,
        grid=(1,),
        in_specs=[pl.BlockSpec((t, h), lambda i: (0, 0)), pl.BlockSpec((1, h), lambda i: (0, 0))],
        out_specs=pl.BlockSpec((t, h), lambda i: (0, 0)),
        out_shape=jax.ShapeDtypeStruct((t, h), F32),
        name="logf",
    )(fl, b_forget.reshape(1, h))


_BIAS_PIECES = 3


def _bias_selectors(h):
    sel = np.zeros((2, h, _BIAS_PIECES, h, LANES), np.float32)
    for hh in range(h):
        for p in range(_BIAS_PIECES):
            sel[0, hh, p, hh, p] = 1.0
            sel[1, hh, p, hh, _BIAS_PIECES + p] = -1.0
    return sel


def _bias_columns(f, sel_ref, query_side):
    out = None
    for p, piece in enumerate(_split3(f)):
        d = _dot(piece, sel_ref[0, p])
        out = d if out is None else out + d
    lane = lax.broadcasted_iota(jnp.int32, out.shape, 1)
    ones_at = (lane >= _BIAS_PIECES) & (lane < 2 * _BIAS_PIECES) if query_side else lane < _BIAS_PIECES
    return jnp.where(ones_at, 1.0, out).astype(BF16)


def _fox_prompt_body(q_ref, k_ref, v_ref, g_ref, f_ref, selq_ref, selk_ref, o_ref,
                     kx_scr, vx_scr, m_scr, acc_scr, p_scr, alpha_scr, *, tq, tk, rc):
    i = pl.program_id(1)
    nd = tq // tk
    nl = tk // LANES
    t_all = k_ref.shape[0]

    @pl.when(i == 0)
    def _():
        kx_scr[:, 0:HEAD_DIM] = k_ref[...]
        kx_scr[:, HEAD_DIM:] = _bias_columns(f_ref[...], selk_ref, False)
        vx_scr[:, 0:HEAD_DIM] = v_ref[...]
        vx_scr[:, HEAD_DIM:] = jnp.ones((t_all, LANES), BF16)

    q0 = pl.multiple_of(i * tq, tq)
    qx = jnp.concatenate([q_ref[...], _bias_columns(f_ref[pl.ds(q0, tq), :], selq_ref, True)], axis=1)
    m_scr[...] = jnp.full(m_scr.shape, -jnp.inf, F32)
    acc_scr[...] = jnp.zeros(acc_scr.shape, F32)
    p_scr[...] = jnp.zeros(p_scr.shape, BF16)
    alpha_scr[...] = jnp.ones(alpha_scr.shape, F32)

    def apply_pending(slot, k_prev):
        vxt = vx_scr[pl.ds(pl.multiple_of(k_prev, tk), tk), :]
        for r0 in range(0, tq, rc):
            rows = slice(r0, r0 + rc)
            pv = _dot(p_scr[slot, rows, :], vxt)
            alpha = alpha_scr[slot, rows, :]
            acc_scr[rows, 0:HEAD_DIM] = alpha * acc_scr[rows, 0:HEAD_DIM] + pv[:, 0:HEAD_DIM]
            acc_scr[rows, HEAD_DIM:] = alpha * acc_scr[rows, HEAD_DIM:] + pv[:, HEAD_DIM:]

    def score(slot, k0, masked):
        kxt = kx_scr[pl.ds(pl.multiple_of(k0, tk), tk), :]
        for r0 in range(0, tq, rc):
            rows = slice(r0, r0 + rc)
            s = _dot_nt(qx[r0:r0 + rc, :], kxt)
            sb = [s[:, c * LANES:(c + 1) * LANES] for c in range(nl)]
            if masked:
                qpos = lax.broadcasted_iota(jnp.int32, (rc, LANES), 0) + (q0 + r0)
                kpos = lax.broadcasted_iota(jnp.int32, (rc, LANES), 1) + k0
                sb = [jnp.where((kpos + c * LANES) <= qpos, sb[c], -jnp.inf) for c in range(nl)]
            mx = sb[0]
            for c in range(1, nl):
                mx = jnp.maximum(mx, sb[c])
            m_old = m_scr[rows, :]
            m_new = jnp.maximum(m_old, jnp.max(mx, axis=-1, keepdims=True))
            alpha_scr[slot, rows, :] = jnp.exp2(m_old - m_new)
            for c in range(nl):
                p_scr[slot, rows, c * LANES:(c + 1) * LANES] = jnp.exp2(sb[c] - m_new).astype(BF16)
            m_scr[rows, :] = m_new

    def pair(k_pending, k_a, k_b, masked):
        score(0, k_a, masked)
        apply_pending(1, k_pending)
        score(1, k_b, masked)
        apply_pending(0, k_a)

    assert nd % 2 == 0
    k_pending = q0
    for d in range(0, nd, 2):
        pair(k_pending, q0 + d * tk, q0 + (d + 1) * tk, True)
        k_pending = q0 + (d + 1) * tk

    def before(j, kp):
        pair(kp, 2 * j * tk, (2 * j + 1) * tk, False)
        return (2 * j + 1) * tk

    k_pending = lax.fori_loop(0, (i * nd) // 2, before, k_pending)
    apply_pending(1, k_pending)
    o_ref[...] = (acc_scr[:, 0:HEAD_DIM] / acc_scr[:, HEAD_DIM:] * _silu(g_ref[...])).astype(o_ref.dtype)


def fox_prompt(q, k, v, gate, f2, tq, tk, rc, out_dtype):
    t, width = q.shape
    h = width // HEAD_DIM
    sel = jnp.asarray(_bias_selectors(h), BF16)
    q_spec = pl.BlockSpec((tq, HEAD_DIM), lambda hh, i: (i, hh))
    kv_spec = pl.BlockSpec((t, HEAD_DIM), lambda hh, i: (0, hh))
    sel_spec = pl.BlockSpec((1, _BIAS_PIECES, h, LANES), lambda hh, i: (hh, 0, 0, 0))
    return pl.pallas_call(
        functools.partial(_fox_prompt_body, tq=tq, tk=tk, rc=rc),
        grid=(h, t // tq),
        in_specs=[q_spec, kv_spec, kv_spec, q_spec, pl.BlockSpec((t, h), lambda hh, i: (0, 0)),
                  sel_spec, sel_spec],
        out_specs=q_spec,
        out_shape=jax.ShapeDtypeStruct((t, width), out_dtype),
        scratch_shapes=[pltpu.VMEM((t, 2 * HEAD_DIM), BF16), pltpu.VMEM((t, 2 * HEAD_DIM), BF16),
                        pltpu.VMEM((tq, LANES), F32), pltpu.VMEM((tq, 2 * HEAD_DIM), F32),
                        pltpu.VMEM((2, tq, tk), BF16), pltpu.VMEM((2, tq, LANES), F32)],
        compiler_params=_params("parallel", "arbitrary"),
        name="fox_prompt",
    )(q, k, v, gate, f2, sel[0], sel[1])


HEADS_PER_GROUP = 8


def _expand_queries(q, tq):
    gw = q.shape[1]
    rep = jnp.concatenate([q] * HEADS_PER_GROUP, axis=0)
    r = lax.broadcasted_iota(jnp.int32, (HEADS_PER_GROUP * tq, gw), 0) // tq
    cidx = lax.broadcasted_iota(jnp.int32, (HEADS_PER_GROUP * tq, gw), 1) // HEAD_DIM
    return jnp.where(r == cidx, rep, 0.0)


def _collect_heads(full, tq):
    return jnp.concatenate(
        [full[hh * tq:(hh + 1) * tq, hh * HEAD_DIM:(hh + 1) * HEAD_DIM] for hh in range(HEADS_PER_GROUP)],
        axis=1)


def _sb_decode_body(q_ref, kn_ref, vn_ref, kc_ref, vc_ref, g_ref, u_ref, o_ref, k_scr, v_scr, *, past, tq):
    tk = past + tq
    k_scr[0:past, :] = kc_ref[0].astype(BF16)
    k_scr[past:tk, :] = kn_ref[...].astype(BF16)
    v_scr[0:past, :] = vc_ref[0].astype(BF16)
    v_scr[past:tk, :] = vn_ref[...].astype(BF16)

    lanes = HEADS_PER_GROUP * tq
    qx = _expand_queries(q_ref[...] * (HEAD_DIM ** -0.5), tq).astype(BF16)
    z = _dot_nt(k_scr[...], qx)
    kpos = lax.broadcasted_iota(jnp.int32, (tk, lanes), 0)
    qpos = past + lax.broadcasted_iota(jnp.int32, (tk, lanes), 1) % tq
    valid = kpos < qpos
    ls_neg = -(jnp.maximum(z, 0.0) + jnp.log(1.0 + jnp.exp(-jnp.abs(z))))
    lm = jnp.where(valid, ls_neg, 0.0)
    hi = lm.astype(BF16)
    lo = (lm - hi.astype(F32)).astype(BF16)
    umat = u_ref[...]
    tail = _dot(umat, hi) + _dot(umat, lo)
    w = jnp.where(valid, jnp.exp(z + ls_neg + tail), 0.0)
    full = _dot_tn(w.astype(BF16), v_scr[...])
    o_ref[...] = (_collect_heads(full, tq) * _silu(g_ref[...])).astype(o_ref.dtype)


def sb_decode(q, k_new, v_new, k_cache, v_cache, gate, batch, tq, out_dtype):
    n, width = q.shape
    past = k_cache.shape[1]
    tk = past + tq
    gw = HEADS_PER_GROUP * HEAD_DIM
    groups = width // gw
    umat = jnp.asarray(_tail_matrix(tk).T, BF16)
    row_spec = pl.BlockSpec((tq, gw), lambda b, g: (b, g))
    cache_spec = pl.BlockSpec((1, past, gw), lambda b, g: (b, 0, g))
    return pl.pallas_call(
        functools.partial(_sb_decode_body, past=past, tq=tq),
        grid=(batch, groups),
        in_specs=[row_spec, row_spec, row_spec, cache_spec, cache_spec, row_spec,
                  pl.BlockSpec((tk, tk), lambda b, g: (0, 0))],
        out_specs=row_spec,
        out_shape=jax.ShapeDtypeStruct((n, width), out_dtype),
        scratch_shapes=[pltpu.VMEM((tk, gw), BF16), pltpu.VMEM((tk, gw), BF16)],
        compiler_params=_params("parallel", "parallel"),
        name="sb_decode",
    )(q, k_new, v_new, k_cache, v_cache, gate, umat)


def _fox_decode_body(q_ref, kn_ref, vn_ref, kc_ref, vc_ref, g_ref, lfn_ref, lfc_ref, ex_ref, tri_ref, o_ref,
                     k_scr, v_scr, lf_scr, *, past, tq):
    tk = past + tq
    k_scr[0:past, :] = kc_ref[0].astype(BF16)
    k_scr[past:tk, :] = kn_ref[...].astype(BF16)
    v_scr[0:past, :] = vc_ref[0].astype(BF16)
    v_scr[past:tk, :] = vn_ref[...].astype(BF16)
    lf_scr[0:past, :] = lfc_ref[0]
    lf_scr[past:tk, :] = lfn_ref[...]

    lanes = HEADS_PER_GROUP * tq
    hi, mid, lo = _split3(lf_scr[...])
    ex = ex_ref[0]
    lfx = _dot(hi, ex) + _dot(mid, ex) + _dot(lo, ex)
    f_key = _dot_exact_lhs01(tri_ref[...], lfx)
    kpos = lax.broadcasted_iota(jnp.int32, (tk, lanes), 0)
    qpos = past + lax.broadcasted_iota(jnp.int32, (tk, lanes), 1) % tq
    f_query = jnp.sum(jnp.where(kpos == qpos, f_key, 0.0), axis=0, keepdims=True)

    qx = _expand_queries(q_ref[...] * (HEAD_DIM ** -0.5), tq).astype(BF16)
    s = _dot_nt(k_scr[...], qx) + (f_query - f_key)
    s = jnp.where(kpos <= qpos, s, -jnp.inf)
    p = jnp.exp(s - jnp.max(s, axis=0, keepdims=True))
    p = p / jnp.sum(p, axis=0, keepdims=True)
    full = _dot_tn(p.astype(BF16), v_scr[...])
    o_ref[...] = (_collect_heads(full, tq) * _silu(g_ref[...])).astype(o_ref.dtype)


def fox_decode(q, k_new, v_new, k_cache, v_cache, gate, lf_new, lf_cache, batch, tq, out_dtype):
    n, width = q.shape
    h = width // HEAD_DIM
    past = k_cache.shape[1]
    tk = past + tq
    gw = HEADS_PER_GROUP * HEAD_DIM
    groups = width // gw
    lanes = HEADS_PER_GROUP * tq
    tri = jnp.asarray(np.tril(np.ones((tk, tk), np.float32)), BF16)
    head_of_lane = np.arange(lanes)[None, None, :] // tq + HEADS_PER_GROUP * np.arange(groups)[:, None, None]
    expand = jnp.asarray((np.arange(h)[None, :, None] == head_of_lane).astype(np.float32), BF16)
    row_spec = pl.BlockSpec((tq, gw), lambda b, g: (b, g))
    cache_spec = pl.BlockSpec((1, past, gw), lambda b, g: (b, 0, g))
    return pl.pallas_call(
        functools.partial(_fox_decode_body, past=past, tq=tq),
        grid=(batch, groups),
        in_specs=[row_spec, row_spec, row_spec, cache_spec, cache_spec, row_spec,
                  pl.BlockSpec((tq, h), lambda b, g: (b, 0)),
                  pl.BlockSpec((1, past, h), lambda b, g: (b, 0, 0)),
                  pl.BlockSpec((1, h, lanes), lambda b, g: (g, 0, 0)),
                  pl.BlockSpec((tk, tk), lambda b, g: (0, 0))],
        out_specs=row_spec,
        out_shape=jax.ShapeDtypeStruct((n, width), out_dtype),
        scratch_shapes=[pltpu.VMEM((tk, gw), BF16), pltpu.VMEM((tk, gw), BF16), pltpu.VMEM((tk, h), F32)],
        compiler_params=_params("parallel", "parallel"),
        name="fox_decode",
    )(q, k_new, v_new, k_cache, v_cache, gate, lf_new, lf_cache, expand, tri)


QK_SCALE_LOG2 = LOG2E * HEAD_DIM ** -0.5
FLAT32 = (False, F32, 1.0)
FLAT16 = (False, BF16, 1.0)
HEADS32 = (True, F32, 1.0)
QUERY16 = (False, BF16, QK_SCALE_LOG2)


def _tile(m, pref):
    return pref if m % pref == 0 else m


def _even_layer(x, norm_w, w_in, w_out, lb, a_norm_w, s0, caches, batch, seq, prompt):
    n, d = x.shape
    half = d // 2
    tm = _tile(n, 1024)
    hn = rmsnorm(x, norm_w, BF16, _tile(n, 256))

    def p(group, *outs):
        return proj([hn], w_in, group * half, half, tm, 512, outs)

    def p_cache(group, *outs):
        return proj([hn], w_in[:, group * half:(group + 1) * half].astype(BF16), 0, half, tm, 1024, outs)

    (qa,), (fa,), (ia,), (ga,), (gb,) = p(0, FLAT32), p(1, FLAT32), p(2, FLAT32), p(3, FLAT32), p(7, FLAT32)
    if prompt:
        (qb,) = p(4, QUERY16)
        kb_leaf, kb = p_cache(5, HEADS32, FLAT16)
        vb_leaf, vb = p_cache(6, HEADS32, FLAT16)
        oa, s_new = hgrn2(qa, fa, ia, ga, lb, a_norm_w, s0, batch, seq, 1024, HGRN_CHUNK, 1, BF16)
        ob = sb_prompt(qb, kb, vb, gb, 1024, 256, 1024, BF16)
    else:
        (qb,) = p(4, FLAT32)
        kb_leaf, kb = p_cache(5, HEADS32, FLAT32)
        vb_leaf, vb = p_cache(6, HEADS32, FLAT32)
        oa, s_new = hgrn2(qa, fa, ia, ga, lb, a_norm_w, s0, batch, seq, seq, seq, half // HEAD_DIM, BF16)
        ob = sb_decode(qb, kb, vb, caches[0], caches[1], gb, batch, seq, BF16)
    (y,) = proj([oa, ob], w_out, 0, d, tm, 512, (FLAT32,), residual=x)
    return y, s_new, kb_leaf, vb_leaf


def _odd_layer(x, norm_w, w_in, b_forget, w_out, caches, batch, seq, prompt):
    n, d = x.shape
    heads = d // HEAD_DIM
    tm = _tile(n, 1024)
    hn = rmsnorm(x, norm_w, BF16, _tile(n, 256))

    def p(group, *outs):
        return proj([hn], w_in, group * d, d, tm, 512, outs)

    def p_cache(group, *outs):
        return proj([hn], w_in[:, group * d:(group + 1) * d].astype(BF16), 0, d, tm, 1024, outs)

    (gate,) = p(3, FLAT32)
    (fl,) = proj([hn], w_in[:, 4 * d:], 0, heads, tm, heads, (FLAT32,))
    if prompt:
        (q,) = p(0, QUERY16)
        k_leaf, k = p_cache(1, HEADS32, FLAT16)
        v_leaf, v = p_cache(2, HEADS32, FLAT16)
        logf, f2 = logf_cumsum(fl, b_forget, 512)
        o = fox_prompt(q, k, v, gate, f2, 1024, 512, 256, BF16)
    else:
        (q,) = p(0, FLAT32)
        k_leaf, k = p_cache(1, HEADS32, FLAT32)
        v_leaf, v = p_cache(2, HEADS32, FLAT32)
        logf = logf_only(fl, b_forget)
        o = fox_decode(q, k, v, caches[0], caches[1], gate, logf, caches[2], batch, seq, BF16)
    (y,) = proj([o], w_out, 0, d, tm, 512, (FLAT32,), residual=x)
    return y, k_leaf, v_leaf, logf


def kernel(x_prompt, x_sample, state_a_hgrn, cache_b_k, cache_b_v, cache_c_k, cache_c_v, cache_c_logf,
           norm_w, final_norm_w, w_in_even, w_out_even, lb_logits, a_norm_w, w_in_odd, b_forget, w_out_odd):
    bp, tp, d = x_prompt.shape
    bs, ts, _ = x_sample.shape
    depth = norm_w.shape[0]
    n_even = w_in_even.shape[0]
    past = cache_b_k.shape[2]
    lb_all = jnp.cumsum(jax.nn.softmax(lb_logits.astype(F32), axis=0), axis=0)[:n_even]

    hp = x_prompt.reshape(bp * tp, d)
    hs = x_sample.reshape(bs * ts, d)
    outs = {name: [] for name in ("sa_p", "sa_s", "bk_p", "bv_p", "bk_s", "bv_s",
                                  "ck_p", "cv_p", "cf_p", "ck_s", "cv_s", "cf_s")}
    for layer in range(depth):
        j = layer // 2
        if layer % 2 == 0:
            a_heads = state_a_hgrn.shape[2]
            b_heads = cache_b_k.shape[3]
            zeros = jnp.zeros((bp, a_heads) + state_a_hgrn.shape[3:], F32)
            hp, sp, kp, vp = _even_layer(hp, norm_w[layer], w_in_even[j], w_out_even[j], lb_all[j], a_norm_w[j],
                                         zeros, None, bp, tp, True)
            ck = cache_b_k[j].reshape(bs, past, -1)
            cv = cache_b_v[j].reshape(bs, past, -1)
            hs, ss, ksm, vsm = _even_layer(hs, norm_w[layer], w_in_even[j], w_out_even[j], lb_all[j], a_norm_w[j],
                                           state_a_hgrn[j], (ck, cv), bs, ts, False)
            outs["sa_p"].append(sp); outs["sa_s"].append(ss)
            outs["bk_p"].append(kp.reshape(bp, tp, b_heads, HEAD_DIM))
            outs["bv_p"].append(vp.reshape(bp, tp, b_heads, HEAD_DIM))
            outs["bk_s"].append(ksm.reshape(bs, ts, b_heads, HEAD_DIM))
            outs["bv_s"].append(vsm.reshape(bs, ts, b_heads, HEAD_DIM))
        else:
            c_heads = cache_c_k.shape[3]
            hp, kp, vp, fp = _odd_layer(hp, norm_w[layer], w_in_odd[j], b_forget[j], w_out_odd[j], None, bp, tp, True)
            caches = (cache_c_k[j].reshape(bs, past, -1), cache_c_v[j].reshape(bs, past, -1), cache_c_logf[j])
            hs, ksm, vsm, fsm = _odd_layer(hs, norm_w[layer], w_in_odd[j], b_forget[j], w_out_odd[j], caches, bs, ts, False)
            outs["ck_p"].append(kp.reshape(bp, tp, c_heads, HEAD_DIM))
            outs["cv_p"].append(vp.reshape(bp, tp, c_heads, HEAD_DIM))
            outs["cf_p"].append(fp.reshape(bp, tp, c_heads))
            outs["ck_s"].append(ksm.reshape(bs, ts, c_heads, HEAD_DIM))
            outs["cv_s"].append(vsm.reshape(bs, ts, c_heads, HEAD_DIM))
            outs["cf_s"].append(fsm.reshape(bs, ts, c_heads))
    y_prompt = rmsnorm(hp, final_norm_w, F32, _tile(bp * tp, 256)).reshape(bp, tp, d)
    y_sample = rmsnorm(hs, final_norm_w, F32, _tile(bs * ts, 256)).reshape(bs, ts, d)
    st = {k: jnp.stack(v) for k, v in outs.items()}
    return (y_prompt, y_sample, st["sa_p"], st["sa_s"], st["bk_p"], st["bv_p"], st["bk_s"], st["bv_s"],
            st["ck_p"], st["cv_p"], st["cf_p"], st["ck_s"], st["cv_s"], st["cf_s"])
```

```python
import functools

import numpy as np
import jax
import jax.numpy as jnp
from jax import lax
from jax.experimental import pallas as pl
from jax.experimental.pallas import tpu as pltpu

F32 = jnp.float32
BF16 = jnp.bfloat16

EPS = 1e-6
HEAD_DIM = 128
LANES = 128
LOG2E = 1.4426950408889634
HGRN_CHUNK = 64
VMEM_LIMIT_BYTES = 56 * 1024 * 1024

_NT = (((1,), (1,)), ((), ()))
_TN = (((0,), (0,)), ((), ()))


def _params(*sem):
    return pltpu.CompilerParams(dimension_semantics=sem, vmem_limit_bytes=VMEM_LIMIT_BYTES)


def _dot(a, b):
    return jnp.dot(a, b, preferred_element_type=F32)


def _dot_nt(a, b):
    return lax.dot_general(a, b, _NT, preferred_element_type=F32)


def _dot_tn(a, b):
    return lax.dot_general(a, b, _TN, preferred_element_type=F32)


def _split3(x):
    hi = x.astype(BF16)
    r1 = x - hi.astype(F32)
    mid = r1.astype(BF16)
    lo = (r1 - mid.astype(F32)).astype(BF16)
    return hi, mid, lo


def _dot_exact_lhs01(a01, x):
    hi, mid, lo = _split3(x)
    return _dot(a01, hi) + _dot(a01, mid) + _dot(a01, lo)


def _sigmoid_pair(z):
    e = jnp.exp(-jnp.abs(z))
    r = 1.0 / (1.0 + e)
    er = e * r
    pos = z >= 0
    return jnp.where(pos, r, er), jnp.where(pos, er, r)


def _silu(x):
    return x * _sigmoid_pair(x)[0]


def _log_sigmoid(x):
    return jnp.minimum(x, 0.0) - jnp.log(1.0 + jnp.exp(-jnp.abs(x)))


def _rmsnorm_body(x_ref, w_ref, o_ref):
    x = x_ref[...]
    ms = jnp.mean(x * x, axis=-1, keepdims=True)
    o_ref[...] = (x * lax.rsqrt(ms + EPS) * w_ref[...]).astype(o_ref.dtype)


def rmsnorm(x, w, out_dtype, tm):
    m, d = x.shape
    return pl.pallas_call(
        _rmsnorm_body,
        grid=(m // tm,),
        in_specs=[pl.BlockSpec((tm, d), lambda i: (i, 0)), pl.BlockSpec((1, d), lambda i: (0, 0))],
        out_specs=pl.BlockSpec((tm, d), lambda i: (i, 0)),
        out_shape=jax.ShapeDtypeStruct((m, d), out_dtype),
        compiler_params=_params("parallel"),
        name="rmsnorm",
    )(x, w.reshape(1, d))


def _round_body(w_ref, o_ref):
    o_ref[...] = w_ref[...].astype(o_ref.dtype)


def round_weights(w, layer, rows):
    _, k, n = w.shape
    return pl.pallas_call(
        _round_body,
        grid=(k // rows,),
        in_specs=[pl.BlockSpec((None, rows, n), lambda i: (layer, i, 0))],
        out_specs=pl.BlockSpec((rows, n), lambda i: (i, 0)),
        out_shape=jax.ShapeDtypeStruct((k, n), BF16),
        compiler_params=_params("parallel"),
        name="round_weights",
    )(w)


def _proj_body(*refs, n_in, has_residual, outs):
    a_refs = refs[:n_in]
    w_refs = refs[n_in:2 * n_in]
    pos = 2 * n_in
    r_ref = refs[pos] if has_residual else None
    o_refs = refs[pos + int(has_residual):]
    acc = None
    for a_ref, w_ref in zip(a_refs, w_refs):
        d = _dot(a_ref[...], w_ref[...])
        acc = d if acc is None else acc + d
    if has_residual:
        acc = r_ref[...] + acc
    for o_ref, (by_head, _, scale) in zip(o_refs, outs):
        val = acc if scale == 1.0 else acc * scale
        if by_head:
            for hh in range(o_ref.shape[1]):
                o_ref[:, hh, :] = val[:, hh * HEAD_DIM:(hh + 1) * HEAD_DIM].astype(o_ref.dtype)
        else:
            o_ref[...] = val.astype(o_ref.dtype)


def proj(a_list, w, col0, ncols, tm, tn, outs, residual=None):
    m = a_list[0].shape[0]
    kk = a_list[0].shape[1]
    assert all(a.shape == (m, kk) for a in a_list) and w.shape[0] == kk * len(a_list)
    assert m % tm == 0 and ncols % tn == 0 and col0 % tn == 0
    cb = col0 // tn
    in_specs = [pl.BlockSpec((tm, kk), lambda i, j: (i, 0)) for _ in a_list]
    in_specs += [pl.BlockSpec((kk, tn), functools.partial(lambda i, j, r: (r, cb + j), r=r)) for r in range(len(a_list))]
    args = list(a_list) + [w] * len(a_list)
    if residual is not None:
        in_specs.append(pl.BlockSpec((tm, tn), lambda i, j: (i, j)))
        args.append(residual)
    out_specs, out_shape = [], []
    for by_head, dtype, _ in outs:
        if by_head:
            out_specs.append(pl.BlockSpec((tm, tn // HEAD_DIM, HEAD_DIM), lambda i, j: (i, j, 0)))
            out_shape.append(jax.ShapeDtypeStruct((m, ncols // HEAD_DIM, HEAD_DIM), dtype))
        else:
            out_specs.append(pl.BlockSpec((tm, tn), lambda i, j: (i, j)))
            out_shape.append(jax.ShapeDtypeStruct((m, ncols), dtype))
    return pl.pallas_call(
        functools.partial(_proj_body, n_in=len(a_list), has_residual=residual is not None, outs=tuple(outs)),
        grid=(m // tm, ncols // tn),
        in_specs=in_specs,
        out_specs=out_specs,
        out_shape=out_shape,
        compiler_params=_params("parallel", "parallel"),
        name="proj",
    )(*args)


def _hgrn_maps(c):
    levels = int(np.log2(c))
    assert 2 ** levels == c
    t = np.arange(c)[:, None]
    s = np.arange(c)[None, :]
    mats = [(s <= t), (s > t)]
    for l in range(levels):
        b = 2 ** l
        start = (t // (2 * b)) * (2 * b)
        upper = (t // b) % 2 == 1
        mats.append((upper & (s >= start + b) & (s <= t)) | ((~upper) & (s > t) & (s <= start + b - 1)))
    return np.concatenate(mats, axis=0).astype(np.float32), levels


def _hgrn_body(q_ref, z_ref, v_ref, g_ref, lb_ref, nw_ref, a_ref, s0_ref, o_ref, sout_ref, st_scr,
               *, c, n_chunks, levels, heads):
    tb = pl.program_id(2)

    @pl.when(tb == 0)
    def _():
        for hh in range(heads):
            st_scr[hh] = s0_ref[0, hh].T

    nw = nw_ref[...]
    amat = a_ref[...]
    row = lax.broadcasted_iota(jnp.int32, (c, c), 0)
    col = lax.broadcasted_iota(jnp.int32, (c, c), 1)
    xor = row ^ col

    units = [(slice(ci * c, (ci + 1) * c), slice(hh * HEAD_DIM, (hh + 1) * HEAD_DIM))
             for hh in range(heads) for ci in range(n_chunks)]
    nu = len(units)
    lbs = [lb_ref[:, cols] for _, cols in units]
    qs = [_silu(q_ref[rows, cols]) for rows, cols in units]
    sigs = [_sigmoid_pair(z_ref[rows, cols]) for rows, cols in units]
    gs = [jnp.log(lbs[u] + (1.0 - lbs[u]) * sigs[u][0]) for u in range(nu)]
    ks = [(1.0 - lbs[u]) * sigs[u][1] for u in range(nu)]
    vs = [v_ref[rows, cols] for rows, cols in units]
    vbs = [v.astype(BF16) for v in vs]
    es = [jnp.exp(_dot_exact_lhs01(amat, g)) for g in gs]
    atts = [None] * nu
    for l in range(levels - 1, -1, -1):
        for u in range(nu):
            el = es[u][(2 + l) * c:(3 + l) * c]
            al = _dot_nt((qs[u] * el).astype(BF16), (ks[u] * el).astype(BF16))
            atts[u] = al if atts[u] is None else jnp.where(xor < 2 ** (l + 1), al, atts[u])
    atts = [jnp.where(row > col, att, 0.0).astype(BF16) for att in atts]
    o_intras = [_dot(atts[u], vbs[u]) + jnp.sum(qs[u] * ks[u], axis=-1, keepdims=True) * vs[u] for u in range(nu)]
    upds = [_dot_tn(vbs[u], (ks[u] * es[u][c:2 * c]).astype(BF16)) for u in range(nu)]
    qgs = [(qs[u] * es[u][0:c]).astype(BF16) for u in range(nu)]

    for hh in range(heads):
        st = st_scr[hh]
        for ci in range(n_chunks):
            u = hh * n_chunks + ci
            rows, cols = units[u]
            o = _dot_nt(qgs[u], st.astype(BF16)) + o_intras[u]
            st = st * es[u][c - 1:c] + upds[u]
            ms = jnp.mean(o * o, axis=-1, keepdims=True)
            y = o * lax.rsqrt(ms + EPS) * nw
            o_ref[rows, cols] = (y * _silu(g_ref[rows, cols])).astype(o_ref.dtype)
        st_scr[hh] = st

    @pl.when(tb == pl.num_programs(2) - 1)
    def _():
        for hh in range(heads):
            sout_ref[0, hh] = st_scr[hh].T


def hgrn2(qa, fa, ia, ga, lb, a_norm_w, s0, batch, seq, rows_per_step, c, heads_per_step, out_dtype):
    n, width = qa.shape
    h = width // HEAD_DIM
    assert n == batch * seq and seq % rows_per_step == 0 and rows_per_step % c == 0 and h % heads_per_step == 0
    nb = seq // rows_per_step
    gw = heads_per_step * HEAD_DIM
    amat_np, levels = _hgrn_maps(c)
    amat = jnp.asarray(amat_np, BF16)
    row_spec = pl.BlockSpec((rows_per_step, gw), lambda b, hh, t: (b * nb + t, hh))
    state_spec = pl.BlockSpec((1, heads_per_step, HEAD_DIM, HEAD_DIM), lambda b, hh, t: (b, hh, 0, 0))
    body = functools.partial(_hgrn_body, c=c, n_chunks=rows_per_step // c, levels=levels, heads=heads_per_step)
    return pl.pallas_call(
        body,
        grid=(batch, h // heads_per_step, nb),
        in_specs=[row_spec, row_spec, row_spec, row_spec,
                  pl.BlockSpec((1, gw), lambda b, hh, t: (0, hh)),
                  pl.BlockSpec((1, HEAD_DIM), lambda b, hh, t: (0, 0)),
                  pl.BlockSpec(amat.shape, lambda b, hh, t: (0, 0)),
                  state_spec],
        out_specs=[row_spec, state_spec],
        out_shape=[jax.ShapeDtypeStruct((n, width), out_dtype),
                   jax.ShapeDtypeStruct((batch, h, HEAD_DIM, HEAD_DIM), F32)],
        scratch_shapes=[pltpu.VMEM((heads_per_step, HEAD_DIM, HEAD_DIM), F32)],
        compiler_params=_params("parallel", "parallel", "arbitrary"),
        name="hgrn2",
    )(qa, fa, ia, ga, lb.reshape(1, width), a_norm_w.reshape(1, HEAD_DIM), amat, s0)


def _neg_abs(x):
    return pltpu.bitcast(pltpu.bitcast(x, jnp.uint32) | jnp.uint32(0x80000000), F32)


def _sb_prompt_body(q_ref, k_ref, v_ref, g_ref, uu_ref, o_ref, run_scr, acc_scr, w_scr, *, tq, tk, rc):
    i = pl.program_id(1)
    nd = tq // tk
    nl = tk // LANES
    q0 = pl.multiple_of(i * tq, tq)
    run_scr[...] = jnp.zeros(run_scr.shape, F32)
    acc_scr[...] = jnp.zeros(acc_scr.shape, F32)

    def apply_pending(slot, k_prev, first_row):
        vt = v_ref[pl.ds(pl.multiple_of(k_prev, tk), tk), :]
        for r0 in range(first_row, tq, rc):
            rows = slice(r0, min(r0 + rc, tq))
            acc_scr[rows, :] = acc_scr[rows, :] + _dot(w_scr[slot, rows, :], vt)

    def score(slot, k0, first_row, masked):
        kt = k_ref[pl.ds(pl.multiple_of(k0, tk), tk), :]
        for r0 in range(first_row, tq, rc):
            r1 = min(r0 + rc, tq)
            rows = slice(r0, r1)
            z = _dot_nt(q_ref[rows, :], kt)
            if masked:
                qpos = lax.broadcasted_iota(jnp.int32, (r1 - r0, LANES), 0) + (q0 + r0)
                kpos = lax.broadcasted_iota(jnp.int32, (r1 - r0, LANES), 1) + k0
            zs, sps, his, los, valids = [], [], [], [], []
            rowsum = None
            for c in range(nl):
                zc = z[:, c * LANES:(c + 1) * LANES]
                sp = jnp.maximum(zc, 0.0) + jnp.log2(1.0 + jnp.exp2(_neg_abs(zc)))
                if masked:
                    valid = (kpos + c * LANES) < qpos
                    sp = jnp.where(valid, sp, 0.0)
                    valids.append(valid)
                hi = sp.astype(BF16)
                lo = (sp - hi.astype(F32)).astype(BF16)
                zs.append(zc); sps.append(sp); his.append(hi); los.append(lo)
                rowsum = sp if rowsum is None else rowsum + sp
            tail = _dot(jnp.concatenate(his + los, axis=1), uu_ref[...])
            run = run_scr[rows, :]
            for c in range(nl):
                w = jnp.exp2(zs[c] - (sps[c] + tail[:, c * LANES:(c + 1) * LANES] + run))
                if masked:
                    w = jnp.where(valids[c], w, 0.0)
                w_scr[slot, rows, c * LANES:(c + 1) * LANES] = w.astype(BF16)
            run_scr[rows, :] = run + jnp.sum(rowsum, axis=-1, keepdims=True)

    assert nd % 2 == 0
    for d in range(nd - 1, 0, -2):
        score(0, q0 + d * tk, d * tk, True)
        if d + 1 < nd:
            apply_pending(1, q0 + (d + 1) * tk, (d + 1) * tk)
        score(1, q0 + (d - 1) * tk, (d - 1) * tk, True)
        apply_pending(0, q0 + d * tk, d * tk)

    def before(jj, kp):
        k_a = q0 - (2 * jj + 1) * tk
        score(0, k_a, 0, False)
        apply_pending(1, kp, 0)
        score(1, k_a - tk, 0, False)
        apply_pending(0, k_a, 0)
        return k_a - tk

    k_pending = lax.fori_loop(0, (i * nd) // 2, before, q0)
    apply_pending(1, k_pending, 0)
    o_ref[...] = (acc_scr[...] * _silu(g_ref[...])).astype(o_ref.dtype)


def _tail_matrix(n):
    sp = np.arange(n)[:, None]
    s = np.arange(n)[None, :]
    return (sp > s).astype(np.float32)


def sb_prompt(q, k, v, gate, tq, tk, rc, out_dtype):
    t, width = q.shape
    h = width // HEAD_DIM
    u = _tail_matrix(tk)
    uu = jnp.asarray(np.concatenate([u, u], axis=0), BF16)
    q_spec = pl.BlockSpec((tq, HEAD_DIM), lambda hh, i: (i, hh))
    kv_spec = pl.BlockSpec((t, HEAD_DIM), lambda hh, i: (0, hh))
    return pl.pallas_call(
        functools.partial(_sb_prompt_body, tq=tq, tk=tk, rc=rc),
        grid=(h, t // tq),
        in_specs=[q_spec, kv_spec, kv_spec, q_spec, pl.BlockSpec(uu.shape, lambda hh, i: (0, 0))],
        out_specs=q_spec,
        out_shape=jax.ShapeDtypeStruct((t, width), out_dtype),
        scratch_shapes=[pltpu.VMEM((tq, LANES), F32), pltpu.VMEM((tq, HEAD_DIM), F32),
                        pltpu.VMEM((2, tq, tk), BF16)],
        compiler_params=_params("parallel", "arbitrary"),
        name="sb_prompt",
    )(q, k, v, gate, uu)


def _logf_cumsum_body(fl_ref, b_ref, tri_ref, lf_ref, f2_ref, carry_scr):
    @pl.when(pl.program_id(0) == 0)
    def _():
        carry_scr[...] = jnp.zeros_like(carry_scr)

    lf = _log_sigmoid(fl_ref[...] + b_ref[...])
    lf_ref[...] = lf
    f = carry_scr[...] + _dot_exact_lhs01(tri_ref[...], lf)
    f2_ref[...] = f * LOG2E
    carry_scr[...] = f[f.shape[0] - 1:, :]


def logf_cumsum(fl, b_forget, blk):
    t, h = fl.shape
    tri = jnp.asarray(np.tril(np.ones((blk, blk), np.float32)), BF16)
    spec = pl.BlockSpec((blk, h), lambda i: (i, 0))
    return pl.pallas_call(
        _logf_cumsum_body,
        grid=(t // blk,),
        in_specs=[spec, pl.BlockSpec((1, h), lambda i: (0, 0)), pl.BlockSpec((blk, blk), lambda i: (0, 0))],
        out_specs=[spec, spec],
        out_shape=[jax.ShapeDtypeStruct((t, h), F32), jax.ShapeDtypeStruct((t, h), F32)],
        scratch_shapes=[pltpu.VMEM((1, h), F32)],
        compiler_params=_params("arbitrary"),
        name="logf_cumsum",
    )(fl, b_forget.reshape(1, h), tri)


def _logf_body(fl_ref, b_ref, lf_ref):
    lf_ref[...] = _log_sigmoid(fl_ref[...] + b_ref[...])


def logf_only(fl, b_forget):
    t, h = fl.shape
    return pl.pallas_call(
        _logf_body,
        grid=(1,),
        in_specs=[pl.BlockSpec((t, h), lambda i: (0, 0)), pl.BlockSpec((1, h), lambda i: (0, 0))],
        out_specs=pl.BlockSpec((t, h), lambda i: (0, 0)),
        out_shape=jax.ShapeDtypeStruct((t, h), F32),
        name="logf",
    )(fl, b_forget.reshape(1, h))


_BIAS_PIECES = 3


def _bias_selectors(h):
    sel = np.zeros((2, h, _BIAS_PIECES, h, LANES), np.float32)
    for hh in range(h):
        for p in range(_BIAS_PIECES):
            sel[0, hh, p, hh, p] = 1.0
            sel[1, hh, p, hh, _BIAS_PIECES + p] = -1.0
    return sel


def _bias_columns(f, sel_ref, query_side):
    out = None
    for p, piece in enumerate(_split3(f)):
        d = _dot(piece, sel_ref[0, p])
        out = d if out is None else out + d
    lane = lax.broadcasted_iota(jnp.int32, out.shape, 1)
    ones_at = (lane >= _BIAS_PIECES) & (lane < 2 * _BIAS_PIECES) if query_side else lane < _BIAS_PIECES
    return jnp.where(ones_at, 1.0, out).astype(BF16)


def _fox_prompt_body(q_ref, k_ref, v_ref, g_ref, f_ref, selq_ref, selk_ref, o_ref,
                     kx_scr, vx_scr, m_scr, acc_scr, p_scr, alpha_scr, *, tq, tk, rc):
    i = pl.program_id(1)
    nd = tq // tk
    nl = tk // LANES
    t_all = k_ref.shape[0]

    @pl.when(i == 0)
    def _():
        kx_scr[:, 0:HEAD_DIM] = k_ref[...]
        kx_scr[:, HEAD_DIM:] = _bias_columns(f_ref[...], selk_ref, False)
        vx_scr[:, 0:HEAD_DIM] = v_ref[...]
        vx_scr[:, HEAD_DIM:] = jnp.ones((t_all, LANES), BF16)

    q0 = pl.multiple_of(i * tq, tq)
    qx = jnp.concatenate([q_ref[...], _bias_columns(f_ref[pl.ds(q0, tq), :], selq_ref, True)], axis=1)
    m_scr[...] = jnp.full(m_scr.shape, -jnp.inf, F32)
    acc_scr[...] = jnp.zeros(acc_scr.shape, F32)

    def apply_pending(slot, k_prev, first_row):
        vxt = vx_scr[pl.ds(pl.multiple_of(k_prev, tk), tk), :]
        for r0 in range(first_row, tq, rc):
            rows = slice(r0, r0 + rc)
            pv = _dot(p_scr[slot, rows, :], vxt)
            alpha = alpha_scr[slot, rows, :]
            acc_scr[rows, 0:HEAD_DIM] = alpha * acc_scr[rows, 0:HEAD_DIM] + pv[:, 0:HEAD_DIM]
            acc_scr[rows, HEAD_DIM:] = alpha * acc_scr[rows, HEAD_DIM:] + pv[:, HEAD_DIM:]

    def score(slot, k0, first_row, masked):
        kxt = kx_scr[pl.ds(pl.multiple_of(k0, tk), tk), :]
        for r0 in range(first_row, tq, rc):
            rows = slice(r0, r0 + rc)
            s = _dot_nt(qx[r0:r0 + rc, :], kxt)
            sb = [s[:, c * LANES:(c + 1) * LANES] for c in range(nl)]
            if masked:
                qpos = lax.broadcasted_iota(jnp.int32, (rc, LANES), 0) + (q0 + r0)
                kpos = lax.broadcasted_iota(jnp.int32, (rc, LANES), 1) + k0
                sb = [jnp.where((kpos + c * LANES) <= qpos, sb[c], -jnp.inf) for c in range(nl)]
            mx = sb[0]
            for c in range(1, nl):
                mx = jnp.maximum(mx, sb[c])
            m_old = m_scr[rows, :]
            m_new = jnp.maximum(m_old, jnp.max(mx, axis=-1, keepdims=True))
            alpha_scr[slot, rows, :] = jnp.exp2(m_old - m_new)
            for c in range(nl):
                p_scr[slot, rows, c * LANES:(c + 1) * LANES] = jnp.exp2(sb[c] - m_new).astype(BF16)
            m_scr[rows, :] = m_new

    assert nd % 2 == 0 and tk % rc == 0
    for d in range(0, nd, 2):
        score(0, q0 + d * tk, d * tk, True)
        if d > 0:
            apply_pending(1, q0 + (d - 1) * tk, (d - 1) * tk)
        score(1, q0 + (d + 1) * tk, (d + 1) * tk, True)
        apply_pending(0, q0 + d * tk, d * tk)

    p_scr[1, 0:(nd - 1) * tk, :] = jnp.zeros(((nd - 1) * tk, tk), BF16)
    alpha_scr[1, 0:(nd - 1) * tk, :] = jnp.ones(((nd - 1) * tk, LANES), F32)

    def before(j, kp):
        score(0, 2 * j * tk, 0, False)
        apply_pending(1, kp, 0)
        score(1, (2 * j + 1) * tk, 0, False)
        apply_pending(0, 2 * j * tk, 0)
        return (2 * j + 1) * tk

    k_pending = lax.fori_loop(0, (i * nd) // 2, before, q0 + (nd - 1) * tk)
    apply_pending(1, k_pending, 0)
    o_ref[...] = (acc_scr[:, 0:HEAD_DIM] / acc_scr[:, HEAD_DIM:] * _silu(g_ref[...])).astype(o_ref.dtype)


def fox_prompt(q, k, v, gate, f2, tq, tk, rc, out_dtype):
    t, width = q.shape
    h = width // HEAD_DIM
    sel = jnp.asarray(_bias_selectors(h), BF16)
    q_spec = pl.BlockSpec((tq, HEAD_DIM), lambda hh, i: (i, hh))
    kv_spec = pl.BlockSpec((t, HEAD_DIM), lambda hh, i: (0, hh))
    sel_spec = pl.BlockSpec((1, _BIAS_PIECES, h, LANES), lambda hh, i: (hh, 0, 0, 0))
    return pl.pallas_call(
        functools.partial(_fox_prompt_body, tq=tq, tk=tk, rc=rc),
        grid=(h, t // tq),
        in_specs=[q_spec, kv_spec, kv_spec, q_spec, pl.BlockSpec((t, h), lambda hh, i: (0, 0)),
                  sel_spec, sel_spec],
        out_specs=q_spec,
        out_shape=jax.ShapeDtypeStruct((t, width), out_dtype),
        scratch_shapes=[pltpu.VMEM((t, 2 * HEAD_DIM), BF16), pltpu.VMEM((t, 2 * HEAD_DIM), BF16),
                        pltpu.VMEM((tq, LANES), F32), pltpu.VMEM((tq, 2 * HEAD_DIM), F32),
                        pltpu.VMEM((2, tq, tk), BF16), pltpu.VMEM((2, tq, LANES), F32)],
        compiler_params=_params("parallel", "arbitrary"),
        name="fox_prompt",
    )(q, k, v, gate, f2, sel[0], sel[1])


HEADS_PER_GROUP = 8


def _expand_queries(q, tq):
    gw = q.shape[1]
    rep = jnp.concatenate([q] * HEADS_PER_GROUP, axis=0)
    r = lax.broadcasted_iota(jnp.int32, (HEADS_PER_GROUP * tq, gw), 0) // tq
    cidx = lax.broadcasted_iota(jnp.int32, (HEADS_PER_GROUP * tq, gw), 1) // HEAD_DIM
    return jnp.where(r == cidx, rep, 0.0)


def _collect_heads(full, tq):
    return jnp.concatenate(
        [full[hh * tq:(hh + 1) * tq, hh * HEAD_DIM:(hh + 1) * HEAD_DIM] for hh in range(HEADS_PER_GROUP)],
        axis=1)


def _gather_keys(cache_ref, new_ref, scr, past, tk):
    g = HEADS_PER_GROUP
    x = cache_ref[0].reshape(past // g, g, g, HEAD_DIM)
    x = jnp.swapaxes(x, 1, 2)
    for hh in range(g):
        scr[0:past, hh * HEAD_DIM:(hh + 1) * HEAD_DIM] = x[:, hh].reshape(past, HEAD_DIM).astype(BF16)
    scr[past:tk, :] = new_ref[...].astype(BF16)


def _row_blocks(n):
    return [(r0, min(r0 + LANES, n)) for r0 in range(0, n, LANES)]


def _cumsum_rows(pieces, tri_ref):
    n = pieces[0].shape[0]
    out, carry = [], None
    for r0, r1 in _row_blocks(n):
        tri = tri_ref[0:r1 - r0, 0:r1 - r0]
        local = None
        for piece in pieces:
            d = _dot(tri, piece[r0:r1])
            local = d if local is None else local + d
        if carry is not None:
            local = local + carry
        carry = local[r1 - r0 - 1:r1 - r0]
        out.append(local)
    return jnp.concatenate(out, axis=0)


def _tailsum_rows(pieces, tri_ref):
    n = pieces[0].shape[0]
    out, carry = [], None
    for r0, r1 in reversed(_row_blocks(n)):
        tri = tri_ref[0:r1 - r0, 0:r1 - r0]
        local, total = None, None
        for piece in pieces:
            blk = piece[r0:r1]
            d = _dot_tn(tri, blk)
            local = d if local is None else local + d
            t = blk.astype(F32)
            total = t if total is None else total + t
        local = local - total
        if carry is not None:
            local = local + carry
        carry = local[0:1] + total[0:1]
        out.append(local)
    return jnp.concatenate(out[::-1], axis=0)


def _sb_decode_body(q_ref, kn_ref, vn_ref, kc_ref, vc_ref, g_ref, tri_ref, o_ref, k_scr, v_scr, *, past, tq):
    tk = past + tq
    _gather_keys(kc_ref, kn_ref, k_scr, past, tk)
    _gather_keys(vc_ref, vn_ref, v_scr, past, tk)

    lanes = HEADS_PER_GROUP * tq
    qx = _expand_queries(q_ref[...] * (HEAD_DIM ** -0.5), tq).astype(BF16)
    z = _dot_nt(k_scr[...], qx)
    kpos = lax.broadcasted_iota(jnp.int32, (tk, lanes), 0)
    qpos = past + lax.broadcasted_iota(jnp.int32, (tk, lanes), 1) % tq
    valid = kpos < qpos
    ls_neg = -(jnp.maximum(z, 0.0) + jnp.log(1.0 + jnp.exp(-jnp.abs(z))))
    lm = jnp.where(valid, ls_neg, 0.0)
    hi = lm.astype(BF16)
    lo = (lm - hi.astype(F32)).astype(BF16)
    tail = _tailsum_rows((hi, lo), tri_ref)
    w = jnp.where(valid, jnp.exp(z + ls_neg + tail), 0.0)
    full = _dot_tn(w.astype(BF16), v_scr[...])
    o_ref[...] = (_collect_heads(full, tq) * _silu(g_ref[...])).astype(o_ref.dtype)


def _cache_spec(layer, past):
    return pl.BlockSpec((None, 1, past, HEADS_PER_GROUP, HEAD_DIM), lambda b, g: (layer, b, 0, g, 0))


def sb_decode(q, k_new, v_new, k_cache, v_cache, layer, gate, batch, tq, out_dtype):
    n, width = q.shape
    past = k_cache.shape[2]
    tk = past + tq
    gw = HEADS_PER_GROUP * HEAD_DIM
    groups = width // gw
    tri = jnp.asarray(np.tril(np.ones((LANES, LANES), np.float32)), BF16)
    row_spec = pl.BlockSpec((tq, gw), lambda b, g: (b, g))
    cache_spec = _cache_spec(layer, past)
    return pl.pallas_call(
        functools.partial(_sb_decode_body, past=past, tq=tq),
        grid=(batch, groups),
        in_specs=[row_spec, row_spec, row_spec, cache_spec, cache_spec, row_spec,
                  pl.BlockSpec((LANES, LANES), lambda b, g: (0, 0))],
        out_specs=row_spec,
        out_shape=jax.ShapeDtypeStruct((n, width), out_dtype),
        scratch_shapes=[pltpu.VMEM((tk, gw), BF16), pltpu.VMEM((tk, gw), BF16)],
        compiler_params=_params("parallel", "parallel"),
        name="sb_decode",
    )(q, k_new, v_new, k_cache, v_cache, gate, tri)


def _fox_decode_body(q_ref, kn_ref, vn_ref, kc_ref, vc_ref, g_ref, lfn_ref, lfc_ref, ex_ref, tri_ref, o_ref,
                     k_scr, v_scr, lf_scr, *, past, tq):
    tk = past + tq
    _gather_keys(kc_ref, kn_ref, k_scr, past, tk)
    _gather_keys(vc_ref, vn_ref, v_scr, past, tk)
    lf_scr[0:past, :] = lfc_ref[0]
    lf_scr[past:tk, :] = lfn_ref[...]

    lanes = HEADS_PER_GROUP * tq
    hi, mid, lo = _split3(lf_scr[...])
    ex = ex_ref[0]
    lfx = _dot(hi, ex) + _dot(mid, ex) + _dot(lo, ex)
    f_key = _cumsum_rows(_split3(lfx), tri_ref)
    kpos = lax.broadcasted_iota(jnp.int32, (tk, lanes), 0)
    qpos = past + lax.broadcasted_iota(jnp.int32, (tk, lanes), 1) % tq
    f_query = jnp.sum(jnp.where(kpos == qpos, f_key, 0.0), axis=0, keepdims=True)

    qx = _expand_queries(q_ref[...] * (HEAD_DIM ** -0.5), tq).astype(BF16)
    s = _dot_nt(k_scr[...], qx) + (f_query - f_key)
    s = jnp.where(kpos <= qpos, s, -jnp.inf)
    p = jnp.exp(s - jnp.max(s, axis=0, keepdims=True))
    p = p / jnp.sum(p, axis=0, keepdims=True)
    full = _dot_tn(p.astype(BF16), v_scr[...])
    o_ref[...] = (_collect_heads(full, tq) * _silu(g_ref[...])).astype(o_ref.dtype)


def fox_decode(q, k_new, v_new, k_cache, v_cache, layer, gate, lf_new, lf_cache, batch, tq, out_dtype):
    n, width = q.shape
    h = width // HEAD_DIM
    past = k_cache.shape[2]
    tk = past + tq
    gw = HEADS_PER_GROUP * HEAD_DIM
    groups = width // gw
    lanes = HEADS_PER_GROUP * tq
    tri = jnp.asarray(np.tril(np.ones((LANES, LANES), np.float32)), BF16)
    head_of_lane = np.arange(lanes)[None, None, :] // tq + HEADS_PER_GROUP * np.arange(groups)[:, None, None]
    expand = jnp.asarray((np.arange(h)[None, :, None] == head_of_lane).astype(np.float32), BF16)
    row_spec = pl.BlockSpec((tq, gw), lambda b, g: (b, g))
    cache_spec = _cache_spec(layer, past)
    return pl.pallas_call(
        functools.partial(_fox_decode_body, past=past, tq=tq),
        grid=(batch, groups),
        in_specs=[row_spec, row_spec, row_spec, cache_spec, cache_spec, row_spec,
                  pl.BlockSpec((tq, h), lambda b, g: (b, 0)),
                  pl.BlockSpec((None, 1, past, h), lambda b, g: (layer, b, 0, 0)),
                  pl.BlockSpec((1, h, lanes), lambda b, g: (g, 0, 0)),
                  pl.BlockSpec((LANES, LANES), lambda b, g: (0, 0))],
        out_specs=row_spec,
        out_shape=jax.ShapeDtypeStruct((n, width), out_dtype),
        scratch_shapes=[pltpu.VMEM((tk, gw), BF16), pltpu.VMEM((tk, gw), BF16), pltpu.VMEM((tk, h), F32)],
        compiler_params=_params("parallel", "parallel"),
        name="fox_decode",
    )(q, k_new, v_new, k_cache, v_cache, gate, lf_new, lf_cache, expand, tri)


QK_SCALE_LOG2 = LOG2E * HEAD_DIM ** -0.5
FLAT32 = (False, F32, 1.0)
FLAT16 = (False, BF16, 1.0)
HEADS32 = (True, F32, 1.0)
QUERY16 = (False, BF16, QK_SCALE_LOG2)


def _tile(m, pref):
    return pref if m % pref == 0 else m


def _even_layer(x, norm_w, w_in, w_out, lb, a_norm_w, s0, caches, batch, seq, prompt):
    n, d = x.shape
    half = d // 2
    tm = _tile(n, 1024)
    hn = rmsnorm(x, norm_w, BF16, _tile(n, 256))

    def p(group, *outs):
        return proj([hn], w_in, group * half, half, tm, 1024, outs)

    (qa,), (fa,), (ia,), (ga,), (gb,) = p(0, FLAT32), p(1, FLAT32), p(2, FLAT32), p(3, FLAT32), p(7, FLAT32)
    if prompt:
        (qb,) = p(4, QUERY16)
        kb_leaf, kb = p(5, HEADS32, FLAT16)
        vb_leaf, vb = p(6, HEADS32, FLAT16)
        oa, s_new = hgrn2(qa, fa, ia, ga, lb, a_norm_w, s0, batch, seq, 1024, HGRN_CHUNK, 1, BF16)
        ob = sb_prompt(qb, kb, vb, gb, 1024, 256, 1024, BF16)
    else:
        (qb,) = p(4, FLAT32)
        kb_leaf, kb = p(5, HEADS32, FLAT32)
        vb_leaf, vb = p(6, HEADS32, FLAT32)
        oa, s_new = hgrn2(qa, fa, ia, ga, lb, a_norm_w, s0, batch, seq, seq, seq, half // HEAD_DIM, BF16)
        ob = sb_decode(qb, kb, vb, caches[0], caches[1], caches[2], gb, batch, seq, BF16)
    (y,) = proj([oa, ob], w_out, 0, d, tm, 1024, (FLAT32,), residual=x)
    return y, s_new, kb_leaf, vb_leaf


def _odd_layer(x, norm_w, w_in, w_fl, b_forget, w_out, caches, batch, seq, prompt):
    n, d = x.shape
    heads = d // HEAD_DIM
    tm = _tile(n, 1024)
    hn = rmsnorm(x, norm_w, BF16, _tile(n, 256))

    def p(group, *outs):
        return proj([hn], w_in, group * d, d, tm, 1024, outs)

    (gate,) = p(3, FLAT32)
    (fl,) = proj([hn], w_fl, 0, heads, tm, heads, (FLAT32,))
    if prompt:
        (q,) = p(0, QUERY16)
        k_leaf, k = p(1, HEADS32, FLAT16)
        v_leaf, v = p(2, HEADS32, FLAT16)
        logf, f2 = logf_cumsum(fl, b_forget, 512)
        o = fox_prompt(q, k, v, gate, f2, 1024, 512, 256, BF16)
    else:
        (q,) = p(0, FLAT32)
        k_leaf, k = p(1, HEADS32, FLAT32)
        v_leaf, v = p(2, HEADS32, FLAT32)
        logf = logf_only(fl, b_forget)
        o = fox_decode(q, k, v, caches[0], caches[1], caches[3], gate, logf, caches[2], batch, seq, BF16)
    (y,) = proj([o], w_out, 0, d, tm, 1024, (FLAT32,), residual=x)
    return y, k_leaf, v_leaf, logf


def kernel(x_prompt, x_sample, state_a_hgrn, cache_b_k, cache_b_v, cache_c_k, cache_c_v, cache_c_logf,
           norm_w, final_norm_w, w_in_even, w_out_even, lb_logits, a_norm_w, w_in_odd, b_forget, w_out_odd):
    bp, tp, d = x_prompt.shape
    bs, ts, _ = x_sample.shape
    depth = norm_w.shape[0]
    n_even = w_in_even.shape[0]
    lb_all = jnp.cumsum(jax.nn.softmax(lb_logits.astype(F32), axis=0), axis=0)[:n_even]

    hp = x_prompt.reshape(bp * tp, d)
    hs = x_sample.reshape(bs * ts, d)
    outs = {name: [] for name in ("sa_p", "sa_s", "bk_p", "bv_p", "bk_s", "bv_s",
                                  "ck_p", "cv_p", "cf_p", "ck_s", "cv_s", "cf_s")}
    for layer in range(depth):
        j = layer // 2
        if layer % 2 == 0:
            a_heads = state_a_hgrn.shape[2]
            b_heads = cache_b_k.shape[3]
            zeros = jnp.zeros((bp, a_heads) + state_a_hgrn.shape[3:], F32)
            w_in = round_weights(w_in_even, j, 128)
            w_out = round_weights(w_out_even, j, 512)
            hp, sp, kp, vp = _even_layer(hp, norm_w[layer], w_in, w_out, lb_all[j], a_norm_w[j],
                                         zeros, None, bp, tp, True)
            hs, ss, ksm, vsm = _even_layer(hs, norm_w[layer], w_in, w_out, lb_all[j], a_norm_w[j],
                                           state_a_hgrn[j], (cache_b_k, cache_b_v, j), bs, ts, False)
            outs["sa_p"].append(sp); outs["sa_s"].append(ss)
            outs["bk_p"].append(kp.reshape(bp, tp, b_heads, HEAD_DIM))
            outs["bv_p"].append(vp.reshape(bp, tp, b_heads, HEAD_DIM))
            outs["bk_s"].append(ksm.reshape(bs, ts, b_heads, HEAD_DIM))
            outs["bv_s"].append(vsm.reshape(bs, ts, b_heads, HEAD_DIM))
        else:
            c_heads = cache_c_k.shape[3]
            w_in = round_weights(w_in_odd, j, 128)
            w_fl = w_in[:, 4 * c_heads * HEAD_DIM:]
            w_out = round_weights(w_out_odd, j, 512)
            hp, kp, vp, fp = _odd_layer(hp, norm_w[layer], w_in, w_fl, b_forget[j], w_out, None, bp, tp, True)
            caches = (cache_c_k, cache_c_v, cache_c_logf, j)
            hs, ksm, vsm, fsm = _odd_layer(hs, norm_w[layer], w_in, w_fl, b_forget[j], w_out, caches, bs, ts, False)
            outs["ck_p"].append(kp.reshape(bp, tp, c_heads, HEAD_DIM))
            outs["cv_p"].append(vp.reshape(bp, tp, c_heads, HEAD_DIM))
            outs["cf_p"].append(fp.reshape(bp, tp, c_heads))
            outs["ck_s"].append(ksm.reshape(bs, ts, c_heads, HEAD_DIM))
            outs["cv_s"].append(vsm.reshape(bs, ts, c_heads, HEAD_DIM))
            outs["cf_s"].append(fsm.reshape(bs, ts, c_heads))
    y_prompt = rmsnorm(hp, final_norm_w, F32, _tile(bp * tp, 256)).reshape(bp, tp, d)
    y_sample = rmsnorm(hs, final_norm_w, F32, _tile(bs * ts, 256)).reshape(bs, ts, d)
    st = {k: jnp.stack(v) for k, v in outs.items()}
    return (y_prompt, y_sample, st["sa_p"], st["sa_s"], st["bk_p"], st["bv_p"], st["bk_s"], st["bv_s"],
            st["ck_p"], st["cv_p"], st["cf_p"], st["ck_s"], st["cv_s"], st["cf_s"])
```

```python
import functools

import numpy as np
import jax
import jax.numpy as jnp
from jax import lax
from jax.experimental import pallas as pl
from jax.experimental.pallas import tpu as pltpu

F32 = jnp.float32
BF16 = jnp.bfloat16

EPS = 1e-6
HEAD_DIM = 128
LANES = 128
LOG2E = 1.4426950408889634
HGRN_CHUNK = 64
VMEM_LIMIT_BYTES = 56 * 1024 * 1024

_NT = (((1,), (1,)), ((), ()))
_TN = (((0,), (0,)), ((), ()))


def _params(*sem):
    return pltpu.CompilerParams(dimension_semantics=sem, vmem_limit_bytes=VMEM_LIMIT_BYTES)


def _dot(a, b):
    return jnp.dot(a, b, preferred_element_type=F32)


def _dot_nt(a, b):
    return lax.dot_general(a, b, _NT, preferred_element_type=F32)


def _dot_tn(a, b):
    return lax.dot_general(a, b, _TN, preferred_element_type=F32)


def _split3(x):
    hi = x.astype(BF16)
    r1 = x - hi.astype(F32)
    mid = r1.astype(BF16)
    lo = (r1 - mid.astype(F32)).astype(BF16)
    return hi, mid, lo


def _dot_exact_lhs01(a01, x):
    hi, mid, lo = _split3(x)
    return _dot(a01, hi) + _dot(a01, mid) + _dot(a01, lo)


def _dot_exact_lhs01x3(a01x3, x):
    return _dot(a01x3, jnp.concatenate(_split3(x), axis=0))


def _sigmoid_pair(z):
    e = jnp.exp(-jnp.abs(z))
    r = 1.0 / (1.0 + e)
    er = e * r
    pos = z >= 0
    return jnp.where(pos, r, er), jnp.where(pos, er, r)


def _silu(x):
    return x * _sigmoid_pair(x)[0]


def _log_sigmoid(x):
    return jnp.minimum(x, 0.0) - jnp.log(1.0 + jnp.exp(-jnp.abs(x)))


def _rmsnorm_body(x_ref, w_ref, o_ref):
    x = x_ref[...]
    ms = jnp.mean(x * x, axis=-1, keepdims=True)
    o_ref[...] = (x * lax.rsqrt(ms + EPS) * w_ref[...]).astype(o_ref.dtype)


def rmsnorm(x, w, out_dtype, tm):
    m, d = x.shape
    return pl.pallas_call(
        _rmsnorm_body,
        grid=(m // tm,),
        in_specs=[pl.BlockSpec((tm, d), lambda i: (i, 0)), pl.BlockSpec((1, d), lambda i: (0, 0))],
        out_specs=pl.BlockSpec((tm, d), lambda i: (i, 0)),
        out_shape=jax.ShapeDtypeStruct((m, d), out_dtype),
        compiler_params=_params("parallel"),
        name="rmsnorm",
    )(x, w.reshape(1, d))


def _round_body(w_ref, o_ref):
    o_ref[...] = w_ref[...].astype(o_ref.dtype)


def round_weights(w, layer, rows):
    _, k, n = w.shape
    return pl.pallas_call(
        _round_body,
        grid=(k // rows,),
        in_specs=[pl.BlockSpec((None, rows, n), lambda i: (layer, i, 0))],
        out_specs=pl.BlockSpec((rows, n), lambda i: (i, 0)),
        out_shape=jax.ShapeDtypeStruct((k, n), BF16),
        compiler_params=_params("parallel"),
        name="round_weights",
    )(w)


def _proj_body(*refs, n_in, has_residual, outs):
    a_refs = refs[:n_in]
    w_refs = refs[n_in:2 * n_in]
    pos = 2 * n_in
    r_ref = refs[pos] if has_residual else None
    o_refs = refs[pos + int(has_residual):]
    acc = None
    for a_ref, w_ref in zip(a_refs, w_refs):
        d = _dot(a_ref[...], w_ref[...])
        acc = d if acc is None else acc + d
    if has_residual:
        acc = r_ref[...] + acc
    for o_ref, (_, scale) in zip(o_refs, outs):
        val = acc if scale == 1.0 else acc * scale
        o_ref[...] = val.astype(o_ref.dtype)


def proj(a_list, w, col0, ncols, tm, tn, outs, residual=None):
    m = a_list[0].shape[0]
    kk = a_list[0].shape[1]
    assert all(a.shape == (m, kk) for a in a_list) and w.shape[0] == kk * len(a_list)
    assert m % tm == 0 and ncols % tn == 0 and col0 % tn == 0
    cb = col0 // tn
    in_specs = [pl.BlockSpec((tm, kk), lambda i, j: (i, 0)) for _ in a_list]
    in_specs += [pl.BlockSpec((kk, tn), functools.partial(lambda i, j, r: (r, cb + j), r=r)) for r in range(len(a_list))]
    args = list(a_list) + [w] * len(a_list)
    if residual is not None:
        in_specs.append(pl.BlockSpec((tm, tn), lambda i, j: (i, j)))
        args.append(residual)
    out_specs = [pl.BlockSpec((tm, tn), lambda i, j: (i, j)) for _ in outs]
    out_shape = [jax.ShapeDtypeStruct((m, ncols), dtype) for dtype, _ in outs]
    return pl.pallas_call(
        functools.partial(_proj_body, n_in=len(a_list), has_residual=residual is not None, outs=tuple(outs)),
        grid=(m // tm, ncols // tn),
        in_specs=in_specs,
        out_specs=out_specs,
        out_shape=out_shape,
        compiler_params=_params("parallel", "parallel"),
        name="proj",
    )(*args)


def _hgrn_maps(c):
    levels = int(np.log2(c))
    assert 2 ** levels == c
    t = np.arange(c)[:, None]
    s = np.arange(c)[None, :]
    mats = [(s <= t), (s > t)]
    for l in range(levels):
        b = 2 ** l
        start = (t // (2 * b)) * (2 * b)
        upper = (t // b) % 2 == 1
        mats.append((upper & (s >= start + b) & (s <= t)) | ((~upper) & (s > t) & (s <= start + b - 1)))
    return np.concatenate(mats, axis=0).astype(np.float32), levels


def _hgrn_body(q_ref, z_ref, v_ref, g_ref, lb_ref, nw_ref, a_ref, s0_ref, o_ref, sout_ref, st_scr,
               *, c, n_chunks, levels, heads):
    tb = pl.program_id(2)

    @pl.when(tb == 0)
    def _():
        for hh in range(heads):
            st_scr[hh] = s0_ref[0, hh].T

    nw = nw_ref[...]
    amat = a_ref[...]
    row = lax.broadcasted_iota(jnp.int32, (c, c), 0)
    col = lax.broadcasted_iota(jnp.int32, (c, c), 1)
    xor = row ^ col

    units = [(slice(ci * c, (ci + 1) * c), slice(hh * HEAD_DIM, (hh + 1) * HEAD_DIM))
             for hh in range(heads) for ci in range(n_chunks)]
    nu = len(units)
    lbs = [lb_ref[:, cols] for _, cols in units]
    qs = [_silu(q_ref[rows, cols]) for rows, cols in units]
    sigs = [_sigmoid_pair(z_ref[rows, cols]) for rows, cols in units]
    gs = [jnp.log(lbs[u] + (1.0 - lbs[u]) * sigs[u][0]) for u in range(nu)]
    ks = [(1.0 - lbs[u]) * sigs[u][1] for u in range(nu)]
    vs = [v_ref[rows, cols] for rows, cols in units]
    vbs = [v.astype(BF16) for v in vs]
    es = [jnp.exp(_dot_exact_lhs01x3(amat, g)) for g in gs]
    atts = [None] * nu
    for l in range(levels - 1, -1, -1):
        for u in range(nu):
            el = es[u][(2 + l) * c:(3 + l) * c]
            al = _dot_nt((qs[u] * el).astype(BF16), (ks[u] * el).astype(BF16))
            atts[u] = al if atts[u] is None else jnp.where(xor < 2 ** (l + 1), al, atts[u])
    atts = [jnp.where(row > col, att, 0.0).astype(BF16) for att in atts]
    o_intras = [_dot(atts[u], vbs[u]) + jnp.sum(qs[u] * ks[u], axis=-1, keepdims=True) * vs[u] for u in range(nu)]
    upds = [_dot_tn(vbs[u], (ks[u] * es[u][c:2 * c]).astype(BF16)) for u in range(nu)]
    qgs = [(qs[u] * es[u][0:c]).astype(BF16) for u in range(nu)]

    for hh in range(heads):
        st = st_scr[hh]
        for ci in range(n_chunks):
            u = hh * n_chunks + ci
            rows, cols = units[u]
            o = _dot_nt(qgs[u], st.astype(BF16)) + o_intras[u]
            st = st * es[u][c - 1:c] + upds[u]
            ms = jnp.mean(o * o, axis=-1, keepdims=True)
            y = o * lax.rsqrt(ms + EPS) * nw
            o_ref[rows, cols] = (y * _silu(g_ref[rows, cols])).astype(o_ref.dtype)
        st_scr[hh] = st

    @pl.when(tb == pl.num_programs(2) - 1)
    def _():
        for hh in range(heads):
            sout_ref[0, hh] = st_scr[hh].T


def hgrn2(qa, fa, ia, ga, lb, a_norm_w, s0, batch, seq, rows_per_step, c, heads_per_step, out_dtype):
    n, width = qa.shape
    h = width // HEAD_DIM
    assert n == batch * seq and seq % rows_per_step == 0 and rows_per_step % c == 0 and h % heads_per_step == 0
    nb = seq // rows_per_step
    gw = heads_per_step * HEAD_DIM
    amat_np, levels = _hgrn_maps(c)
    amat = jnp.asarray(np.concatenate([amat_np] * 3, axis=1), BF16)
    row_spec = pl.BlockSpec((rows_per_step, gw), lambda b, hh, t: (b * nb + t, hh))
    state_spec = pl.BlockSpec((1, heads_per_step, HEAD_DIM, HEAD_DIM), lambda b, hh, t: (b, hh, 0, 0))
    body = functools.partial(_hgrn_body, c=c, n_chunks=rows_per_step // c, levels=levels, heads=heads_per_step)
    return pl.pallas_call(
        body,
        grid=(batch, h // heads_per_step, nb),
        in_specs=[row_spec, row_spec, row_spec, row_spec,
                  pl.BlockSpec((1, gw), lambda b, hh, t: (0, hh)),
                  pl.BlockSpec((1, HEAD_DIM), lambda b, hh, t: (0, 0)),
                  pl.BlockSpec(amat.shape, lambda b, hh, t: (0, 0)),
                  state_spec],
        out_specs=[row_spec, state_spec],
        out_shape=[jax.ShapeDtypeStruct((n, width), out_dtype),
                   jax.ShapeDtypeStruct((batch, h, HEAD_DIM, HEAD_DIM), F32)],
        scratch_shapes=[pltpu.VMEM((heads_per_step, HEAD_DIM, HEAD_DIM), F32)],
        compiler_params=_params("parallel", "parallel", "arbitrary"),
        name="hgrn2",
    )(qa, fa, ia, ga, lb.reshape(1, width), a_norm_w.reshape(1, HEAD_DIM), amat, s0)


def _neg_abs(x):
    return -jnp.abs(x)


def _sb_prompt_body(q_ref, k_ref, v_ref, g_ref, uu_ref, o_ref, run_scr, acc_scr, w_scr, *, tq, tk, rc):
    i = pl.program_id(1)
    nd = tq // tk
    nl = tk // LANES
    q0 = pl.multiple_of(i * tq, tq)
    run_scr[...] = jnp.zeros(run_scr.shape, F32)
    acc_scr[...] = jnp.zeros(acc_scr.shape, F32)

    def apply_pending(slot, k_prev, first_row):
        vt = v_ref[pl.ds(pl.multiple_of(k_prev, tk), tk), :]
        for r0 in range(first_row, tq, rc):
            rows = slice(r0, min(r0 + rc, tq))
            acc_scr[rows, :] = acc_scr[rows, :] + _dot(w_scr[slot, rows, :], vt)

    def score(slot, k0, first_row, masked):
        kt = k_ref[pl.ds(pl.multiple_of(k0, tk), tk), :]
        for r0 in range(first_row, tq, rc):
            r1 = min(r0 + rc, tq)
            rows = slice(r0, r1)
            z = _dot_nt(q_ref[rows, :], kt)
            if masked:
                qpos = lax.broadcasted_iota(jnp.int32, (r1 - r0, LANES), 0) + (q0 + r0)
                kpos = lax.broadcasted_iota(jnp.int32, (r1 - r0, LANES), 1) + k0
            zs, sps, his, los, valids = [], [], [], [], []
            rowsum = None
            for c in range(nl):
                zc = z[:, c * LANES:(c + 1) * LANES]
                sp = jnp.maximum(zc, 0.0) + jnp.log2(1.0 + jnp.exp2(_neg_abs(zc)))
                if masked:
                    valid = (kpos + c * LANES) < qpos
                    sp = jnp.where(valid, sp, 0.0)
                    valids.append(valid)
                hi = sp.astype(BF16)
                lo = (sp - hi.astype(F32)).astype(BF16)
                zs.append(zc); sps.append(sp); his.append(hi); los.append(lo)
                rowsum = sp if rowsum is None else rowsum + sp
            tail = _dot(jnp.concatenate(his + los, axis=1), uu_ref[...])
            run = run_scr[rows, :]
            for c in range(nl):
                w = jnp.exp2(zs[c] - (sps[c] + tail[:, c * LANES:(c + 1) * LANES] + run))
                if masked:
                    w = jnp.where(valids[c], w, 0.0)
                w_scr[slot, rows, c * LANES:(c + 1) * LANES] = w.astype(BF16)
            run_scr[rows, :] = run + jnp.sum(rowsum, axis=-1, keepdims=True)

    assert nd % 2 == 0
    for d in range(nd - 1, 0, -2):
        score(0, q0 + d * tk, d * tk, True)
        if d + 1 < nd:
            apply_pending(1, q0 + (d + 1) * tk, (d + 1) * tk)
        score(1, q0 + (d - 1) * tk, (d - 1) * tk, True)
        apply_pending(0, q0 + d * tk, d * tk)

    def before(jj, kp):
        for u in range(unroll):
            k_a = q0 - (2 * (unroll * jj + u) + 1) * tk
            score(0, k_a, 0, False)
            apply_pending(1, kp, 0)
            score(1, k_a - tk, 0, False)
            apply_pending(0, k_a, 0)
            kp = k_a - tk
        return kp

    unroll = 2
    assert nd % (2 * unroll) == 0
    k_pending = lax.fori_loop(0, (i * nd) // (2 * unroll), before, q0)
    apply_pending(1, k_pending, 0)
    o_ref[...] = (acc_scr[...] * _silu(g_ref[...])).astype(o_ref.dtype)


def _tail_matrix(n):
    sp = np.arange(n)[:, None]
    s = np.arange(n)[None, :]
    return (sp > s).astype(np.float32)


def sb_prompt(q, k, v, gate, tq, tk, rc, out_dtype):
    t, width = q.shape
    h = width // HEAD_DIM
    u = _tail_matrix(tk)
    uu = jnp.asarray(np.concatenate([u, u], axis=0), BF16)
    q_spec = pl.BlockSpec((tq, HEAD_DIM), lambda hh, i: (i, hh))
    kv_spec = pl.BlockSpec((t, HEAD_DIM), lambda hh, i: (0, hh))
    return pl.pallas_call(
        functools.partial(_sb_prompt_body, tq=tq, tk=tk, rc=rc),
        grid=(h, t // tq),
        in_specs=[q_spec, kv_spec, kv_spec, q_spec, pl.BlockSpec(uu.shape, lambda hh, i: (0, 0))],
        out_specs=q_spec,
        out_shape=jax.ShapeDtypeStruct((t, width), out_dtype),
        scratch_shapes=[pltpu.VMEM((tq, LANES), F32), pltpu.VMEM((tq, HEAD_DIM), F32),
                        pltpu.VMEM((2, tq, tk), BF16)],
        compiler_params=_params("parallel", "arbitrary"),
        name="sb_prompt",
    )(q, k, v, gate, uu)


def _logf_cumsum_body(fl_ref, b_ref, tri_ref, lf_ref, f2_ref, carry_scr):
    @pl.when(pl.program_id(0) == 0)
    def _():
        carry_scr[...] = jnp.zeros_like(carry_scr)

    lf = _log_sigmoid(fl_ref[...] + b_ref[...])
    lf_ref[...] = lf
    f = carry_scr[...] + _dot_exact_lhs01(tri_ref[...], lf)
    f2_ref[...] = f * LOG2E
    carry_scr[...] = f[f.shape[0] - 1:, :]


def logf_cumsum(fl, b_forget, blk):
    t, h = fl.shape
    tri = jnp.asarray(np.tril(np.ones((blk, blk), np.float32)), BF16)
    spec = pl.BlockSpec((blk, h), lambda i: (i, 0))
    return pl.pallas_call(
        _logf_cumsum_body,
        grid=(t // blk,),
        in_specs=[spec, pl.BlockSpec((1, h), lambda i: (0, 0)), pl.BlockSpec((blk, blk), lambda i: (0, 0))],
        out_specs=[spec, spec],
        out_shape=[jax.ShapeDtypeStruct((t, h), F32), jax.ShapeDtypeStruct((t, h), F32)],
        scratch_shapes=[pltpu.VMEM((1, h), F32)],
        compiler_params=_params("arbitrary"),
        name="logf_cumsum",
    )(fl, b_forget.reshape(1, h), tri)


def _logf_body(fl_ref, b_ref, lf_ref):
    lf_ref[...] = _log_sigmoid(fl_ref[...] + b_ref[...])


def logf_only(fl, b_forget):
    t, h = fl.shape
    return pl.pallas_call(
        _logf_body,
        grid=(1,),
        in_specs=[pl.BlockSpec((t, h), lambda i: (0, 0)), pl.BlockSpec((1, h), lambda i: (0, 0))],
        out_specs=pl.BlockSpec((t, h), lambda i: (0, 0)),
        out_shape=jax.ShapeDtypeStruct((t, h), F32),
        name="logf",
    )(fl, b_forget.reshape(1, h))


_BIAS_PIECES = 3


def _bias_selectors(h):
    sel = np.zeros((2, h, _BIAS_PIECES, h, LANES), np.float32)
    for hh in range(h):
        for p in range(_BIAS_PIECES):
            sel[0, hh, p, hh, p] = 1.0
            sel[1, hh, p, hh, _BIAS_PIECES + p] = -1.0
    return sel


def _bias_columns(f, sel_ref, query_side):
    out = None
    for p, piece in enumerate(_split3(f)):
        d = _dot(piece, sel_ref[0, p])
        out = d if out is None else out + d
    lane = lax.broadcasted_iota(jnp.int32, out.shape, 1)
    ones_at = (lane >= _BIAS_PIECES) & (lane < 2 * _BIAS_PIECES) if query_side else lane < _BIAS_PIECES
    return jnp.where(ones_at, 1.0, out).astype(BF16)


def _fox_prompt_body(q_ref, k_ref, v_ref, g_ref, f_ref, selq_ref, selk_ref, o_ref,
                     kx_scr, vx_scr, m_scr, acc_scr, p_scr, alpha_scr, *, tq, tk, rc):
    i = pl.program_id(1)
    nd = tq // tk
    nl = tk // LANES
    t_all = k_ref.shape[0]

    @pl.when(i == 0)
    def _():
        kx_scr[:, 0:HEAD_DIM] = k_ref[...]
        kx_scr[:, HEAD_DIM:] = _bias_columns(f_ref[...], selk_ref, False)
        vx_scr[:, 0:HEAD_DIM] = v_ref[...]
        vx_scr[:, HEAD_DIM:] = jnp.ones((t_all, LANES), BF16)

    q0 = pl.multiple_of(i * tq, tq)
    qx = jnp.concatenate([q_ref[...], _bias_columns(f_ref[pl.ds(q0, tq), :], selq_ref, True)], axis=1)
    m_scr[...] = jnp.full(m_scr.shape, -jnp.inf, F32)
    acc_scr[...] = jnp.zeros(acc_scr.shape, F32)

    def apply_pending(slot, k_prev, first_row):
        vxt = vx_scr[pl.ds(pl.multiple_of(k_prev, tk), tk), :]
        for r0 in range(first_row, tq, rc):
            rows = slice(r0, r0 + rc)
            pv = _dot(p_scr[slot, rows, :], vxt)
            alpha = alpha_scr[slot, rows, :]
            acc_scr[rows, 0:HEAD_DIM] = alpha * acc_scr[rows, 0:HEAD_DIM] + pv[:, 0:HEAD_DIM]
            acc_scr[rows, HEAD_DIM:] = alpha * acc_scr[rows, HEAD_DIM:] + pv[:, HEAD_DIM:]

    def score(slot, k0, first_row, masked):
        kxt = kx_scr[pl.ds(pl.multiple_of(k0, tk), tk), :]
        for r0 in range(first_row, tq, rc):
            rows = slice(r0, r0 + rc)
            s = _dot_nt(qx[r0:r0 + rc, :], kxt)
            sb = [s[:, c * LANES:(c + 1) * LANES] for c in range(nl)]
            if masked:
                qpos = lax.broadcasted_iota(jnp.int32, (rc, LANES), 0) + (q0 + r0)
                kpos = lax.broadcasted_iota(jnp.int32, (rc, LANES), 1) + k0
                sb = [jnp.where((kpos + c * LANES) <= qpos, sb[c], -jnp.inf) for c in range(nl)]
            mx = sb[0]
            for c in range(1, nl):
                mx = jnp.maximum(mx, sb[c])
            m_old = m_scr[rows, :]
            m_new = jnp.maximum(m_old, jnp.max(mx, axis=-1, keepdims=True))
            alpha_scr[slot, rows, :] = jnp.exp2(m_old - m_new)
            for c in range(nl):
                p_scr[slot, rows, c * LANES:(c + 1) * LANES] = jnp.exp2(sb[c] - m_new).astype(BF16)
            m_scr[rows, :] = m_new

    assert nd % 2 == 0 and tk % rc == 0
    for d in range(0, nd, 2):
        score(0, q0 + d * tk, d * tk, True)
        if d > 0:
            apply_pending(1, q0 + (d - 1) * tk, (d - 1) * tk)
        score(1, q0 + (d + 1) * tk, (d + 1) * tk, True)
        apply_pending(0, q0 + d * tk, d * tk)

    p_scr[1, 0:(nd - 1) * tk, :] = jnp.zeros(((nd - 1) * tk, tk), BF16)
    alpha_scr[1, 0:(nd - 1) * tk, :] = jnp.ones(((nd - 1) * tk, LANES), F32)

    def pair(j, kp):
        score(0, 2 * j * tk, 0, False)
        apply_pending(1, kp, 0)
        score(1, (2 * j + 1) * tk, 0, False)
        apply_pending(0, 2 * j * tk, 0)
        return (2 * j + 1) * tk

    n_pairs = (i * nd) // 2
    odd = n_pairs % 2
    k_pending = q0 + (nd - 1) * tk

    @pl.when(odd == 1)
    def _():
        pair(0, k_pending)

    k_pending = jnp.where(odd == 1, tk, k_pending)
    k_pending = lax.fori_loop(0, n_pairs // 2, lambda jj, kp: pair(odd + 2 * jj + 1, pair(odd + 2 * jj, kp)),
                              k_pending)
    apply_pending(1, k_pending, 0)
    o_ref[...] = (acc_scr[:, 0:HEAD_DIM] / acc_scr[:, HEAD_DIM:] * _silu(g_ref[...])).astype(o_ref.dtype)


def fox_prompt(q, k, v, gate, f2, tq, tk, rc, out_dtype):
    t, width = q.shape
    h = width // HEAD_DIM
    sel = jnp.asarray(_bias_selectors(h), BF16)
    q_spec = pl.BlockSpec((tq, HEAD_DIM), lambda hh, i: (i, hh))
    kv_spec = pl.BlockSpec((t, HEAD_DIM), lambda hh, i: (0, hh))
    sel_spec = pl.BlockSpec((1, _BIAS_PIECES, h, LANES), lambda hh, i: (hh, 0, 0, 0))
    return pl.pallas_call(
        functools.partial(_fox_prompt_body, tq=tq, tk=tk, rc=rc),
        grid=(h, t // tq),
        in_specs=[q_spec, kv_spec, kv_spec, q_spec, pl.BlockSpec((t, h), lambda hh, i: (0, 0)),
                  sel_spec, sel_spec],
        out_specs=q_spec,
        out_shape=jax.ShapeDtypeStruct((t, width), out_dtype),
        scratch_shapes=[pltpu.VMEM((t, 2 * HEAD_DIM), BF16), pltpu.VMEM((t, 2 * HEAD_DIM), BF16),
                        pltpu.VMEM((tq, LANES), F32), pltpu.VMEM((tq, 2 * HEAD_DIM), F32),
                        pltpu.VMEM((2, tq, tk), BF16), pltpu.VMEM((2, tq, LANES), F32)],
        compiler_params=_params("parallel", "arbitrary"),
        name="fox_prompt",
    )(q, k, v, gate, f2, sel[0], sel[1])


HEADS_PER_GROUP = 8


def _expand_queries(q, tq):
    gw = q.shape[1]
    rep = jnp.concatenate([q] * HEADS_PER_GROUP, axis=0)
    r = lax.broadcasted_iota(jnp.int32, (HEADS_PER_GROUP * tq, gw), 0) // tq
    cidx = lax.broadcasted_iota(jnp.int32, (HEADS_PER_GROUP * tq, gw), 1) // HEAD_DIM
    return jnp.where(r == cidx, rep, 0.0)


def _collect_heads(full, tq):
    return jnp.concatenate(
        [full[hh * tq:(hh + 1) * tq, hh * HEAD_DIM:(hh + 1) * HEAD_DIM] for hh in range(HEADS_PER_GROUP)],
        axis=1)


def _gather_keys(cache_ref, new_ref, scr, past, tk):
    g = HEADS_PER_GROUP
    x = cache_ref[0].reshape(past // g, g, g, HEAD_DIM)
    x = jnp.swapaxes(x, 1, 2)
    for hh in range(g):
        scr[0:past, hh * HEAD_DIM:(hh + 1) * HEAD_DIM] = x[:, hh].reshape(past, HEAD_DIM).astype(BF16)
    scr[past:tk, :] = new_ref[...].astype(BF16)


def _row_blocks(n):
    return [(r0, min(r0 + LANES, n)) for r0 in range(0, n, LANES)]


def _cumsum_rows(pieces, tri_ref):
    n = pieces[0].shape[0]
    out, carry = [], None
    for r0, r1 in _row_blocks(n):
        tri = tri_ref[0:r1 - r0, 0:r1 - r0]
        local = None
        for piece in pieces:
            d = _dot(tri, piece[r0:r1])
            local = d if local is None else local + d
        if carry is not None:
            local = local + carry
        carry = local[r1 - r0 - 1:r1 - r0]
        out.append(local)
    return jnp.concatenate(out, axis=0)


def _tailsum_rows(pieces, tri_ref):
    n = pieces[0].shape[0]
    out, carry = [], None
    for r0, r1 in reversed(_row_blocks(n)):
        tri = tri_ref[0:r1 - r0, 0:r1 - r0]
        local, total = None, None
        for piece in pieces:
            blk = piece[r0:r1]
            d = _dot_tn(tri, blk)
            local = d if local is None else local + d
            t = blk.astype(F32)
            total = t if total is None else total + t
        local = local - total
        if carry is not None:
            local = local + carry
        carry = local[0:1] + total[0:1]
        out.append(local)
    return jnp.concatenate(out[::-1], axis=0)


def _sb_decode_body(q_ref, kn_ref, vn_ref, kc_ref, vc_ref, g_ref, tri_ref, o_ref, k_scr, v_scr, *, past, tq):
    tk = past + tq
    _gather_keys(kc_ref, kn_ref, k_scr, past, tk)
    _gather_keys(vc_ref, vn_ref, v_scr, past, tk)

    lanes = HEADS_PER_GROUP * tq
    qx = _expand_queries(q_ref[...] * (HEAD_DIM ** -0.5), tq).astype(BF16)
    z = _dot_nt(k_scr[...], qx)
    kpos = lax.broadcasted_iota(jnp.int32, (tk, lanes), 0)
    qpos = past + lax.broadcasted_iota(jnp.int32, (tk, lanes), 1) % tq
    valid = kpos < qpos
    ls_neg = -(jnp.maximum(z, 0.0) + jnp.log(1.0 + jnp.exp(-jnp.abs(z))))
    lm = jnp.where(valid, ls_neg, 0.0)
    hi = lm.astype(BF16)
    lo = (lm - hi.astype(F32)).astype(BF16)
    tail = _tailsum_rows((hi, lo), tri_ref)
    w = jnp.where(valid, jnp.exp(z + ls_neg + tail), 0.0)
    full = _dot_tn(w.astype(BF16), v_scr[...])
    o_ref[...] = (_collect_heads(full, tq) * _silu(g_ref[...])).astype(o_ref.dtype)


def _cache_spec(layer, past):
    return pl.BlockSpec((None, 1, past, HEADS_PER_GROUP, HEAD_DIM), lambda b, g: (layer, b, 0, g, 0))


def sb_decode(q, k_new, v_new, k_cache, v_cache, layer, gate, batch, tq, out_dtype):
    n, width = q.shape
    past = k_cache.shape[2]
    tk = past + tq
    gw = HEADS_PER_GROUP * HEAD_DIM
    groups = width // gw
    tri = jnp.asarray(np.tril(np.ones((LANES, LANES), np.float32)), BF16)
    row_spec = pl.BlockSpec((tq, gw), lambda b, g: (b, g))
    cache_spec = _cache_spec(layer, past)
    return pl.pallas_call(
        functools.partial(_sb_decode_body, past=past, tq=tq),
        grid=(batch, groups),
        in_specs=[row_spec, row_spec, row_spec, cache_spec, cache_spec, row_spec,
                  pl.BlockSpec((LANES, LANES), lambda b, g: (0, 0))],
        out_specs=row_spec,
        out_shape=jax.ShapeDtypeStruct((n, width), out_dtype),
        scratch_shapes=[pltpu.VMEM((tk, gw), BF16), pltpu.VMEM((tk, gw), BF16)],
        compiler_params=_params("parallel", "parallel"),
        name="sb_decode",
    )(q, k_new, v_new, k_cache, v_cache, gate, tri)


def _fox_decode_body(q_ref, kn_ref, vn_ref, kc_ref, vc_ref, g_ref, lfn_ref, lfc_ref, ex_ref, tri_ref, o_ref,
                     k_scr, v_scr, lf_scr, *, past, tq):
    tk = past + tq
    _gather_keys(kc_ref, kn_ref, k_scr, past, tk)
    _gather_keys(vc_ref, vn_ref, v_scr, past, tk)
    lf_scr[0:past, :] = lfc_ref[0]
    lf_scr[past:tk, :] = lfn_ref[...]

    lanes = HEADS_PER_GROUP * tq
    hi, mid, lo = _split3(lf_scr[...])
    ex = ex_ref[0]
    lfx = _dot(hi, ex) + _dot(mid, ex) + _dot(lo, ex)
    f_key = _cumsum_rows(_split3(lfx), tri_ref)
    kpos = lax.broadcasted_iota(jnp.int32, (tk, lanes), 0)
    qpos = past + lax.broadcasted_iota(jnp.int32, (tk, lanes), 1) % tq
    f_query = jnp.sum(jnp.where(kpos == qpos, f_key, 0.0), axis=0, keepdims=True)

    qx = _expand_queries(q_ref[...] * (HEAD_DIM ** -0.5), tq).astype(BF16)
    s = _dot_nt(k_scr[...], qx) + (f_query - f_key)
    s = jnp.where(kpos <= qpos, s, -jnp.inf)
    p = jnp.exp(s - jnp.max(s, axis=0, keepdims=True))
    p = p / jnp.sum(p, axis=0, keepdims=True)
    full = _dot_tn(p.astype(BF16), v_scr[...])
    o_ref[...] = (_collect_heads(full, tq) * _silu(g_ref[...])).astype(o_ref.dtype)


def fox_decode(q, k_new, v_new, k_cache, v_cache, layer, gate, lf_new, lf_cache, batch, tq, out_dtype):
    n, width = q.shape
    h = width // HEAD_DIM
    past = k_cache.shape[2]
    tk = past + tq
    gw = HEADS_PER_GROUP * HEAD_DIM
    groups = width // gw
    lanes = HEADS_PER_GROUP * tq
    tri = jnp.asarray(np.tril(np.ones((LANES, LANES), np.float32)), BF16)
    head_of_lane = np.arange(lanes)[None, None, :] // tq + HEADS_PER_GROUP * np.arange(groups)[:, None, None]
    expand = jnp.asarray((np.arange(h)[None, :, None] == head_of_lane).astype(np.float32), BF16)
    row_spec = pl.BlockSpec((tq, gw), lambda b, g: (b, g))
    cache_spec = _cache_spec(layer, past)
    return pl.pallas_call(
        functools.partial(_fox_decode_body, past=past, tq=tq),
        grid=(batch, groups),
        in_specs=[row_spec, row_spec, row_spec, cache_spec, cache_spec, row_spec,
                  pl.BlockSpec((tq, h), lambda b, g: (b, 0)),
                  pl.BlockSpec((None, 1, past, h), lambda b, g: (layer, b, 0, 0)),
                  pl.BlockSpec((1, h, lanes), lambda b, g: (g, 0, 0)),
                  pl.BlockSpec((LANES, LANES), lambda b, g: (0, 0))],
        out_specs=row_spec,
        out_shape=jax.ShapeDtypeStruct((n, width), out_dtype),
        scratch_shapes=[pltpu.VMEM((tk, gw), BF16), pltpu.VMEM((tk, gw), BF16), pltpu.VMEM((tk, h), F32)],
        compiler_params=_params("parallel", "parallel"),
        name="fox_decode",
    )(q, k_new, v_new, k_cache, v_cache, gate, lf_new, lf_cache, expand, tri)


QK_SCALE_LOG2 = LOG2E * HEAD_DIM ** -0.5
FLAT32 = (F32, 1.0)
FLAT16 = (BF16, 1.0)
QUERY16 = (BF16, QK_SCALE_LOG2)


def _tile(m, pref):
    return pref if m % pref == 0 else m


def _even_layer(x, norm_w, w_in, w_out, lb, a_norm_w, s0, caches, batch, seq, prompt):
    n, d = x.shape
    half = d // 2
    tm = _tile(n, 1024)
    hn = rmsnorm(x, norm_w, BF16, _tile(n, 256))

    tn = 1024 if prompt else 512

    def p(group, *outs):
        return proj([hn], w_in, group * half, half, tm, tn, outs)

    (qa,), (fa,), (ia,), (ga,), (gb,) = p(0, FLAT32), p(1, FLAT32), p(2, FLAT32), p(3, FLAT32), p(7, FLAT32)
    if prompt:
        (qb,) = p(4, QUERY16)
        kb_leaf, kb = p(5, FLAT32, FLAT16)
        vb_leaf, vb = p(6, FLAT32, FLAT16)
        oa, s_new = hgrn2(qa, fa, ia, ga, lb, a_norm_w, s0, batch, seq, 1024, HGRN_CHUNK, 1, BF16)
        ob = sb_prompt(qb, kb, vb, gb, 1024, 256, 1024, BF16)
    else:
        (qb,), (kb,), (vb,) = p(4, FLAT32), p(5, FLAT32), p(6, FLAT32)
        kb_leaf, vb_leaf = kb, vb
        oa, s_new = hgrn2(qa, fa, ia, ga, lb, a_norm_w, s0, batch, seq, seq, seq, half // HEAD_DIM, BF16)
        ob = sb_decode(qb, kb, vb, caches[0], caches[1], caches[2], gb, batch, seq, BF16)
    (y,) = proj([oa, ob], w_out, 0, d, tm, tn, (FLAT32,), residual=x)
    return y, s_new, kb_leaf, vb_leaf


def _odd_layer(x, norm_w, w_in, w_fl, b_forget, w_out, caches, batch, seq, prompt):
    n, d = x.shape
    heads = d // HEAD_DIM
    tm = _tile(n, 1024)
    hn = rmsnorm(x, norm_w, BF16, _tile(n, 256))

    tn = 1024 if prompt else 512

    def p(group, *outs):
        return proj([hn], w_in, group * d, d, tm, tn, outs)

    (gate,) = p(3, FLAT32)
    (fl,) = proj([hn], w_fl, 0, heads, tm, heads, (FLAT32,))
    if prompt:
        (q,) = p(0, QUERY16)
        k_leaf, k = p(1, FLAT32, FLAT16)
        v_leaf, v = p(2, FLAT32, FLAT16)
        logf, f2 = logf_cumsum(fl, b_forget, 512)
        o = fox_prompt(q, k, v, gate, f2, 1024, 512, 256, BF16)
    else:
        (q,), (k,), (v,) = p(0, FLAT32), p(1, FLAT32), p(2, FLAT32)
        k_leaf, v_leaf = k, v
        logf = logf_only(fl, b_forget)
        o = fox_decode(q, k, v, caches[0], caches[1], caches[3], gate, logf, caches[2], batch, seq, BF16)
    (y,) = proj([o], w_out, 0, d, tm, tn, (FLAT32,), residual=x)
    return y, k_leaf, v_leaf, logf


def kernel(x_prompt, x_sample, state_a_hgrn, cache_b_k, cache_b_v, cache_c_k, cache_c_v, cache_c_logf,
           norm_w, final_norm_w, w_in_even, w_out_even, lb_logits, a_norm_w, w_in_odd, b_forget, w_out_odd):
    bp, tp, d = x_prompt.shape
    bs, ts, _ = x_sample.shape
    depth = norm_w.shape[0]
    n_even = w_in_even.shape[0]
    lb_all = jnp.cumsum(jax.nn.softmax(lb_logits.astype(F32), axis=0), axis=0)[:n_even]

    hp = x_prompt.reshape(bp * tp, d)
    hs = x_sample.reshape(bs * ts, d)
    outs = {name: [] for name in ("sa_p", "sa_s", "bk_p", "bv_p", "bk_s", "bv_s",
                                  "ck_p", "cv_p", "cf_p", "ck_s", "cv_s", "cf_s")}
    for layer in range(depth):
        j = layer // 2
        if layer % 2 == 0:
            a_heads = state_a_hgrn.shape[2]
            b_heads = cache_b_k.shape[3]
            zeros = jnp.zeros((bp, a_heads) + state_a_hgrn.shape[3:], F32)
            w_in = round_weights(w_in_even, j, 128)
            w_out = round_weights(w_out_even, j, 512)
            hp, sp, kp, vp = _even_layer(hp, norm_w[layer], w_in, w_out, lb_all[j], a_norm_w[j],
                                         zeros, None, bp, tp, True)
            hs, ss, ksm, vsm = _even_layer(hs, norm_w[layer], w_in, w_out, lb_all[j], a_norm_w[j],
                                           state_a_hgrn[j], (cache_b_k, cache_b_v, j), bs, ts, False)
            outs["sa_p"].append(sp); outs["sa_s"].append(ss)
            outs["bk_p"].append(kp.reshape(bp, tp, b_heads, HEAD_DIM))
            outs["bv_p"].append(vp.reshape(bp, tp, b_heads, HEAD_DIM))
            outs["bk_s"].append(ksm.reshape(bs, ts, b_heads, HEAD_DIM))
            outs["bv_s"].append(vsm.reshape(bs, ts, b_heads, HEAD_DIM))
        else:
            c_heads = cache_c_k.shape[3]
            w_in = w_in_odd[j].astype(BF16)
            w_fl = w_in[:, 4 * c_heads * HEAD_DIM:]
            w_out = round_weights(w_out_odd, j, 512)
            hp, kp, vp, fp = _odd_layer(hp, norm_w[layer], w_in, w_fl, b_forget[j], w_out, None, bp, tp, True)
            caches = (cache_c_k, cache_c_v, cache_c_logf, j)
            hs, ksm, vsm, fsm = _odd_layer(hs, norm_w[layer], w_in, w_fl, b_forget[j], w_out, caches, bs, ts, False)
            outs["ck_p"].append(kp.reshape(bp, tp, c_heads, HEAD_DIM))
            outs["cv_p"].append(vp.reshape(bp, tp, c_heads, HEAD_DIM))
            outs["cf_p"].append(fp.reshape(bp, tp, c_heads))
            outs["ck_s"].append(ksm.reshape(bs, ts, c_heads, HEAD_DIM))
            outs["cv_s"].append(vsm.reshape(bs, ts, c_heads, HEAD_DIM))
            outs["cf_s"].append(fsm.reshape(bs, ts, c_heads))
    y_prompt = rmsnorm(hp, final_norm_w, F32, _tile(bp * tp, 256)).reshape(bp, tp, d)
    y_sample = rmsnorm(hs, final_norm_w, F32, _tile(bs * ts, 256)).reshape(bs, ts, d)
    st = {k: jnp.stack(v) for k, v in outs.items()}
    return (y_prompt, y_sample, st["sa_p"], st["sa_s"], st["bk_p"], st["bv_p"], st["bk_s"], st["bv_s"],
            st["ck_p"], st["cv_p"], st["cf_p"], st["ck_s"], st["cv_s"], st["cf_s"])
```

```python
import functools

import numpy as np
import jax
import jax.numpy as jnp
from jax import lax
from jax.experimental import pallas as pl
from jax.experimental.pallas import tpu as pltpu

F32 = jnp.float32
BF16 = jnp.bfloat16

EPS = 1e-6
HEAD_DIM = 128
LANES = 128
LOG2E = 1.4426950408889634
HGRN_CHUNK = 64
VMEM_LIMIT_BYTES = 56 * 1024 * 1024

_NT = (((1,), (1,)), ((), ()))
_TN = (((0,), (0,)), ((), ()))


def _params(*sem):
    return pltpu.CompilerParams(dimension_semantics=sem, vmem_limit_bytes=VMEM_LIMIT_BYTES)


def _dot(a, b):
    return jnp.dot(a, b, preferred_element_type=F32)


def _dot_nt(a, b):
    return lax.dot_general(a, b, _NT, preferred_element_type=F32)


def _dot_tn(a, b):
    return lax.dot_general(a, b, _TN, preferred_element_type=F32)


def _split3(x):
    hi = x.astype(BF16)
    r1 = x - hi.astype(F32)
    mid = r1.astype(BF16)
    lo = (r1 - mid.astype(F32)).astype(BF16)
    return hi, mid, lo


def _dot_exact_lhs01(a01, x):
    hi, mid, lo = _split3(x)
    return _dot(a01, hi) + _dot(a01, mid) + _dot(a01, lo)


def _dot_exact_lhs01x3(a01x3, x):
    return _dot(a01x3, jnp.concatenate(_split3(x), axis=0))


def _sigmoid_pair(z):
    e = jnp.exp(-jnp.abs(z))
    r = 1.0 / (1.0 + e)
    er = e * r
    pos = z >= 0
    return jnp.where(pos, r, er), jnp.where(pos, er, r)


def _silu(x):
    return x * _sigmoid_pair(x)[0]


def _log_sigmoid(x):
    return jnp.minimum(x, 0.0) - jnp.log(1.0 + jnp.exp(-jnp.abs(x)))


def _rmsnorm_body(x_ref, w_ref, o_ref):
    x = x_ref[...]
    ms = jnp.mean(x * x, axis=-1, keepdims=True)
    o_ref[...] = (x * lax.rsqrt(ms + EPS) * w_ref[...]).astype(o_ref.dtype)


def rmsnorm(x, w, out_dtype, tm):
    m, d = x.shape
    return pl.pallas_call(
        _rmsnorm_body,
        grid=(m // tm,),
        in_specs=[pl.BlockSpec((tm, d), lambda i: (i, 0)), pl.BlockSpec((1, d), lambda i: (0, 0))],
        out_specs=pl.BlockSpec((tm, d), lambda i: (i, 0)),
        out_shape=jax.ShapeDtypeStruct((m, d), out_dtype),
        compiler_params=_params("parallel"),
        name="rmsnorm",
    )(x, w.reshape(1, d))


def _round_body(w_ref, o_ref):
    o_ref[...] = w_ref[...].astype(o_ref.dtype)


def round_weights(w, layer, rows):
    _, k, n = w.shape
    return pl.pallas_call(
        _round_body,
        grid=(k // rows,),
        in_specs=[pl.BlockSpec((None, rows, n), lambda i: (layer, i, 0))],
        out_specs=pl.BlockSpec((rows, n), lambda i: (i, 0)),
        out_shape=jax.ShapeDtypeStruct((k, n), BF16),
        compiler_params=_params("parallel"),
        name="round_weights",
    )(w)


def _round_proj_body(a_ref, w_ref, y_ref, wb_ref):
    wb = w_ref[...].astype(BF16)
    wb_ref[...] = wb
    y_ref[...] = _dot(a_ref[...], wb)


def round_and_proj(a, w, layer, ncols, tn):
    m, kk = a.shape
    return pl.pallas_call(
        _round_proj_body,
        grid=(ncols // tn,),
        in_specs=[pl.BlockSpec((m, kk), lambda j: (0, 0)), pl.BlockSpec((None, kk, tn), lambda j: (layer, 0, j))],
        out_specs=[pl.BlockSpec((m, tn), lambda j: (0, j)), pl.BlockSpec((kk, tn), lambda j: (0, j))],
        out_shape=[jax.ShapeDtypeStruct((m, ncols), F32), jax.ShapeDtypeStruct((kk, ncols), BF16)],
        compiler_params=_params("parallel"),
        name="round_and_proj",
    )(a, w)


def _proj_body(*refs, n_in, has_residual, outs):
    a_refs = refs[:n_in]
    w_refs = refs[n_in:2 * n_in]
    pos = 2 * n_in
    r_ref = refs[pos] if has_residual else None
    o_refs = refs[pos + int(has_residual):]
    acc = None
    for a_ref, w_ref in zip(a_refs, w_refs):
        d = _dot(a_ref[...], w_ref[...])
        acc = d if acc is None else acc + d
    if has_residual:
        acc = r_ref[...] + acc
    for o_ref, (_, scale) in zip(o_refs, outs):
        val = acc if scale == 1.0 else acc * scale
        o_ref[...] = val.astype(o_ref.dtype)


def proj(a_list, w, col0, ncols, tm, tn, outs, residual=None):
    m = a_list[0].shape[0]
    kk = a_list[0].shape[1]
    assert all(a.shape == (m, kk) for a in a_list) and w.shape[0] == kk * len(a_list)
    assert m % tm == 0 and ncols % tn == 0 and col0 % tn == 0
    cb = col0 // tn
    in_specs = [pl.BlockSpec((tm, kk), lambda i, j: (i, 0)) for _ in a_list]
    in_specs += [pl.BlockSpec((kk, tn), functools.partial(lambda i, j, r: (r, cb + j), r=r)) for r in range(len(a_list))]
    args = list(a_list) + [w] * len(a_list)
    if residual is not None:
        in_specs.append(pl.BlockSpec((tm, tn), lambda i, j: (i, j)))
        args.append(residual)
    out_specs = [pl.BlockSpec((tm, tn), lambda i, j: (i, j)) for _ in outs]
    out_shape = [jax.ShapeDtypeStruct((m, ncols), dtype) for dtype, _ in outs]
    return pl.pallas_call(
        functools.partial(_proj_body, n_in=len(a_list), has_residual=residual is not None, outs=tuple(outs)),
        grid=(m // tm, ncols // tn),
        in_specs=in_specs,
        out_specs=out_specs,
        out_shape=out_shape,
        compiler_params=_params("parallel", "parallel"),
        name="proj",
    )(*args)


def _hgrn_maps(c):
    levels = int(np.log2(c))
    assert 2 ** levels == c
    t = np.arange(c)[:, None]
    s = np.arange(c)[None, :]
    mats = [(s <= t), (s > t)]
    for l in range(levels):
        b = 2 ** l
        start = (t // (2 * b)) * (2 * b)
        upper = (t // b) % 2 == 1
        mats.append((upper & (s >= start + b) & (s <= t)) | ((~upper) & (s > t) & (s <= start + b - 1)))
    return np.concatenate(mats, axis=0).astype(np.float32), levels


def _hgrn_body(q_ref, z_ref, v_ref, g_ref, lb_ref, nw_ref, a_ref, s0_ref, o_ref, sout_ref, st_scr,
               *, c, n_chunks, levels, heads):
    tb = pl.program_id(2)

    @pl.when(tb == 0)
    def _():
        for hh in range(heads):
            st_scr[hh] = s0_ref[0, hh].T

    nw = nw_ref[...]
    amat = a_ref[...]
    row = lax.broadcasted_iota(jnp.int32, (c, c), 0)
    col = lax.broadcasted_iota(jnp.int32, (c, c), 1)
    xor = row ^ col

    units = [(slice(ci * c, (ci + 1) * c), slice(hh * HEAD_DIM, (hh + 1) * HEAD_DIM))
             for hh in range(heads) for ci in range(n_chunks)]
    nu = len(units)
    lbs = [lb_ref[:, cols] for _, cols in units]
    qs = [_silu(q_ref[rows, cols]) for rows, cols in units]
    sigs = [_sigmoid_pair(z_ref[rows, cols]) for rows, cols in units]
    gs = [jnp.log(lbs[u] + (1.0 - lbs[u]) * sigs[u][0]) for u in range(nu)]
    ks = [(1.0 - lbs[u]) * sigs[u][1] for u in range(nu)]
    vs = [v_ref[rows, cols] for rows, cols in units]
    vbs = [v.astype(BF16) for v in vs]
    es = [jnp.exp(_dot_exact_lhs01x3(amat, g)) for g in gs]
    atts = [None] * nu
    for l in range(levels - 1, -1, -1):
        for u in range(nu):
            el = es[u][(2 + l) * c:(3 + l) * c]
            al = _dot_nt((qs[u] * el).astype(BF16), (ks[u] * el).astype(BF16))
            atts[u] = al if atts[u] is None else jnp.where(xor < 2 ** (l + 1), al, atts[u])
    atts = [jnp.where(row > col, att, 0.0).astype(BF16) for att in atts]
    o_intras = [_dot(atts[u], vbs[u]) + jnp.sum(qs[u] * ks[u], axis=-1, keepdims=True) * vs[u] for u in range(nu)]
    upds = [_dot_tn(vbs[u], (ks[u] * es[u][c:2 * c]).astype(BF16)) for u in range(nu)]
    qgs = [(qs[u] * es[u][0:c]).astype(BF16) for u in range(nu)]

    for hh in range(heads):
        st = st_scr[hh]
        for ci in range(n_chunks):
            u = hh * n_chunks + ci
            rows, cols = units[u]
            o = _dot_nt(qgs[u], st.astype(BF16)) + o_intras[u]
            st = st * es[u][c - 1:c] + upds[u]
            ms = jnp.mean(o * o, axis=-1, keepdims=True)
            y = o * lax.rsqrt(ms + EPS) * nw
            o_ref[rows, cols] = (y * _silu(g_ref[rows, cols])).astype(o_ref.dtype)
        st_scr[hh] = st

    @pl.when(tb == pl.num_programs(2) - 1)
    def _():
        for hh in range(heads):
            sout_ref[0, hh] = st_scr[hh].T


def hgrn2(qa, fa, ia, ga, lb, a_norm_w, s0, batch, seq, rows_per_step, c, heads_per_step, out_dtype,
          col0=(0, 0, 0, 0)):
    n = qa.shape[0]
    width = lb.shape[0]
    h = width // HEAD_DIM
    assert n == batch * seq and seq % rows_per_step == 0 and rows_per_step % c == 0 and h % heads_per_step == 0
    nb = seq // rows_per_step
    gw = heads_per_step * HEAD_DIM
    amat_np, levels = _hgrn_maps(c)
    amat = jnp.asarray(np.concatenate([amat_np] * 3, axis=1), BF16)
    row_spec = pl.BlockSpec((rows_per_step, gw), lambda b, hh, t: (b * nb + t, hh))
    in_row_specs = [pl.BlockSpec((rows_per_step, gw), functools.partial(lambda b, hh, t, off: (b * nb + t, off + hh), off=c0 // gw))
                    for c0 in col0]
    assert all(c0 % gw == 0 for c0 in col0)
    state_spec = pl.BlockSpec((1, heads_per_step, HEAD_DIM, HEAD_DIM), lambda b, hh, t: (b, hh, 0, 0))
    body = functools.partial(_hgrn_body, c=c, n_chunks=rows_per_step // c, levels=levels, heads=heads_per_step)
    return pl.pallas_call(
        body,
        grid=(batch, h // heads_per_step, nb),
        in_specs=in_row_specs + [
                  pl.BlockSpec((1, gw), lambda b, hh, t: (0, hh)),
                  pl.BlockSpec((1, HEAD_DIM), lambda b, hh, t: (0, 0)),
                  pl.BlockSpec(amat.shape, lambda b, hh, t: (0, 0)),
                  state_spec],
        out_specs=[row_spec, state_spec],
        out_shape=[jax.ShapeDtypeStruct((n, width), out_dtype),
                   jax.ShapeDtypeStruct((batch, h, HEAD_DIM, HEAD_DIM), F32)],
        scratch_shapes=[pltpu.VMEM((heads_per_step, HEAD_DIM, HEAD_DIM), F32)],
        compiler_params=_params("parallel", "parallel", "arbitrary"),
        name="hgrn2",
    )(qa, fa, ia, ga, lb.reshape(1, width), a_norm_w.reshape(1, HEAD_DIM), amat, s0)


def _neg_abs(x):
    return -jnp.abs(x)


def _sb_prompt_body(q_ref, k_ref, v_ref, g_ref, uu_ref, o_ref, run_scr, acc_scr, w_scr, *, tq, tk, rc):
    i = pl.program_id(1)
    nd = tq // tk
    nl = tk // LANES
    q0 = pl.multiple_of(i * tq, tq)
    run_scr[...] = jnp.zeros(run_scr.shape, F32)
    acc_scr[...] = jnp.zeros(acc_scr.shape, F32)

    def apply_pending(slot, k_prev, first_row):
        vt = v_ref[pl.ds(pl.multiple_of(k_prev, tk), tk), :]
        for r0 in range(first_row, tq, rc):
            rows = slice(r0, min(r0 + rc, tq))
            acc_scr[rows, :] = acc_scr[rows, :] + _dot(w_scr[slot, rows, :], vt)

    def score(slot, k0, first_row, masked):
        kt = k_ref[pl.ds(pl.multiple_of(k0, tk), tk), :]
        for r0 in range(first_row, tq, rc):
            r1 = min(r0 + rc, tq)
            rows = slice(r0, r1)
            z = _dot_nt(q_ref[rows, :], kt)
            if masked:
                qpos = lax.broadcasted_iota(jnp.int32, (r1 - r0, LANES), 0) + (q0 + r0)
                kpos = lax.broadcasted_iota(jnp.int32, (r1 - r0, LANES), 1) + k0
            zs, sps, his, los, valids = [], [], [], [], []
            rowsum = None
            for c in range(nl):
                zc = z[:, c * LANES:(c + 1) * LANES]
                sp = jnp.maximum(zc, 0.0) + jnp.log2(1.0 + jnp.exp2(_neg_abs(zc)))
                if masked:
                    valid = (kpos + c * LANES) < qpos
                    sp = jnp.where(valid, sp, 0.0)
                    valids.append(valid)
                hi = sp.astype(BF16)
                lo = (sp - hi.astype(F32)).astype(BF16)
                zs.append(zc); sps.append(sp); his.append(hi); los.append(lo)
                rowsum = sp if rowsum is None else rowsum + sp
            tail = _dot(jnp.concatenate(his + los, axis=1), uu_ref[...])
            run = run_scr[rows, :]
            for c in range(nl):
                w = jnp.exp2(zs[c] - (sps[c] + tail[:, c * LANES:(c + 1) * LANES] + run))
                if masked:
                    w = jnp.where(valids[c], w, 0.0)
                w_scr[slot, rows, c * LANES:(c + 1) * LANES] = w.astype(BF16)
            run_scr[rows, :] = run + jnp.sum(rowsum, axis=-1, keepdims=True)

    assert nd % 2 == 0
    for d in range(nd - 1, 0, -2):
        score(0, q0 + d * tk, d * tk, True)
        if d + 1 < nd:
            apply_pending(1, q0 + (d + 1) * tk, (d + 1) * tk)
        score(1, q0 + (d - 1) * tk, (d - 1) * tk, True)
        apply_pending(0, q0 + d * tk, d * tk)

    def before(jj, kp):
        for u in range(unroll):
            k_a = q0 - (2 * (unroll * jj + u) + 1) * tk
            score(0, k_a, 0, False)
            apply_pending(1, kp, 0)
            score(1, k_a - tk, 0, False)
            apply_pending(0, k_a, 0)
            kp = k_a - tk
        return kp

    unroll = 2
    assert nd % (2 * unroll) == 0
    k_pending = lax.fori_loop(0, (i * nd) // (2 * unroll), before, q0)
    apply_pending(1, k_pending, 0)
    o_ref[...] = (acc_scr[...] * _silu(g_ref[...])).astype(o_ref.dtype)


def _tail_matrix(n):
    sp = np.arange(n)[:, None]
    s = np.arange(n)[None, :]
    return (sp > s).astype(np.float32)


def sb_prompt(q, k, v, gate, tq, tk, rc, out_dtype):
    t, width = q.shape
    h = width // HEAD_DIM
    u = _tail_matrix(tk)
    uu = jnp.asarray(np.concatenate([u, u], axis=0), BF16)
    q_spec = pl.BlockSpec((tq, HEAD_DIM), lambda hh, i: (i, hh))
    kv_spec = pl.BlockSpec((t, HEAD_DIM), lambda hh, i: (0, hh))
    return pl.pallas_call(
        functools.partial(_sb_prompt_body, tq=tq, tk=tk, rc=rc),
        grid=(h, t // tq),
        in_specs=[q_spec, kv_spec, kv_spec, q_spec, pl.BlockSpec(uu.shape, lambda hh, i: (0, 0))],
        out_specs=q_spec,
        out_shape=jax.ShapeDtypeStruct((t, width), out_dtype),
        scratch_shapes=[pltpu.VMEM((tq, LANES), F32), pltpu.VMEM((tq, HEAD_DIM), F32),
                        pltpu.VMEM((2, tq, tk), BF16)],
        compiler_params=_params("parallel", "arbitrary"),
        name="sb_prompt",
    )(q, k, v, gate, uu)


def _logf_cumsum_body(fl_ref, b_ref, tri_ref, lf_ref, f2_ref, carry_scr):
    @pl.when(pl.program_id(0) == 0)
    def _():
        carry_scr[...] = jnp.zeros_like(carry_scr)

    lf = _log_sigmoid(fl_ref[...] + b_ref[...])
    lf_ref[...] = lf
    f = carry_scr[...] + _dot_exact_lhs01(tri_ref[...], lf)
    f2_ref[...] = f * LOG2E
    carry_scr[...] = f[f.shape[0] - 1:, :]


def logf_cumsum(fl, b_forget, blk):
    t, h = fl.shape
    tri = jnp.asarray(np.tril(np.ones((blk, blk), np.float32)), BF16)
    spec = pl.BlockSpec((blk, h), lambda i: (i, 0))
    return pl.pallas_call(
        _logf_cumsum_body,
        grid=(t // blk,),
        in_specs=[spec, pl.BlockSpec((1, h), lambda i: (0, 0)), pl.BlockSpec((blk, blk), lambda i: (0, 0))],
        out_specs=[spec, spec],
        out_shape=[jax.ShapeDtypeStruct((t, h), F32), jax.ShapeDtypeStruct((t, h), F32)],
        scratch_shapes=[pltpu.VMEM((1, h), F32)],
        compiler_params=_params("arbitrary"),
        name="logf_cumsum",
    )(fl, b_forget.reshape(1, h), tri)


def _logf_body(fl_ref, b_ref, lf_ref):
    lf_ref[...] = _log_sigmoid(fl_ref[...] + b_ref[...])


def logf_only(fl, b_forget):
    t, h = fl.shape
    return pl.pallas_call(
        _logf_body,
        grid=(1,),
        in_specs=[pl.BlockSpec((t, h), lambda i: (0, 0)), pl.BlockSpec((1, h), lambda i: (0, 0))],
        out_specs=pl.BlockSpec((t, h), lambda i: (0, 0)),
        out_shape=jax.ShapeDtypeStruct((t, h), F32),
        name="logf",
    )(fl, b_forget.reshape(1, h))


_BIAS_PIECES = 3


def _bias_selectors(h):
    sel = np.zeros((2, h, _BIAS_PIECES, h, LANES), np.float32)
    for hh in range(h):
        for p in range(_BIAS_PIECES):
            sel[0, hh, p, hh, p] = 1.0
            sel[1, hh, p, hh, _BIAS_PIECES + p] = -1.0
    return sel


def _bias_columns(f, sel_ref, query_side):
    out = None
    for p, piece in enumerate(_split3(f)):
        d = _dot(piece, sel_ref[0, p])
        out = d if out is None else out + d
    lane = lax.broadcasted_iota(jnp.int32, out.shape, 1)
    ones_at = (lane >= _BIAS_PIECES) & (lane < 2 * _BIAS_PIECES) if query_side else lane < _BIAS_PIECES
    return jnp.where(ones_at, 1.0, out).astype(BF16)


def _fox_prompt_body(q_ref, k_ref, v_ref, g_ref, f_ref, selq_ref, selk_ref, o_ref,
                     kx_scr, vx_scr, m_scr, acc_scr, p_scr, alpha_scr, *, tq, tk, rc):
    i = pl.program_id(1)
    nd = tq // tk
    nl = tk // LANES
    t_all = k_ref.shape[0]

    @pl.when(i == 0)
    def _():
        kx_scr[:, 0:HEAD_DIM] = k_ref[...]
        kx_scr[:, HEAD_DIM:] = _bias_columns(f_ref[...], selk_ref, False)
        vx_scr[:, 0:HEAD_DIM] = v_ref[...]
        vx_scr[:, HEAD_DIM:] = jnp.ones((t_all, LANES), BF16)

    q0 = pl.multiple_of(i * tq, tq)
    qx = jnp.concatenate([q_ref[...], _bias_columns(f_ref[pl.ds(q0, tq), :], selq_ref, True)], axis=1)
    m_scr[...] = jnp.full(m_scr.shape, -jnp.inf, F32)
    acc_scr[...] = jnp.zeros(acc_scr.shape, F32)

    def apply_pending(slot, k_prev, first_row):
        vxt = vx_scr[pl.ds(pl.multiple_of(k_prev, tk), tk), :]
        for r0 in range(first_row, tq, rc):
            rows = slice(r0, r0 + rc)
            pv = _dot(p_scr[slot, rows, :], vxt)
            alpha = alpha_scr[slot, rows, :]
            acc_scr[rows, 0:HEAD_DIM] = alpha * acc_scr[rows, 0:HEAD_DIM] + pv[:, 0:HEAD_DIM]
            acc_scr[rows, HEAD_DIM:] = alpha * acc_scr[rows, HEAD_DIM:] + pv[:, HEAD_DIM:]

    def score(slot, k0, first_row, masked):
        kxt = kx_scr[pl.ds(pl.multiple_of(k0, tk), tk), :]
        for r0 in range(first_row, tq, rc):
            rows = slice(r0, r0 + rc)
            s = _dot_nt(qx[r0:r0 + rc, :], kxt)
            sb = [s[:, c * LANES:(c + 1) * LANES] for c in range(nl)]
            if masked:
                qpos = lax.broadcasted_iota(jnp.int32, (rc, LANES), 0) + (q0 + r0)
                kpos = lax.broadcasted_iota(jnp.int32, (rc, LANES), 1) + k0
                sb = [jnp.where((kpos + c * LANES) <= qpos, sb[c], -jnp.inf) for c in range(nl)]
            mx = sb[0]
            for c in range(1, nl):
                mx = jnp.maximum(mx, sb[c])
            m_old = m_scr[rows, :]
            m_new = jnp.maximum(m_old, jnp.max(mx, axis=-1, keepdims=True))
            alpha_scr[slot, rows, :] = jnp.exp2(m_old - m_new)
            for c in range(nl):
                p_scr[slot, rows, c * LANES:(c + 1) * LANES] = jnp.exp2(sb[c] - m_new).astype(BF16)
            m_scr[rows, :] = m_new

    assert nd % 2 == 0 and tk % rc == 0
    for d in range(0, nd, 2):
        score(0, q0 + d * tk, d * tk, True)
        if d > 0:
            apply_pending(1, q0 + (d - 1) * tk, (d - 1) * tk)
        score(1, q0 + (d + 1) * tk, (d + 1) * tk, True)
        apply_pending(0, q0 + d * tk, d * tk)

    p_scr[1, 0:(nd - 1) * tk, :] = jnp.zeros(((nd - 1) * tk, tk), BF16)
    alpha_scr[1, 0:(nd - 1) * tk, :] = jnp.ones(((nd - 1) * tk, LANES), F32)

    def pair(j, kp):
        score(0, 2 * j * tk, 0, False)
        apply_pending(1, kp, 0)
        score(1, (2 * j + 1) * tk, 0, False)
        apply_pending(0, 2 * j * tk, 0)
        return (2 * j + 1) * tk

    n_pairs = (i * nd) // 2
    odd = n_pairs % 2
    k_pending = q0 + (nd - 1) * tk

    @pl.when(odd == 1)
    def _():
        pair(0, k_pending)

    k_pending = jnp.where(odd == 1, tk, k_pending)
    k_pending = lax.fori_loop(0, n_pairs // 2, lambda jj, kp: pair(odd + 2 * jj + 1, pair(odd + 2 * jj, kp)),
                              k_pending)
    apply_pending(1, k_pending, 0)
    o_ref[...] = (acc_scr[:, 0:HEAD_DIM] / acc_scr[:, HEAD_DIM:] * _silu(g_ref[...])).astype(o_ref.dtype)


def fox_prompt(q, k, v, gate, f2, tq, tk, rc, out_dtype):
    t, width = q.shape
    h = width // HEAD_DIM
    sel = jnp.asarray(_bias_selectors(h), BF16)
    q_spec = pl.BlockSpec((tq, HEAD_DIM), lambda hh, i: (i, hh))
    kv_spec = pl.BlockSpec((t, HEAD_DIM), lambda hh, i: (0, hh))
    sel_spec = pl.BlockSpec((1, _BIAS_PIECES, h, LANES), lambda hh, i: (hh, 0, 0, 0))
    return pl.pallas_call(
        functools.partial(_fox_prompt_body, tq=tq, tk=tk, rc=rc),
        grid=(h, t // tq),
        in_specs=[q_spec, kv_spec, kv_spec, q_spec, pl.BlockSpec((t, h), lambda hh, i: (0, 0)),
                  sel_spec, sel_spec],
        out_specs=q_spec,
        out_shape=jax.ShapeDtypeStruct((t, width), out_dtype),
        scratch_shapes=[pltpu.VMEM((t, 2 * HEAD_DIM), BF16), pltpu.VMEM((t, 2 * HEAD_DIM), BF16),
                        pltpu.VMEM((tq, LANES), F32), pltpu.VMEM((tq, 2 * HEAD_DIM), F32),
                        pltpu.VMEM((2, tq, tk), BF16), pltpu.VMEM((2, tq, LANES), F32)],
        compiler_params=_params("parallel", "arbitrary"),
        name="fox_prompt",
    )(q, k, v, gate, f2, sel[0], sel[1])


HEADS_PER_GROUP = 8


def _expand_queries(q, tq):
    gw = q.shape[1]
    rep = jnp.concatenate([q] * HEADS_PER_GROUP, axis=0)
    r = lax.broadcasted_iota(jnp.int32, (HEADS_PER_GROUP * tq, gw), 0) // tq
    cidx = lax.broadcasted_iota(jnp.int32, (HEADS_PER_GROUP * tq, gw), 1) // HEAD_DIM
    return jnp.where(r == cidx, rep, 0.0)


def _collect_heads(full, tq):
    return jnp.concatenate(
        [full[hh * tq:(hh + 1) * tq, hh * HEAD_DIM:(hh + 1) * HEAD_DIM] for hh in range(HEADS_PER_GROUP)],
        axis=1)


def _gather_keys(cache_ref, new_ref, scr, past, tk):
    g = HEADS_PER_GROUP
    x = cache_ref[0].reshape(past // g, g, g, HEAD_DIM)
    x = jnp.swapaxes(x, 1, 2)
    for hh in range(g):
        scr[0:past, hh * HEAD_DIM:(hh + 1) * HEAD_DIM] = x[:, hh].reshape(past, HEAD_DIM).astype(BF16)
    scr[past:tk, :] = new_ref[...].astype(BF16)


def _row_blocks(n):
    return [(r0, min(r0 + LANES, n)) for r0 in range(0, n, LANES)]


def _cumsum_rows(pieces, tri_ref):
    n = pieces[0].shape[0]
    out, carry = [], None
    for r0, r1 in _row_blocks(n):
        tri = tri_ref[0:r1 - r0, 0:r1 - r0]
        local = None
        for piece in pieces:
            d = _dot(tri, piece[r0:r1])
            local = d if local is None else local + d
        if carry is not None:
            local = local + carry
        carry = local[r1 - r0 - 1:r1 - r0]
        out.append(local)
    return jnp.concatenate(out, axis=0)


def _tailsum_rows(pieces, tri_ref):
    n = pieces[0].shape[0]
    out, carry = [], None
    for r0, r1 in reversed(_row_blocks(n)):
        tri = tri_ref[0:r1 - r0, 0:r1 - r0]
        local, total = None, None
        for piece in pieces:
            blk = piece[r0:r1]
            d = _dot_tn(tri, blk)
            local = d if local is None else local + d
            t = blk.astype(F32)
            total = t if total is None else total + t
        local = local - total
        if carry is not None:
            local = local + carry
        carry = local[0:1] + total[0:1]
        out.append(local)
    return jnp.concatenate(out[::-1], axis=0)


def _sb_decode_body(q_ref, kn_ref, vn_ref, kc_ref, vc_ref, g_ref, tri_ref, o_ref, k_scr, v_scr, *, past, tq):
    tk = past + tq
    _gather_keys(kc_ref, kn_ref, k_scr, past, tk)
    _gather_keys(vc_ref, vn_ref, v_scr, past, tk)

    lanes = HEADS_PER_GROUP * tq
    qx = _expand_queries(q_ref[...] * (HEAD_DIM ** -0.5), tq).astype(BF16)
    z = _dot_nt(k_scr[...], qx)
    kpos = lax.broadcasted_iota(jnp.int32, (tk, lanes), 0)
    qpos = past + lax.broadcasted_iota(jnp.int32, (tk, lanes), 1) % tq
    valid = kpos < qpos
    ls_neg = -(jnp.maximum(z, 0.0) + jnp.log(1.0 + jnp.exp(-jnp.abs(z))))
    lm = jnp.where(valid, ls_neg, 0.0)
    hi = lm.astype(BF16)
    lo = (lm - hi.astype(F32)).astype(BF16)
    tail = _tailsum_rows((hi, lo), tri_ref)
    w = jnp.where(valid, jnp.exp(z + ls_neg + tail), 0.0)
    full = _dot_tn(w.astype(BF16), v_scr[...])
    o_ref[...] = (_collect_heads(full, tq) * _silu(g_ref[...])).astype(o_ref.dtype)


def _cache_spec(layer, past):
    return pl.BlockSpec((None, 1, past, HEADS_PER_GROUP, HEAD_DIM), lambda b, g: (layer, b, 0, g, 0))


def _decode_row_specs(tq, gw, col0):
    assert all(c0 % gw == 0 for c0 in col0)
    return [pl.BlockSpec((tq, gw), functools.partial(lambda b, g, off: (b, off + g), off=c0 // gw)) for c0 in col0]


def sb_decode(q, k_new, v_new, k_cache, v_cache, layer, gate, batch, tq, out_dtype, col0=(0, 0, 0, 0)):
    n = q.shape[0]
    width = k_cache.shape[3] * HEAD_DIM
    past = k_cache.shape[2]
    tk = past + tq
    gw = HEADS_PER_GROUP * HEAD_DIM
    groups = width // gw
    tri = jnp.asarray(np.tril(np.ones((LANES, LANES), np.float32)), BF16)
    row_spec = pl.BlockSpec((tq, gw), lambda b, g: (b, g))
    rs = _decode_row_specs(tq, gw, col0)
    cache_spec = _cache_spec(layer, past)
    return pl.pallas_call(
        functools.partial(_sb_decode_body, past=past, tq=tq),
        grid=(batch, groups),
        in_specs=[rs[0], rs[1], rs[2], cache_spec, cache_spec, rs[3],
                  pl.BlockSpec((LANES, LANES), lambda b, g: (0, 0))],
        out_specs=row_spec,
        out_shape=jax.ShapeDtypeStruct((n, width), out_dtype),
        scratch_shapes=[pltpu.VMEM((tk, gw), BF16), pltpu.VMEM((tk, gw), BF16)],
        compiler_params=_params("parallel", "parallel"),
        name="sb_decode",
    )(q, k_new, v_new, k_cache, v_cache, gate, tri)


def _fox_decode_body(q_ref, kn_ref, vn_ref, kc_ref, vc_ref, g_ref, lfn_ref, lfc_ref, ex_ref, tri_ref, o_ref,
                     k_scr, v_scr, lf_scr, *, past, tq):
    tk = past + tq
    _gather_keys(kc_ref, kn_ref, k_scr, past, tk)
    _gather_keys(vc_ref, vn_ref, v_scr, past, tk)
    lf_scr[0:past, :] = lfc_ref[0]
    lf_scr[past:tk, :] = lfn_ref[...]

    lanes = HEADS_PER_GROUP * tq
    hi, mid, lo = _split3(lf_scr[...])
    ex = ex_ref[0]
    lfx = _dot(hi, ex) + _dot(mid, ex) + _dot(lo, ex)
    f_key = _cumsum_rows(_split3(lfx), tri_ref)
    kpos = lax.broadcasted_iota(jnp.int32, (tk, lanes), 0)
    qpos = past + lax.broadcasted_iota(jnp.int32, (tk, lanes), 1) % tq
    f_query = jnp.sum(jnp.where(kpos == qpos, f_key, 0.0), axis=0, keepdims=True)

    qx = _expand_queries(q_ref[...] * (HEAD_DIM ** -0.5), tq).astype(BF16)
    s = _dot_nt(k_scr[...], qx) + (f_query - f_key)
    s = jnp.where(kpos <= qpos, s, -jnp.inf)
    p = jnp.exp(s - jnp.max(s, axis=0, keepdims=True))
    p = p / jnp.sum(p, axis=0, keepdims=True)
    full = _dot_tn(p.astype(BF16), v_scr[...])
    o_ref[...] = (_collect_heads(full, tq) * _silu(g_ref[...])).astype(o_ref.dtype)


def fox_decode(q, k_new, v_new, k_cache, v_cache, layer, gate, lf_new, lf_cache, batch, tq, out_dtype,
               col0=(0, 0, 0, 0)):
    n = q.shape[0]
    h = k_cache.shape[3]
    width = h * HEAD_DIM
    past = k_cache.shape[2]
    tk = past + tq
    gw = HEADS_PER_GROUP * HEAD_DIM
    groups = width // gw
    lanes = HEADS_PER_GROUP * tq
    tri = jnp.asarray(np.tril(np.ones((LANES, LANES), np.float32)), BF16)
    head_of_lane = np.arange(lanes)[None, None, :] // tq + HEADS_PER_GROUP * np.arange(groups)[:, None, None]
    expand = jnp.asarray((np.arange(h)[None, :, None] == head_of_lane).astype(np.float32), BF16)
    row_spec = pl.BlockSpec((tq, gw), lambda b, g: (b, g))
    rs = _decode_row_specs(tq, gw, col0)
    cache_spec = _cache_spec(layer, past)
    return pl.pallas_call(
        functools.partial(_fox_decode_body, past=past, tq=tq),
        grid=(batch, groups),
        in_specs=[rs[0], rs[1], rs[2], cache_spec, cache_spec, rs[3],
                  pl.BlockSpec((tq, h), lambda b, g: (b, 0)),
                  pl.BlockSpec((None, 1, past, h), lambda b, g: (layer, b, 0, 0)),
                  pl.BlockSpec((1, h, lanes), lambda b, g: (g, 0, 0)),
                  pl.BlockSpec((LANES, LANES), lambda b, g: (0, 0))],
        out_specs=row_spec,
        out_shape=jax.ShapeDtypeStruct((n, width), out_dtype),
        scratch_shapes=[pltpu.VMEM((tk, gw), BF16), pltpu.VMEM((tk, gw), BF16), pltpu.VMEM((tk, h), F32)],
        compiler_params=_params("parallel", "parallel"),
        name="fox_decode",
    )(q, k_new, v_new, k_cache, v_cache, gate, lf_new, lf_cache, expand, tri)


QK_SCALE_LOG2 = LOG2E * HEAD_DIM ** -0.5
FLAT32 = (F32, 1.0)
FLAT16 = (BF16, 1.0)
QUERY16 = (BF16, QK_SCALE_LOG2)


def _tile(m, pref):
    return pref if m % pref == 0 else m


def _even_prompt(x, norm_w, w_in, w_out, lb, a_norm_w, s0, seq):
    n, d = x.shape
    half = d // 2
    hn = rmsnorm(x, norm_w, BF16, 256)

    def p(group, *outs):
        return proj([hn], w_in, group * half, half, 1024, 1024, outs)

    (qa,), (fa,), (ia,), (ga,), (gb,) = p(0, FLAT32), p(1, FLAT32), p(2, FLAT32), p(3, FLAT32), p(7, FLAT32)
    (qb,) = p(4, QUERY16)
    kb_leaf, kb = p(5, FLAT32, FLAT16)
    vb_leaf, vb = p(6, FLAT32, FLAT16)
    oa, s_new = hgrn2(qa, fa, ia, ga, lb, a_norm_w, s0, 1, seq, 1024, HGRN_CHUNK, 1, BF16)
    ob = sb_prompt(qb, kb, vb, gb, 1024, 256, 1024, BF16)
    (y,) = proj([oa, ob], w_out, 0, d, 1024, 1024, (FLAT32,), residual=x)
    return y, s_new, kb_leaf, vb_leaf


def _even_decode(x, norm_w, w_in_f32, layer, w_out, lb, a_norm_w, s0, k_cache, v_cache, batch, seq):
    n, d = x.shape
    half = d // 2
    hn = rmsnorm(x, norm_w, BF16, n)
    cols, w_in = round_and_proj(hn, w_in_f32, layer, 8 * half, 512)
    oa, s_new = hgrn2(cols, cols, cols, cols, lb, a_norm_w, s0, batch, seq, seq, seq, half // HEAD_DIM, BF16,
                      col0=(0, half, 2 * half, 3 * half))
    ob = sb_decode(cols, cols, cols, k_cache, v_cache, layer, cols, batch, seq, BF16,
                   col0=(4 * half, 5 * half, 6 * half, 7 * half))
    (y,) = proj([oa, ob], w_out, 0, d, n, 512, (FLAT32,), residual=x)
    return y, s_new, cols[:, 5 * half:6 * half], cols[:, 6 * half:7 * half], w_in


def _odd_prompt(x, norm_w, w_in, w_fl, b_forget, w_out, seq):
    n, d = x.shape
    heads = d // HEAD_DIM
    hn = rmsnorm(x, norm_w, BF16, 256)

    def p(group, *outs):
        return proj([hn], w_in, group * d, d, 1024, 1024, outs)

    (q,), (gate,) = p(0, QUERY16), p(3, FLAT32)
    k_leaf, k = p(1, FLAT32, FLAT16)
    v_leaf, v = p(2, FLAT32, FLAT16)
    (fl,) = proj([hn], w_fl, 0, heads, 1024, heads, (FLAT32,))
    logf, f2 = logf_cumsum(fl, b_forget, 512)
    o = fox_prompt(q, k, v, gate, f2, 1024, 512, 256, BF16)
    (y,) = proj([o], w_out, 0, d, 1024, 1024, (FLAT32,), residual=x)
    return y, k_leaf, v_leaf, logf


def _odd_decode(x, norm_w, w_in, w_fl, b_forget, w_out, k_cache, v_cache, lf_cache, layer, batch, seq):
    n, d = x.shape
    heads = d // HEAD_DIM
    hn = rmsnorm(x, norm_w, BF16, n)
    (cols,) = proj([hn], w_in, 0, 4 * d, n, 512, (FLAT32,))
    (fl,) = proj([hn], w_fl, 0, heads, n, heads, (FLAT32,))
    logf = logf_only(fl, b_forget)
    o = fox_decode(cols, cols, cols, k_cache, v_cache, layer, cols, logf, lf_cache, batch, seq, BF16,
                   col0=(0, d, 2 * d, 3 * d))
    (y,) = proj([o], w_out, 0, d, n, 512, (FLAT32,), residual=x)
    return y, cols[:, d:2 * d], cols[:, 2 * d:3 * d], logf


def kernel(x_prompt, x_sample, state_a_hgrn, cache_b_k, cache_b_v, cache_c_k, cache_c_v, cache_c_logf,
           norm_w, final_norm_w, w_in_even, w_out_even, lb_logits, a_norm_w, w_in_odd, b_forget, w_out_odd):
    bp, tp, d = x_prompt.shape
    bs, ts, _ = x_sample.shape
    assert bp == 1
    depth = norm_w.shape[0]
    n_even = w_in_even.shape[0]
    lb_all = jnp.cumsum(jax.nn.softmax(lb_logits.astype(F32), axis=0), axis=0)[:n_even]

    hp = x_prompt.reshape(bp * tp, d)
    hs = x_sample.reshape(bs * ts, d)
    outs = {name: [] for name in ("sa_p", "sa_s", "bk_p", "bv_p", "bk_s", "bv_s",
                                  "ck_p", "cv_p", "cf_p", "ck_s", "cv_s", "cf_s")}
    for layer in range(depth):
        j = layer // 2
        if layer % 2 == 0:
            a_heads = state_a_hgrn.shape[2]
            b_heads = cache_b_k.shape[3]
            zeros = jnp.zeros((bp, a_heads) + state_a_hgrn.shape[3:], F32)
            w_out = round_weights(w_out_even, j, 512)
            hs, ss, ksm, vsm, w_in = _even_decode(hs, norm_w[layer], w_in_even, j, w_out, lb_all[j], a_norm_w[j],
                                                  state_a_hgrn[j], cache_b_k, cache_b_v, bs, ts)
            hp, sp, kp, vp = _even_prompt(hp, norm_w[layer], w_in, w_out, lb_all[j], a_norm_w[j], zeros, tp)
            outs["sa_p"].append(sp); outs["sa_s"].append(ss)
            outs["bk_p"].append(kp.reshape(bp, tp, b_heads, HEAD_DIM))
            outs["bv_p"].append(vp.reshape(bp, tp, b_heads, HEAD_DIM))
            outs["bk_s"].append(ksm.reshape(bs, ts, b_heads, HEAD_DIM))
            outs["bv_s"].append(vsm.reshape(bs, ts, b_heads, HEAD_DIM))
        else:
            c_heads = cache_c_k.shape[3]
            w_in = w_in_odd[j].astype(BF16)
            w_fl = w_in[:, 4 * c_heads * HEAD_DIM:]
            w_out = round_weights(w_out_odd, j, 512)
            hp, kp, vp, fp = _odd_prompt(hp, norm_w[layer], w_in, w_fl, b_forget[j], w_out, tp)
            hs, ksm, vsm, fsm = _odd_decode(hs, norm_w[layer], w_in, w_fl, b_forget[j], w_out,
                                            cache_c_k, cache_c_v, cache_c_logf, j, bs, ts)
            outs["ck_p"].append(kp.reshape(bp, tp, c_heads, HEAD_DIM))
            outs["cv_p"].append(vp.reshape(bp, tp, c_heads, HEAD_DIM))
            outs["cf_p"].append(fp.reshape(bp, tp, c_heads))
            outs["ck_s"].append(ksm.reshape(bs, ts, c_heads, HEAD_DIM))
            outs["cv_s"].append(vsm.reshape(bs, ts, c_heads, HEAD_DIM))
            outs["cf_s"].append(fsm.reshape(bs, ts, c_heads))
    y_prompt = rmsnorm(hp, final_norm_w, F32, _tile(bp * tp, 256)).reshape(bp, tp, d)
    y_sample = rmsnorm(hs, final_norm_w, F32, _tile(bs * ts, 256)).reshape(bs, ts, d)
    st = {k: jnp.stack(v) for k, v in outs.items()}
    return (y_prompt, y_sample, st["sa_p"], st["sa_s"], st["bk_p"], st["bv_p"], st["bk_s"], st["bv_s"],
            st["ck_p"], st["cv_p"], st["cf_p"], st["ck_s"], st["cv_s"], st["cf_s"])
```

```python
import functools

import numpy as np
import jax
import jax.numpy as jnp
from jax import lax
from jax.experimental import pallas as pl
from jax.experimental.pallas import tpu as pltpu

F32 = jnp.float32
BF16 = jnp.bfloat16

EPS = 1e-6
HEAD_DIM = 128
LANES = 128
LOG2E = 1.4426950408889634
F32_UNDERFLOW_LOG2 = 150.0
HGRN_CHUNK = 64
VMEM_LIMIT_BYTES = 56 * 1024 * 1024

_NT = (((1,), (1,)), ((), ()))
_TN = (((0,), (0,)), ((), ()))


def _params(*sem):
    return pltpu.CompilerParams(dimension_semantics=sem, vmem_limit_bytes=VMEM_LIMIT_BYTES)


def _dot(a, b):
    return jnp.dot(a, b, preferred_element_type=F32)


def _dot_nt(a, b):
    return lax.dot_general(a, b, _NT, preferred_element_type=F32)


def _dot_tn(a, b):
    return lax.dot_general(a, b, _TN, preferred_element_type=F32)


def _split3(x):
    hi = x.astype(BF16)
    r1 = x - hi.astype(F32)
    mid = r1.astype(BF16)
    lo = (r1 - mid.astype(F32)).astype(BF16)
    return hi, mid, lo


def _dot_exact_lhs01(a01, x):
    hi, mid, lo = _split3(x)
    return _dot(a01, hi) + _dot(a01, mid) + _dot(a01, lo)


def _dot_exact_lhs01x3(a01x3, x):
    return _dot(a01x3, jnp.concatenate(_split3(x), axis=0))


def _sigmoid_pair(z):
    e = jnp.exp(-jnp.abs(z))
    r = 1.0 / (1.0 + e)
    er = e * r
    pos = z >= 0
    return jnp.where(pos, r, er), jnp.where(pos, er, r)


def _silu(x):
    return x * _sigmoid_pair(x)[0]


def _log_sigmoid(x):
    return jnp.minimum(x, 0.0) - jnp.log(1.0 + jnp.exp(-jnp.abs(x)))


def _rmsnorm_body(x_ref, w_ref, o_ref):
    x = x_ref[...]
    ms = jnp.mean(x * x, axis=-1, keepdims=True)
    o_ref[...] = (x * lax.rsqrt(ms + EPS) * w_ref[...]).astype(o_ref.dtype)


def rmsnorm(x, w, out_dtype, tm):
    m, d = x.shape
    return pl.pallas_call(
        _rmsnorm_body,
        grid=(m // tm,),
        in_specs=[pl.BlockSpec((tm, d), lambda i: (i, 0)), pl.BlockSpec((1, d), lambda i: (0, 0))],
        out_specs=pl.BlockSpec((tm, d), lambda i: (i, 0)),
        out_shape=jax.ShapeDtypeStruct((m, d), out_dtype),
        compiler_params=_params("parallel"),
        name="rmsnorm",
    )(x, w.reshape(1, d))


def _round_body(w_ref, o_ref):
    o_ref[...] = w_ref[...].astype(o_ref.dtype)


def round_weights(w, layer, rows):
    _, k, n = w.shape
    return pl.pallas_call(
        _round_body,
        grid=(k // rows,),
        in_specs=[pl.BlockSpec((None, rows, n), lambda i: (layer, i, 0))],
        out_specs=pl.BlockSpec((rows, n), lambda i: (i, 0)),
        out_shape=jax.ShapeDtypeStruct((k, n), BF16),
        compiler_params=_params("parallel"),
        name="round_weights",
    )(w)


def _round_proj_body(a_ref, w_ref, y_ref, wb_ref):
    wb = w_ref[...].astype(BF16)
    wb_ref[...] = wb
    y_ref[...] = _dot(a_ref[...], wb)


def round_and_proj(a, w, layer, ncols, tn):
    m, kk = a.shape
    return pl.pallas_call(
        _round_proj_body,
        grid=(ncols // tn,),
        in_specs=[pl.BlockSpec((m, kk), lambda j: (0, 0)), pl.BlockSpec((None, kk, tn), lambda j: (layer, 0, j))],
        out_specs=[pl.BlockSpec((m, tn), lambda j: (0, j)), pl.BlockSpec((kk, tn), lambda j: (0, j))],
        out_shape=[jax.ShapeDtypeStruct((m, ncols), F32), jax.ShapeDtypeStruct((kk, ncols), BF16)],
        compiler_params=_params("parallel"),
        name="round_and_proj",
    )(a, w)


def _proj_body(*refs, n_in, has_residual, outs):
    a_refs = refs[:n_in]
    w_refs = refs[n_in:2 * n_in]
    pos = 2 * n_in
    r_ref = refs[pos] if has_residual else None
    o_refs = refs[pos + int(has_residual):]
    acc = None
    for a_ref, w_ref in zip(a_refs, w_refs):
        d = _dot(a_ref[...], w_ref[...])
        acc = d if acc is None else acc + d
    if has_residual:
        acc = r_ref[...] + acc
    for o_ref, (_, scale) in zip(o_refs, outs):
        val = acc if scale == 1.0 else acc * scale
        o_ref[...] = val.astype(o_ref.dtype)


def proj(a_list, w, col0, ncols, tm, tn, outs, residual=None):
    m = a_list[0].shape[0]
    kk = a_list[0].shape[1]
    assert all(a.shape == (m, kk) for a in a_list) and w.shape[0] == kk * len(a_list)
    assert m % tm == 0 and ncols % tn == 0 and col0 % tn == 0
    cb = col0 // tn
    in_specs = [pl.BlockSpec((tm, kk), lambda i, j: (i, 0)) for _ in a_list]
    in_specs += [pl.BlockSpec((kk, tn), functools.partial(lambda i, j, r: (r, cb + j), r=r)) for r in range(len(a_list))]
    args = list(a_list) + [w] * len(a_list)
    if residual is not None:
        in_specs.append(pl.BlockSpec((tm, tn), lambda i, j: (i, j)))
        args.append(residual)
    out_specs = [pl.BlockSpec((tm, tn), lambda i, j: (i, j)) for _ in outs]
    out_shape = [jax.ShapeDtypeStruct((m, ncols), dtype) for dtype, _ in outs]
    return pl.pallas_call(
        functools.partial(_proj_body, n_in=len(a_list), has_residual=residual is not None, outs=tuple(outs)),
        grid=(m // tm, ncols // tn),
        in_specs=in_specs,
        out_specs=out_specs,
        out_shape=out_shape,
        compiler_params=_params("parallel", "parallel"),
        name="proj",
    )(*args)


def _hgrn_maps(c):
    levels = int(np.log2(c))
    assert 2 ** levels == c
    t = np.arange(c)[:, None]
    s = np.arange(c)[None, :]
    mats = [(s <= t), (s > t)]
    for l in range(levels):
        b = 2 ** l
        start = (t // (2 * b)) * (2 * b)
        upper = (t // b) % 2 == 1
        mats.append((upper & (s >= start + b) & (s <= t)) | ((~upper) & (s > t) & (s <= start + b - 1)))
    return np.concatenate(mats, axis=0).astype(np.float32), levels


def _hgrn_body(q_ref, z_ref, v_ref, g_ref, lb_ref, nw_ref, a_ref, s0_ref, o_ref, sout_ref, st_scr,
               *, c, n_chunks, levels, heads):
    tb = pl.program_id(2)

    @pl.when(tb == 0)
    def _():
        for hh in range(heads):
            st_scr[hh] = s0_ref[0, hh].T

    nw = nw_ref[...]
    amat = a_ref[...]
    row = lax.broadcasted_iota(jnp.int32, (c, c), 0)
    col = lax.broadcasted_iota(jnp.int32, (c, c), 1)
    xor = row ^ col

    units = [(slice(ci * c, (ci + 1) * c), slice(hh * HEAD_DIM, (hh + 1) * HEAD_DIM))
             for hh in range(heads) for ci in range(n_chunks)]
    nu = len(units)
    lbs = [lb_ref[:, cols] for _, cols in units]
    qs = [_silu(q_ref[rows, cols]) for rows, cols in units]
    sigs = [_sigmoid_pair(z_ref[rows, cols]) for rows, cols in units]
    gs = [jnp.log(lbs[u] + (1.0 - lbs[u]) * sigs[u][0]) for u in range(nu)]
    ks = [(1.0 - lbs[u]) * sigs[u][1] for u in range(nu)]
    vs = [v_ref[rows, cols] for rows, cols in units]
    vbs = [v.astype(BF16) for v in vs]
    es = [jnp.exp(_dot_exact_lhs01x3(amat, g)) for g in gs]
    atts = [None] * nu
    for l in range(levels - 1, -1, -1):
        for u in range(nu):
            el = es[u][(2 + l) * c:(3 + l) * c]
            al = _dot_nt((qs[u] * el).astype(BF16), (ks[u] * el).astype(BF16))
            atts[u] = al if atts[u] is None else jnp.where(xor < 2 ** (l + 1), al, atts[u])
    atts = [jnp.where(row > col, att, 0.0).astype(BF16) for att in atts]
    o_intras = [_dot(atts[u], vbs[u]) + jnp.sum(qs[u] * ks[u], axis=-1, keepdims=True) * vs[u] for u in range(nu)]
    upds = [_dot_tn(vbs[u], (ks[u] * es[u][c:2 * c]).astype(BF16)) for u in range(nu)]
    qgs = [(qs[u] * es[u][0:c]).astype(BF16) for u in range(nu)]

    for hh in range(heads):
        st = st_scr[hh]
        for ci in range(n_chunks):
            u = hh * n_chunks + ci
            rows, cols = units[u]
            o = _dot_nt(qgs[u], st.astype(BF16)) + o_intras[u]
            st = st * es[u][c - 1:c] + upds[u]
            ms = jnp.mean(o * o, axis=-1, keepdims=True)
            y = o * lax.rsqrt(ms + EPS) * nw
            o_ref[rows, cols] = (y * _silu(g_ref[rows, cols])).astype(o_ref.dtype)
        st_scr[hh] = st

    @pl.when(tb == pl.num_programs(2) - 1)
    def _():
        for hh in range(heads):
            sout_ref[0, hh] = st_scr[hh].T


def hgrn2(qa, fa, ia, ga, lb, a_norm_w, s0, batch, seq, rows_per_step, c, heads_per_step, out_dtype,
          col0=(0, 0, 0, 0)):
    n = qa.shape[0]
    width = lb.shape[0]
    h = width // HEAD_DIM
    assert n == batch * seq and seq % rows_per_step == 0 and rows_per_step % c == 0 and h % heads_per_step == 0
    nb = seq // rows_per_step
    gw = heads_per_step * HEAD_DIM
    amat_np, levels = _hgrn_maps(c)
    amat = jnp.asarray(np.concatenate([amat_np] * 3, axis=1), BF16)
    row_spec = pl.BlockSpec((rows_per_step, gw), lambda b, hh, t: (b * nb + t, hh))
    in_row_specs = [pl.BlockSpec((rows_per_step, gw), functools.partial(lambda b, hh, t, off: (b * nb + t, off + hh), off=c0 // gw))
                    for c0 in col0]
    assert all(c0 % gw == 0 for c0 in col0)
    state_spec = pl.BlockSpec((1, heads_per_step, HEAD_DIM, HEAD_DIM), lambda b, hh, t: (b, hh, 0, 0))
    body = functools.partial(_hgrn_body, c=c, n_chunks=rows_per_step // c, levels=levels, heads=heads_per_step)
    return pl.pallas_call(
        body,
        grid=(batch, h // heads_per_step, nb),
        in_specs=in_row_specs + [
                  pl.BlockSpec((1, gw), lambda b, hh, t: (0, hh)),
                  pl.BlockSpec((1, HEAD_DIM), lambda b, hh, t: (0, 0)),
                  pl.BlockSpec(amat.shape, lambda b, hh, t: (0, 0)),
                  state_spec],
        out_specs=[row_spec, state_spec],
        out_shape=[jax.ShapeDtypeStruct((n, width), out_dtype),
                   jax.ShapeDtypeStruct((batch, h, HEAD_DIM, HEAD_DIM), F32)],
        scratch_shapes=[pltpu.VMEM((heads_per_step, HEAD_DIM, HEAD_DIM), F32)],
        compiler_params=_params("parallel", "parallel", "arbitrary"),
        name="hgrn2",
    )(qa, fa, ia, ga, lb.reshape(1, width), a_norm_w.reshape(1, HEAD_DIM), amat, s0)


def _neg_abs(x):
    return -jnp.abs(x)


def _sb_prompt_body(q_ref, k_ref, v_ref, g_ref, uu_ref, o_ref, run_scr, acc_scr, w_scr, *, tq, tk, rc):
    i = pl.program_id(1)
    nd = tq // tk
    nl = tk // LANES
    q0 = pl.multiple_of(i * tq, tq)
    run_scr[...] = jnp.zeros(run_scr.shape, F32)
    acc_scr[...] = jnp.zeros(acc_scr.shape, F32)

    def apply_pending(slot, k_prev, first_row):
        vt = v_ref[pl.ds(pl.multiple_of(k_prev, tk), tk), :]
        for r0 in range(first_row, tq, rc):
            rows = slice(r0, min(r0 + rc, tq))
            acc_scr[rows, :] = acc_scr[rows, :] + _dot(w_scr[slot, rows, :], vt)

    def score(slot, k0, first_row, masked):
        kt = k_ref[pl.ds(pl.multiple_of(k0, tk), tk), :]
        for r0 in range(first_row, tq, rc):
            r1 = min(r0 + rc, tq)
            rows = slice(r0, r1)
            z = _dot_nt(q_ref[rows, :], kt)
            if masked:
                qpos = lax.broadcasted_iota(jnp.int32, (r1 - r0, LANES), 0) + (q0 + r0)
                kpos = lax.broadcasted_iota(jnp.int32, (r1 - r0, LANES), 1) + k0
            zs, sps, his, los, valids = [], [], [], [], []
            rowsum = None
            for c in range(nl):
                zc = z[:, c * LANES:(c + 1) * LANES]
                sp = jnp.maximum(zc, 0.0) + jnp.log2(1.0 + jnp.exp2(_neg_abs(zc)))
                if masked:
                    valid = (kpos + c * LANES) < qpos
                    sp = jnp.where(valid, sp, 0.0)
                    valids.append(valid)
                hi = sp.astype(BF16)
                lo = (sp - hi.astype(F32)).astype(BF16)
                zs.append(zc); sps.append(sp); his.append(hi); los.append(lo)
                rowsum = sp if rowsum is None else rowsum + sp
            tail = _dot(jnp.concatenate(his + los, axis=1), uu_ref[...])
            run = run_scr[rows, :]
            for c in range(nl):
                w = jnp.exp2(zs[c] - (sps[c] + tail[:, c * LANES:(c + 1) * LANES] + run))
                if masked:
                    w = jnp.where(valids[c], w, 0.0)
                w_scr[slot, rows, c * LANES:(c + 1) * LANES] = w.astype(BF16)
            run_scr[rows, :] = run + jnp.sum(rowsum, axis=-1, keepdims=True)

    assert nd % 2 == 0
    for d in range(nd - 1, 0, -2):
        score(0, q0 + d * tk, d * tk, True)
        if d + 1 < nd:
            apply_pending(1, q0 + (d + 1) * tk, (d + 1) * tk)
        score(1, q0 + (d - 1) * tk, (d - 1) * tk, True)
        apply_pending(0, q0 + d * tk, d * tk)

    def before(carry):
        jj, kp, _ = carry
        for u in range(unroll):
            k_a = q0 - (2 * (unroll * jj + u) + 1) * tk
            score(0, k_a, 0, False)
            apply_pending(1, kp, 0)
            score(1, k_a - tk, 0, False)
            apply_pending(0, k_a, 0)
            kp = k_a - tk
        return jj + 1, kp, jnp.min(run_scr[...]) < F32_UNDERFLOW_LOG2

    unroll = 1
    assert nd % (2 * unroll) == 0
    n_trips = (i * nd) // (2 * unroll)
    _, k_pending, _ = lax.while_loop(lambda carry: (carry[0] < n_trips) & carry[2], before,
                                     (jnp.int32(0), q0, jnp.bool_(True)))
    apply_pending(1, k_pending, 0)
    o_ref[...] = (acc_scr[...] * _silu(g_ref[...])).astype(o_ref.dtype)


def _tail_matrix(n):
    sp = np.arange(n)[:, None]
    s = np.arange(n)[None, :]
    return (sp > s).astype(np.float32)


def sb_prompt(q, k, v, gate, tq, tk, rc, out_dtype):
    t, width = q.shape
    h = width // HEAD_DIM
    u = _tail_matrix(tk)
    uu = jnp.asarray(np.concatenate([u, u], axis=0), BF16)
    q_spec = pl.BlockSpec((tq, HEAD_DIM), lambda hh, i: (i, hh))
    kv_spec = pl.BlockSpec((t, HEAD_DIM), lambda hh, i: (0, hh))
    return pl.pallas_call(
        functools.partial(_sb_prompt_body, tq=tq, tk=tk, rc=rc),
        grid=(h, t // tq),
        in_specs=[q_spec, kv_spec, kv_spec, q_spec, pl.BlockSpec(uu.shape, lambda hh, i: (0, 0))],
        out_specs=q_spec,
        out_shape=jax.ShapeDtypeStruct((t, width), out_dtype),
        scratch_shapes=[pltpu.VMEM((tq, LANES), F32), pltpu.VMEM((tq, HEAD_DIM), F32),
                        pltpu.VMEM((2, tq, tk), BF16)],
        compiler_params=_params("parallel", "arbitrary"),
        name="sb_prompt",
    )(q, k, v, gate, uu)


def _logf_cumsum_body(fl_ref, b_ref, tri_ref, lf_ref, f2_ref, carry_scr):
    @pl.when(pl.program_id(0) == 0)
    def _():
        carry_scr[...] = jnp.zeros_like(carry_scr)

    lf = _log_sigmoid(fl_ref[...] + b_ref[...])
    lf_ref[...] = lf
    f = carry_scr[...] + _dot_exact_lhs01(tri_ref[...], lf)
    f2_ref[...] = f * LOG2E
    carry_scr[...] = f[f.shape[0] - 1:, :]


def logf_cumsum(fl, b_forget, blk):
    t, h = fl.shape
    tri = jnp.asarray(np.tril(np.ones((blk, blk), np.float32)), BF16)
    spec = pl.BlockSpec((blk, h), lambda i: (i, 0))
    return pl.pallas_call(
        _logf_cumsum_body,
        grid=(t // blk,),
        in_specs=[spec, pl.BlockSpec((1, h), lambda i: (0, 0)), pl.BlockSpec((blk, blk), lambda i: (0, 0))],
        out_specs=[spec, spec],
        out_shape=[jax.ShapeDtypeStruct((t, h), F32), jax.ShapeDtypeStruct((t, h), F32)],
        scratch_shapes=[pltpu.VMEM((1, h), F32)],
        compiler_params=_params("arbitrary"),
        name="logf_cumsum",
    )(fl, b_forget.reshape(1, h), tri)


def _logf_body(fl_ref, b_ref, lf_ref):
    lf_ref[...] = _log_sigmoid(fl_ref[...] + b_ref[...])


def logf_only(fl, b_forget):
    t, h = fl.shape
    return pl.pallas_call(
        _logf_body,
        grid=(1,),
        in_specs=[pl.BlockSpec((t, h), lambda i: (0, 0)), pl.BlockSpec((1, h), lambda i: (0, 0))],
        out_specs=pl.BlockSpec((t, h), lambda i: (0, 0)),
        out_shape=jax.ShapeDtypeStruct((t, h), F32),
        name="logf",
    )(fl, b_forget.reshape(1, h))


_BIAS_PIECES = 3


def _bias_selectors(h):
    sel = np.zeros((2, h, _BIAS_PIECES, h, LANES), np.float32)
    for hh in range(h):
        for p in range(_BIAS_PIECES):
            sel[0, hh, p, hh, p] = 1.0
            sel[1, hh, p, hh, _BIAS_PIECES + p] = -1.0
    return sel


def _bias_columns(f, sel_ref, query_side):
    out = None
    for p, piece in enumerate(_split3(f)):
        d = _dot(piece, sel_ref[0, p])
        out = d if out is None else out + d
    lane = lax.broadcasted_iota(jnp.int32, out.shape, 1)
    ones_at = (lane >= _BIAS_PIECES) & (lane < 2 * _BIAS_PIECES) if query_side else lane < _BIAS_PIECES
    return jnp.where(ones_at, 1.0, out).astype(BF16)


def _fox_prompt_body(q_ref, k_ref, v_ref, g_ref, f_ref, selq_ref, selk_ref, o_ref,
                     kx_scr, vx_scr, m_scr, acc_scr, p_scr, alpha_scr, kmax_scr, ub_scr, *, tq, tk, rc):
    i = pl.program_id(1)
    nd = tq // tk
    nl = tk // LANES
    t_all = k_ref.shape[0]

    head = pl.program_id(0)

    def own_column(f_rows):
        lane = lax.broadcasted_iota(jnp.int32, f_rows.shape, 1)
        return jnp.sum(jnp.where(lane == head, f_rows, 0.0), axis=-1, keepdims=True)

    @pl.when(i == 0)
    def _():
        kx_scr[:, 0:HEAD_DIM] = k_ref[...]
        kx_scr[:, HEAD_DIM:] = _bias_columns(f_ref[...], selk_ref, False)
        vx_scr[:, 0:HEAD_DIM] = v_ref[...]
        vx_scr[:, HEAD_DIM:] = jnp.ones((t_all, LANES), BF16)
        kf = k_ref[...].astype(F32)
        knorm2 = jnp.max(jnp.sum(kf * kf, axis=-1, keepdims=True), axis=0, keepdims=True)
        kmax_scr[...] = jnp.broadcast_to(jnp.sqrt(knorm2), kmax_scr.shape)

    q0 = pl.multiple_of(i * tq, tq)
    qx = jnp.concatenate([q_ref[...], _bias_columns(f_ref[pl.ds(q0, tq), :], selq_ref, True)], axis=1)
    m_scr[...] = jnp.full(m_scr.shape, -jnp.inf, F32)
    acc_scr[...] = jnp.zeros(acc_scr.shape, F32)
    qf = q_ref[...].astype(F32)
    qnorm = jnp.sqrt(jnp.sum(qf * qf, axis=-1, keepdims=True))
    ub_scr[...] = qnorm * kmax_scr[...] * (1.0 + 2.0 ** -10) + own_column(f_ref[pl.ds(q0, tq), :])

    def apply_pending(slot, k_prev, first_row):
        vxt = vx_scr[pl.ds(pl.multiple_of(k_prev, tk), tk), :]
        for r0 in range(first_row, tq, rc):
            rows = slice(r0, r0 + rc)
            pv = _dot(p_scr[slot, rows, :], vxt)
            alpha = alpha_scr[slot, rows, :]
            acc_scr[rows, 0:HEAD_DIM] = alpha * acc_scr[rows, 0:HEAD_DIM] + pv[:, 0:HEAD_DIM]
            acc_scr[rows, HEAD_DIM:] = alpha * acc_scr[rows, HEAD_DIM:] + pv[:, HEAD_DIM:]

    def score(slot, k0, first_row, masked):
        kxt = kx_scr[pl.ds(pl.multiple_of(k0, tk), tk), :]
        for r0 in range(first_row, tq, rc):
            rows = slice(r0, r0 + rc)
            s = _dot_nt(qx[r0:r0 + rc, :], kxt)
            sb = [s[:, c * LANES:(c + 1) * LANES] for c in range(nl)]
            if masked:
                qpos = lax.broadcasted_iota(jnp.int32, (rc, LANES), 0) + (q0 + r0)
                kpos = lax.broadcasted_iota(jnp.int32, (rc, LANES), 1) + k0
                sb = [jnp.where((kpos + c * LANES) <= qpos, sb[c], -jnp.inf) for c in range(nl)]
            mx = sb[0]
            for c in range(1, nl):
                mx = jnp.maximum(mx, sb[c])
            m_old = m_scr[rows, :]
            m_new = jnp.maximum(m_old, jnp.max(mx, axis=-1, keepdims=True))
            alpha_scr[slot, rows, :] = jnp.exp2(m_old - m_new)
            for c in range(nl):
                p_scr[slot, rows, c * LANES:(c + 1) * LANES] = jnp.exp2(sb[c] - m_new).astype(BF16)
            m_scr[rows, :] = m_new

    assert nd % 2 == 0 and tk % rc == 0
    for d in range(0, nd, 2):
        score(0, q0 + d * tk, d * tk, True)
        if d > 0:
            apply_pending(1, q0 + (d - 1) * tk, (d - 1) * tk)
        score(1, q0 + (d + 1) * tk, (d + 1) * tk, True)
        apply_pending(0, q0 + d * tk, d * tk)

    p_scr[1, 0:(nd - 1) * tk, :] = jnp.zeros(((nd - 1) * tk, tk), BF16)
    alpha_scr[1, 0:(nd - 1) * tk, :] = jnp.ones(((nd - 1) * tk, LANES), F32)

    def pair(k_a, kp):
        score(0, k_a, 0, False)
        apply_pending(1, kp, 0)
        score(1, k_a - tk, 0, False)
        apply_pending(0, k_a, 0)
        return k_a - tk

    def visible(k_hi):
        f_hi = own_column(f_ref[pl.ds(jnp.maximum(k_hi, 0), 1), :])
        return jnp.max(ub_scr[...] - f_hi - m_scr[...]) > -(F32_UNDERFLOW_LOG2 + 2.0)

    def trip(carry):
        jj, kp, _ = carry
        kp = pair(q0 - (2 * jj + 1) * tk, kp)
        return jj + 1, kp, visible(kp - 1)

    n_pairs = (i * nd) // 2
    _, k_pending, _ = lax.while_loop(lambda carry: (carry[0] < n_pairs) & carry[2], trip,
                                     (jnp.int32(0), q0 + (nd - 1) * tk, visible(q0 - 1)))
    apply_pending(1, k_pending, 0)
    o_ref[...] = (acc_scr[:, 0:HEAD_DIM] / acc_scr[:, HEAD_DIM:] * _silu(g_ref[...])).astype(o_ref.dtype)


def fox_prompt(q, k, v, gate, f2, tq, tk, rc, out_dtype):
    t, width = q.shape
    h = width // HEAD_DIM
    sel = jnp.asarray(_bias_selectors(h), BF16)
    q_spec = pl.BlockSpec((tq, HEAD_DIM), lambda hh, i: (i, hh))
    kv_spec = pl.BlockSpec((t, HEAD_DIM), lambda hh, i: (0, hh))
    sel_spec = pl.BlockSpec((1, _BIAS_PIECES, h, LANES), lambda hh, i: (hh, 0, 0, 0))
    return pl.pallas_call(
        functools.partial(_fox_prompt_body, tq=tq, tk=tk, rc=rc),
        grid=(h, t // tq),
        in_specs=[q_spec, kv_spec, kv_spec, q_spec, pl.BlockSpec((t, h), lambda hh, i: (0, 0)),
                  sel_spec, sel_spec],
        out_specs=q_spec,
        out_shape=jax.ShapeDtypeStruct((t, width), out_dtype),
        scratch_shapes=[pltpu.VMEM((t, 2 * HEAD_DIM), BF16), pltpu.VMEM((t, 2 * HEAD_DIM), BF16),
                        pltpu.VMEM((tq, LANES), F32), pltpu.VMEM((tq, 2 * HEAD_DIM), F32),
                        pltpu.VMEM((2, tq, tk), BF16), pltpu.VMEM((2, tq, LANES), F32),
                        pltpu.VMEM((1, LANES), F32), pltpu.VMEM((tq, LANES), F32)],
        compiler_params=_params("parallel", "arbitrary"),
        name="fox_prompt",
    )(q, k, v, gate, f2, sel[0], sel[1])


HEADS_PER_GROUP = 8


def _expand_queries(q, tq):
    gw = q.shape[1]
    rep = jnp.concatenate([q] * HEADS_PER_GROUP, axis=0)
    r = lax.broadcasted_iota(jnp.int32, (HEADS_PER_GROUP * tq, gw), 0) // tq
    cidx = lax.broadcasted_iota(jnp.int32, (HEADS_PER_GROUP * tq, gw), 1) // HEAD_DIM
    return jnp.where(r == cidx, rep, 0.0)


def _collect_heads(full, tq):
    return jnp.concatenate(
        [full[hh * tq:(hh + 1) * tq, hh * HEAD_DIM:(hh + 1) * HEAD_DIM] for hh in range(HEADS_PER_GROUP)],
        axis=1)


def _gather_keys(cache_ref, new_ref, scr, past, tk):
    g = HEADS_PER_GROUP
    x = cache_ref[0].reshape(past // g, g, g, HEAD_DIM)
    x = jnp.swapaxes(x, 1, 2)
    for hh in range(g):
        scr[0:past, hh * HEAD_DIM:(hh + 1) * HEAD_DIM] = x[:, hh].reshape(past, HEAD_DIM).astype(BF16)
    scr[past:tk, :] = new_ref[...].astype(BF16)


def _row_blocks(n):
    return [(r0, min(r0 + LANES, n)) for r0 in range(0, n, LANES)]


def _cumsum_rows(pieces, tri_ref):
    n = pieces[0].shape[0]
    out, carry = [], None
    for r0, r1 in _row_blocks(n):
        tri = tri_ref[0:r1 - r0, 0:r1 - r0]
        local = None
        for piece in pieces:
            d = _dot(tri, piece[r0:r1])
            local = d if local is None else local + d
        if carry is not None:
            local = local + carry
        carry = local[r1 - r0 - 1:r1 - r0]
        out.append(local)
    return jnp.concatenate(out, axis=0)


def _tailsum_rows(pieces, tri_ref):
    n = pieces[0].shape[0]
    out, carry = [], None
    for r0, r1 in reversed(_row_blocks(n)):
        tri = tri_ref[0:r1 - r0, 0:r1 - r0]
        local, total = None, None
        for piece in pieces:
            blk = piece[r0:r1]
            d = _dot_tn(tri, blk)
            local = d if local is None else local + d
            t = blk.astype(F32)
            total = t if total is None else total + t
        local = local - total
        if carry is not None:
            local = local + carry
        carry = local[0:1] + total[0:1]
        out.append(local)
    return jnp.concatenate(out[::-1], axis=0)


def _sb_decode_body(q_ref, kn_ref, vn_ref, kc_ref, vc_ref, g_ref, tri_ref, o_ref, k_scr, v_scr, *, past, tq):
    tk = past + tq
    _gather_keys(kc_ref, kn_ref, k_scr, past, tk)
    _gather_keys(vc_ref, vn_ref, v_scr, past, tk)

    lanes = HEADS_PER_GROUP * tq
    qx = _expand_queries(q_ref[...] * (HEAD_DIM ** -0.5), tq).astype(BF16)
    z = _dot_nt(k_scr[...], qx)
    kpos = lax.broadcasted_iota(jnp.int32, (tk, lanes), 0)
    qpos = past + lax.broadcasted_iota(jnp.int32, (tk, lanes), 1) % tq
    valid = kpos < qpos
    ls_neg = -(jnp.maximum(z, 0.0) + jnp.log(1.0 + jnp.exp(-jnp.abs(z))))
    lm = jnp.where(valid, ls_neg, 0.0)
    hi = lm.astype(BF16)
    lo = (lm - hi.astype(F32)).astype(BF16)
    tail = _tailsum_rows((hi, lo), tri_ref)
    w = jnp.where(valid, jnp.exp(z + ls_neg + tail), 0.0)
    full = _dot_tn(w.astype(BF16), v_scr[...])
    o_ref[...] = (_collect_heads(full, tq) * _silu(g_ref[...])).astype(o_ref.dtype)


def _cache_spec(layer, past):
    return pl.BlockSpec((None, 1, past, HEADS_PER_GROUP, HEAD_DIM), lambda b, g: (layer, b, 0, g, 0))


def _decode_row_specs(tq, gw, col0):
    assert all(c0 % gw == 0 for c0 in col0)
    return [pl.BlockSpec((tq, gw), functools.partial(lambda b, g, off: (b, off + g), off=c0 // gw)) for c0 in col0]


def sb_decode(q, k_new, v_new, k_cache, v_cache, layer, gate, batch, tq, out_dtype, col0=(0, 0, 0, 0)):
    n = q.shape[0]
    width = k_cache.shape[3] * HEAD_DIM
    past = k_cache.shape[2]
    tk = past + tq
    gw = HEADS_PER_GROUP * HEAD_DIM
    groups = width // gw
    tri = jnp.asarray(np.tril(np.ones((LANES, LANES), np.float32)), BF16)
    row_spec = pl.BlockSpec((tq, gw), lambda b, g: (b, g))
    rs = _decode_row_specs(tq, gw, col0)
    cache_spec = _cache_spec(layer, past)
    return pl.pallas_call(
        functools.partial(_sb_decode_body, past=past, tq=tq),
        grid=(batch, groups),
        in_specs=[rs[0], rs[1], rs[2], cache_spec, cache_spec, rs[3],
                  pl.BlockSpec((LANES, LANES), lambda b, g: (0, 0))],
        out_specs=row_spec,
        out_shape=jax.ShapeDtypeStruct((n, width), out_dtype),
        scratch_shapes=[pltpu.VMEM((tk, gw), BF16), pltpu.VMEM((tk, gw), BF16)],
        compiler_params=_params("parallel", "parallel"),
        name="sb_decode",
    )(q, k_new, v_new, k_cache, v_cache, gate, tri)


def _fox_decode_body(q_ref, kn_ref, vn_ref, kc_ref, vc_ref, g_ref, lfn_ref, lfc_ref, ex_ref, tri_ref, o_ref,
                     k_scr, v_scr, lf_scr, *, past, tq):
    tk = past + tq
    _gather_keys(kc_ref, kn_ref, k_scr, past, tk)
    _gather_keys(vc_ref, vn_ref, v_scr, past, tk)
    lf_scr[0:past, :] = lfc_ref[0]
    lf_scr[past:tk, :] = lfn_ref[...]

    lanes = HEADS_PER_GROUP * tq
    hi, mid, lo = _split3(lf_scr[...])
    ex = ex_ref[0]
    lfx = _dot(hi, ex) + _dot(mid, ex) + _dot(lo, ex)
    f_key = _cumsum_rows(_split3(lfx), tri_ref)
    kpos = lax.broadcasted_iota(jnp.int32, (tk, lanes), 0)
    qpos = past + lax.broadcasted_iota(jnp.int32, (tk, lanes), 1) % tq
    f_query = jnp.sum(jnp.where(kpos == qpos, f_key, 0.0), axis=0, keepdims=True)

    qx = _expand_queries(q_ref[...] * (HEAD_DIM ** -0.5), tq).astype(BF16)
    s = _dot_nt(k_scr[...], qx) + (f_query - f_key)
    s = jnp.where(kpos <= qpos, s, -jnp.inf)
    p = jnp.exp(s - jnp.max(s, axis=0, keepdims=True))
    p = p / jnp.sum(p, axis=0, keepdims=True)
    full = _dot_tn(p.astype(BF16), v_scr[...])
    o_ref[...] = (_collect_heads(full, tq) * _silu(g_ref[...])).astype(o_ref.dtype)


def fox_decode(q, k_new, v_new, k_cache, v_cache, layer, gate, lf_new, lf_cache, batch, tq, out_dtype,
               col0=(0, 0, 0, 0)):
    n = q.shape[0]
    h = k_cache.shape[3]
    width = h * HEAD_DIM
    past = k_cache.shape[2]
    tk = past + tq
    gw = HEADS_PER_GROUP * HEAD_DIM
    groups = width // gw
    lanes = HEADS_PER_GROUP * tq
    tri = jnp.asarray(np.tril(np.ones((LANES, LANES), np.float32)), BF16)
    head_of_lane = np.arange(lanes)[None, None, :] // tq + HEADS_PER_GROUP * np.arange(groups)[:, None, None]
    expand = jnp.asarray((np.arange(h)[None, :, None] == head_of_lane).astype(np.float32), BF16)
    row_spec = pl.BlockSpec((tq, gw), lambda b, g: (b, g))
    rs = _decode_row_specs(tq, gw, col0)
    cache_spec = _cache_spec(layer, past)
    return pl.pallas_call(
        functools.partial(_fox_decode_body, past=past, tq=tq),
        grid=(batch, groups),
        in_specs=[rs[0], rs[1], rs[2], cache_spec, cache_spec, rs[3],
                  pl.BlockSpec((tq, h), lambda b, g: (b, 0)),
                  pl.BlockSpec((None, 1, past, h), lambda b, g: (layer, b, 0, 0)),
                  pl.BlockSpec((1, h, lanes), lambda b, g: (g, 0, 0)),
                  pl.BlockSpec((LANES, LANES), lambda b, g: (0, 0))],
        out_specs=row_spec,
        out_shape=jax.ShapeDtypeStruct((n, width), out_dtype),
        scratch_shapes=[pltpu.VMEM((tk, gw), BF16), pltpu.VMEM((tk, gw), BF16), pltpu.VMEM((tk, h), F32)],
        compiler_params=_params("parallel", "parallel"),
        name="fox_decode",
    )(q, k_new, v_new, k_cache, v_cache, gate, lf_new, lf_cache, expand, tri)


QK_SCALE_LOG2 = LOG2E * HEAD_DIM ** -0.5
FLAT32 = (F32, 1.0)
FLAT16 = (BF16, 1.0)
QUERY16 = (BF16, QK_SCALE_LOG2)


def _tile(m, pref):
    return pref if m % pref == 0 else m


def _even_prompt(x, norm_w, w_in, w_out, lb, a_norm_w, s0, seq):
    n, d = x.shape
    half = d // 2
    hn = rmsnorm(x, norm_w, BF16, 256)

    def p(group, *outs):
        return proj([hn], w_in, group * half, half, 1024, 1024, outs)

    (qa,), (fa,), (ia,), (ga,), (gb,) = p(0, FLAT32), p(1, FLAT32), p(2, FLAT32), p(3, FLAT32), p(7, FLAT32)
    (qb,) = p(4, QUERY16)
    kb_leaf, kb = p(5, FLAT32, FLAT16)
    vb_leaf, vb = p(6, FLAT32, FLAT16)
    oa, s_new = hgrn2(qa, fa, ia, ga, lb, a_norm_w, s0, 1, seq, 1024, HGRN_CHUNK, 1, BF16)
    ob = sb_prompt(qb, kb, vb, gb, 1024, 256, 1024, BF16)
    (y,) = proj([oa, ob], w_out, 0, d, 1024, 1024, (FLAT32,), residual=x)
    return y, s_new, kb_leaf, vb_leaf


def _even_decode(x, norm_w, w_in_f32, layer, w_out, lb, a_norm_w, s0, k_cache, v_cache, batch, seq):
    n, d = x.shape
    half = d // 2
    hn = rmsnorm(x, norm_w, BF16, n)
    cols, w_in = round_and_proj(hn, w_in_f32, layer, 8 * half, 512)
    oa, s_new = hgrn2(cols, cols, cols, cols, lb, a_norm_w, s0, batch, seq, seq, seq, half // HEAD_DIM, BF16,
                      col0=(0, half, 2 * half, 3 * half))
    ob = sb_decode(cols, cols, cols, k_cache, v_cache, layer, cols, batch, seq, BF16,
                   col0=(4 * half, 5 * half, 6 * half, 7 * half))
    (y,) = proj([oa, ob], w_out, 0, d, n, 512, (FLAT32,), residual=x)
    return y, s_new, cols[:, 5 * half:6 * half], cols[:, 6 * half:7 * half], w_in


def _odd_prompt(x, norm_w, w_in, w_fl, b_forget, w_out, seq):
    n, d = x.shape
    heads = d // HEAD_DIM
    hn = rmsnorm(x, norm_w, BF16, 256)

    def p(group, *outs):
        return proj([hn], w_in, group * d, d, 1024, 1024, outs)

    (q,), (gate,) = p(0, QUERY16), p(3, FLAT32)
    k_leaf, k = p(1, FLAT32, FLAT16)
    v_leaf, v = p(2, FLAT32, FLAT16)
    (fl,) = proj([hn], w_fl, 0, heads, 1024, heads, (FLAT32,))
    logf, f2 = logf_cumsum(fl, b_forget, 512)
    o = fox_prompt(q, k, v, gate, f2, 1024, 512, 256, BF16)
    (y,) = proj([o], w_out, 0, d, 1024, 1024, (FLAT32,), residual=x)
    return y, k_leaf, v_leaf, logf


def _odd_decode(x, norm_w, w_in, w_fl, b_forget, w_out, k_cache, v_cache, lf_cache, layer, batch, seq):
    n, d = x.shape
    heads = d // HEAD_DIM
    hn = rmsnorm(x, norm_w, BF16, n)
    (cols,) = proj([hn], w_in, 0, 4 * d, n, 512, (FLAT32,))
    (fl,) = proj([hn], w_fl, 0, heads, n, heads, (FLAT32,))
    logf = logf_only(fl, b_forget)
    o = fox_decode(cols, cols, cols, k_cache, v_cache, layer, cols, logf, lf_cache, batch, seq, BF16,
                   col0=(0, d, 2 * d, 3 * d))
    (y,) = proj([o], w_out, 0, d, n, 512, (FLAT32,), residual=x)
    return y, cols[:, d:2 * d], cols[:, 2 * d:3 * d], logf


def kernel(x_prompt, x_sample, state_a_hgrn, cache_b_k, cache_b_v, cache_c_k, cache_c_v, cache_c_logf,
           norm_w, final_norm_w, w_in_even, w_out_even, lb_logits, a_norm_w, w_in_odd, b_forget, w_out_odd):
    bp, tp, d = x_prompt.shape
    bs, ts, _ = x_sample.shape
    assert bp == 1
    depth = norm_w.shape[0]
    n_even = w_in_even.shape[0]
    lb_all = jnp.cumsum(jax.nn.softmax(lb_logits.astype(F32), axis=0), axis=0)[:n_even]

    hp = x_prompt.reshape(bp * tp, d)
    hs = x_sample.reshape(bs * ts, d)
    outs = {name: [] for name in ("sa_p", "sa_s", "bk_p", "bv_p", "bk_s", "bv_s",
                                  "ck_p", "cv_p", "cf_p", "ck_s", "cv_s", "cf_s")}
    for layer in range(depth):
        j = layer // 2
        if layer % 2 == 0:
            a_heads = state_a_hgrn.shape[2]
            b_heads = cache_b_k.shape[3]
            zeros = jnp.zeros((bp, a_heads) + state_a_hgrn.shape[3:], F32)
            w_out = round_weights(w_out_even, j, 512)
            hs, ss, ksm, vsm, w_in = _even_decode(hs, norm_w[layer], w_in_even, j, w_out, lb_all[j], a_norm_w[j],
                                                  state_a_hgrn[j], cache_b_k, cache_b_v, bs, ts)
            hp, sp, kp, vp = _even_prompt(hp, norm_w[layer], w_in, w_out, lb_all[j], a_norm_w[j], zeros, tp)
            outs["sa_p"].append(sp); outs["sa_s"].append(ss)
            outs["bk_p"].append(kp.reshape(bp, tp, b_heads, HEAD_DIM))
            outs["bv_p"].append(vp.reshape(bp, tp, b_heads, HEAD_DIM))
            outs["bk_s"].append(ksm.reshape(bs, ts, b_heads, HEAD_DIM))
            outs["bv_s"].append(vsm.reshape(bs, ts, b_heads, HEAD_DIM))
        else:
            c_heads = cache_c_k.shape[3]
            w_in = w_in_odd[j].astype(BF16)
            w_fl = w_in[:, 4 * c_heads * HEAD_DIM:]
            w_out = round_weights(w_out_odd, j, 512)
            hp, kp, vp, fp = _odd_prompt(hp, norm_w[layer], w_in, w_fl, b_forget[j], w_out, tp)
            hs, ksm, vsm, fsm = _odd_decode(hs, norm_w[layer], w_in, w_fl, b_forget[j], w_out,
                                            cache_c_k, cache_c_v, cache_c_logf, j, bs, ts)
            outs["ck_p"].append(kp.reshape(bp, tp, c_heads, HEAD_DIM))
            outs["cv_p"].append(vp.reshape(bp, tp, c_heads, HEAD_DIM))
            outs["cf_p"].append(fp.reshape(bp, tp, c_heads))
            outs["ck_s"].append(ksm.reshape(bs, ts, c_heads, HEAD_DIM))
            outs["cv_s"].append(vsm.reshape(bs, ts, c_heads, HEAD_DIM))
            outs["cf_s"].append(fsm.reshape(bs, ts, c_heads))
    y_prompt = rmsnorm(hp, final_norm_w, F32, _tile(bp * tp, 256)).reshape(bp, tp, d)
    y_sample = rmsnorm(hs, final_norm_w, F32, _tile(bs * ts, 256)).reshape(bs, ts, d)
    st = {k: jnp.stack(v) for k, v in outs.items()}
    return (y_prompt, y_sample, st["sa_p"], st["sa_s"], st["bk_p"], st["bv_p"], st["bk_s"], st["bv_s"],
            st["ck_p"], st["cv_p"], st["cf_p"], st["ck_s"], st["cv_s"], st["cf_s"])
```

```python
import functools

import numpy as np
import jax
import jax.numpy as jnp
from jax import lax
from jax.experimental import pallas as pl
from jax.experimental.pallas import tpu as pltpu

F32 = jnp.float32
BF16 = jnp.bfloat16

EPS = 1e-6
HEAD_DIM = 128
LANES = 128
LOG2E = 1.4426950408889634
F32_UNDERFLOW_LOG2 = 150.0
HGRN_CHUNK = 64
VMEM_LIMIT_BYTES = 56 * 1024 * 1024

PROJ_TILE = 1024
DECODE_PROJ_COLS = 512
NORM_ROWS = 256
ROUND_ROWS = 512
ATTN_ROWS = 1024
SB_KEYS = 256
FOX_KEYS = 512
FOX_ROW_CHUNK = 256
HGRN_ROWS = 1024
CUMSUM_ROWS = 512

_NT = (((1,), (1,)), ((), ()))
_TN = (((0,), (0,)), ((), ()))


def _params(*sem):
    return pltpu.CompilerParams(dimension_semantics=sem, vmem_limit_bytes=VMEM_LIMIT_BYTES)


def _dot(a, b):
    return jnp.dot(a, b, preferred_element_type=F32)


def _dot_nt(a, b):
    return lax.dot_general(a, b, _NT, preferred_element_type=F32)


def _dot_tn(a, b):
    return lax.dot_general(a, b, _TN, preferred_element_type=F32)


def _split3(x):
    hi = x.astype(BF16)
    r1 = x - hi.astype(F32)
    mid = r1.astype(BF16)
    lo = (r1 - mid.astype(F32)).astype(BF16)
    return hi, mid, lo


def _dot_exact_lhs01(a01, x):
    hi, mid, lo = _split3(x)
    return _dot(a01, hi) + _dot(a01, mid) + _dot(a01, lo)


def _dot_exact_lhs01x3(a01x3, x):
    return _dot(a01x3, jnp.concatenate(_split3(x), axis=0))


def _sigmoid_pair(z):
    e = jnp.exp(-jnp.abs(z))
    r = 1.0 / (1.0 + e)
    er = e * r
    pos = z >= 0
    return jnp.where(pos, r, er), jnp.where(pos, er, r)


def _silu(x):
    return x * _sigmoid_pair(x)[0]


def _log_sigmoid(x):
    return jnp.minimum(x, 0.0) - jnp.log(1.0 + jnp.exp(-jnp.abs(x)))


def _rmsnorm_body(x_ref, w_ref, o_ref):
    x = x_ref[...]
    ms = jnp.mean(x * x, axis=-1, keepdims=True)
    o_ref[...] = (x * lax.rsqrt(ms + EPS) * w_ref[...]).astype(o_ref.dtype)


def rmsnorm(x, w, out_dtype, tm):
    m, d = x.shape
    return pl.pallas_call(
        _rmsnorm_body,
        grid=(m // tm,),
        in_specs=[pl.BlockSpec((tm, d), lambda i: (i, 0)), pl.BlockSpec((1, d), lambda i: (0, 0))],
        out_specs=pl.BlockSpec((tm, d), lambda i: (i, 0)),
        out_shape=jax.ShapeDtypeStruct((m, d), out_dtype),
        compiler_params=_params("parallel"),
        name="rmsnorm",
    )(x, w.reshape(1, d))


def _round_body(w_ref, o_ref):
    o_ref[...] = w_ref[...].astype(o_ref.dtype)


def round_weights(w, layer, rows):
    _, k, n = w.shape
    return pl.pallas_call(
        _round_body,
        grid=(k // rows,),
        in_specs=[pl.BlockSpec((None, rows, n), lambda i: (layer, i, 0))],
        out_specs=pl.BlockSpec((rows, n), lambda i: (i, 0)),
        out_shape=jax.ShapeDtypeStruct((k, n), BF16),
        compiler_params=_params("parallel"),
        name="round_weights",
    )(w)


def _round_proj_body(a_ref, w_ref, y_ref, wb_ref):
    wb = w_ref[...].astype(BF16)
    wb_ref[...] = wb
    y_ref[...] = _dot(a_ref[...], wb)


def round_and_proj(a, w, layer, ncols, tn):
    m, kk = a.shape
    return pl.pallas_call(
        _round_proj_body,
        grid=(ncols // tn,),
        in_specs=[pl.BlockSpec((m, kk), lambda j: (0, 0)), pl.BlockSpec((None, kk, tn), lambda j: (layer, 0, j))],
        out_specs=[pl.BlockSpec((m, tn), lambda j: (0, j)), pl.BlockSpec((kk, tn), lambda j: (0, j))],
        out_shape=[jax.ShapeDtypeStruct((m, ncols), F32), jax.ShapeDtypeStruct((kk, ncols), BF16)],
        compiler_params=_params("parallel"),
        name="round_and_proj",
    )(a, w)


def _proj_body(*refs, n_in, has_residual, outs):
    a_refs = refs[:n_in]
    w_refs = refs[n_in:2 * n_in]
    pos = 2 * n_in
    r_ref = refs[pos] if has_residual else None
    o_refs = refs[pos + int(has_residual):]
    acc = None
    for a_ref, w_ref in zip(a_refs, w_refs):
        d = _dot(a_ref[...], w_ref[...])
        acc = d if acc is None else acc + d
    if has_residual:
        acc = r_ref[...] + acc
    for o_ref, (_, scale) in zip(o_refs, outs):
        val = acc if scale == 1.0 else acc * scale
        o_ref[...] = val.astype(o_ref.dtype)


def proj(a_list, w, col0, ncols, tm, tn, outs, residual=None):
    m = a_list[0].shape[0]
    kk = a_list[0].shape[1]
    assert all(a.shape == (m, kk) for a in a_list) and w.shape[0] == kk * len(a_list)
    assert m % tm == 0 and ncols % tn == 0 and col0 % tn == 0
    cb = col0 // tn
    in_specs = [pl.BlockSpec((tm, kk), lambda i, j: (i, 0)) for _ in a_list]
    in_specs += [pl.BlockSpec((kk, tn), functools.partial(lambda i, j, r: (r, cb + j), r=r)) for r in range(len(a_list))]
    args = list(a_list) + [w] * len(a_list)
    if residual is not None:
        in_specs.append(pl.BlockSpec((tm, tn), lambda i, j: (i, j)))
        args.append(residual)
    out_specs = [pl.BlockSpec((tm, tn), lambda i, j: (i, j)) for _ in outs]
    out_shape = [jax.ShapeDtypeStruct((m, ncols), dtype) for dtype, _ in outs]
    return pl.pallas_call(
        functools.partial(_proj_body, n_in=len(a_list), has_residual=residual is not None, outs=tuple(outs)),
        grid=(m // tm, ncols // tn),
        in_specs=in_specs,
        out_specs=out_specs,
        out_shape=out_shape,
        compiler_params=_params("parallel", "parallel"),
        name="proj",
    )(*args)


def _hgrn_maps(c):
    levels = int(np.log2(c))
    assert 2 ** levels == c
    t = np.arange(c)[:, None]
    s = np.arange(c)[None, :]
    mats = [(s <= t), (s > t)]
    for l in range(levels):
        b = 2 ** l
        start = (t // (2 * b)) * (2 * b)
        upper = (t // b) % 2 == 1
        mats.append((upper & (s >= start + b) & (s <= t)) | ((~upper) & (s > t) & (s <= start + b - 1)))
    return np.concatenate(mats, axis=0).astype(np.float32), levels


def _hgrn_body(q_ref, z_ref, v_ref, g_ref, lb_ref, nw_ref, a_ref, s0_ref, o_ref, sout_ref, st_scr,
               *, c, n_chunks, levels, heads):
    tb = pl.program_id(2)

    @pl.when(tb == 0)
    def _():
        for hh in range(heads):
            st_scr[hh] = s0_ref[0, hh].T

    nw = nw_ref[...]
    amat = a_ref[...]
    row = lax.broadcasted_iota(jnp.int32, (c, c), 0)
    col = lax.broadcasted_iota(jnp.int32, (c, c), 1)
    xor = row ^ col

    units = [(slice(ci * c, (ci + 1) * c), slice(hh * HEAD_DIM, (hh + 1) * HEAD_DIM))
             for hh in range(heads) for ci in range(n_chunks)]
    nu = len(units)
    lbs = [lb_ref[:, cols] for _, cols in units]
    qs = [_silu(q_ref[rows, cols]) for rows, cols in units]
    sigs = [_sigmoid_pair(z_ref[rows, cols]) for rows, cols in units]
    gs = [jnp.log(lbs[u] + (1.0 - lbs[u]) * sigs[u][0]) for u in range(nu)]
    ks = [(1.0 - lbs[u]) * sigs[u][1] for u in range(nu)]
    vs = [v_ref[rows, cols] for rows, cols in units]
    vbs = [v.astype(BF16) for v in vs]
    es = [jnp.exp(_dot_exact_lhs01x3(amat, g)) for g in gs]
    atts = [None] * nu
    for l in range(levels - 1, -1, -1):
        for u in range(nu):
            el = es[u][(2 + l) * c:(3 + l) * c]
            al = _dot_nt((qs[u] * el).astype(BF16), (ks[u] * el).astype(BF16))
            atts[u] = al if atts[u] is None else jnp.where(xor < 2 ** (l + 1), al, atts[u])
    atts = [jnp.where(row > col, att, 0.0).astype(BF16) for att in atts]
    o_intras = [_dot(atts[u], vbs[u]) + jnp.sum(qs[u] * ks[u], axis=-1, keepdims=True) * vs[u] for u in range(nu)]
    upds = [_dot_tn(vbs[u], (ks[u] * es[u][c:2 * c]).astype(BF16)) for u in range(nu)]
    qgs = [(qs[u] * es[u][0:c]).astype(BF16) for u in range(nu)]

    for hh in range(heads):
        st = st_scr[hh]
        for ci in range(n_chunks):
            u = hh * n_chunks + ci
            rows, cols = units[u]
            o = _dot_nt(qgs[u], st.astype(BF16)) + o_intras[u]
            st = st * es[u][c - 1:c] + upds[u]
            ms = jnp.mean(o * o, axis=-1, keepdims=True)
            y = o * lax.rsqrt(ms + EPS) * nw
            o_ref[rows, cols] = (y * _silu(g_ref[rows, cols])).astype(o_ref.dtype)
        st_scr[hh] = st

    @pl.when(tb == pl.num_programs(2) - 1)
    def _():
        for hh in range(heads):
            sout_ref[0, hh] = st_scr[hh].T


def hgrn2(qa, fa, ia, ga, lb, a_norm_w, s0, batch, seq, rows_per_step, c, heads_per_step, out_dtype,
          col0=(0, 0, 0, 0)):
    n = qa.shape[0]
    width = lb.shape[0]
    h = width // HEAD_DIM
    assert n == batch * seq and seq % rows_per_step == 0 and rows_per_step % c == 0 and h % heads_per_step == 0
    nb = seq // rows_per_step
    gw = heads_per_step * HEAD_DIM
    amat_np, levels = _hgrn_maps(c)
    amat = jnp.asarray(np.concatenate([amat_np] * 3, axis=1), BF16)
    row_spec = pl.BlockSpec((rows_per_step, gw), lambda b, hh, t: (b * nb + t, hh))
    in_row_specs = [pl.BlockSpec((rows_per_step, gw), functools.partial(lambda b, hh, t, off: (b * nb + t, off + hh), off=c0 // gw))
                    for c0 in col0]
    assert all(c0 % gw == 0 for c0 in col0)
    state_spec = pl.BlockSpec((1, heads_per_step, HEAD_DIM, HEAD_DIM), lambda b, hh, t: (b, hh, 0, 0))
    body = functools.partial(_hgrn_body, c=c, n_chunks=rows_per_step // c, levels=levels, heads=heads_per_step)
    return pl.pallas_call(
        body,
        grid=(batch, h // heads_per_step, nb),
        in_specs=in_row_specs + [
                  pl.BlockSpec((1, gw), lambda b, hh, t: (0, hh)),
                  pl.BlockSpec((1, HEAD_DIM), lambda b, hh, t: (0, 0)),
                  pl.BlockSpec(amat.shape, lambda b, hh, t: (0, 0)),
                  state_spec],
        out_specs=[row_spec, state_spec],
        out_shape=[jax.ShapeDtypeStruct((n, width), out_dtype),
                   jax.ShapeDtypeStruct((batch, h, HEAD_DIM, HEAD_DIM), F32)],
        scratch_shapes=[pltpu.VMEM((heads_per_step, HEAD_DIM, HEAD_DIM), F32)],
        compiler_params=_params("parallel", "parallel", "arbitrary"),
        name="hgrn2",
    )(qa, fa, ia, ga, lb.reshape(1, width), a_norm_w.reshape(1, HEAD_DIM), amat, s0)


def _neg_abs(x):
    return -jnp.abs(x)


def _sb_prompt_body(q_ref, k_ref, v_ref, g_ref, uu_ref, o_ref, run_scr, acc_scr, w_scr, *, tq, tk, rc):
    i = pl.program_id(1)
    nd = tq // tk
    nl = tk // LANES
    q0 = pl.multiple_of(i * tq, tq)
    run_scr[...] = jnp.zeros(run_scr.shape, F32)
    acc_scr[...] = jnp.zeros(acc_scr.shape, F32)

    def apply_pending(slot, k_prev, first_row):
        vt = v_ref[pl.ds(pl.multiple_of(k_prev, tk), tk), :]
        for r0 in range(first_row, tq, rc):
            rows = slice(r0, min(r0 + rc, tq))
            acc_scr[rows, :] = acc_scr[rows, :] + _dot(w_scr[slot, rows, :], vt)

    def score(slot, k0, first_row, masked):
        kt = k_ref[pl.ds(pl.multiple_of(k0, tk), tk), :]
        for r0 in range(first_row, tq, rc):
            r1 = min(r0 + rc, tq)
            rows = slice(r0, r1)
            z = _dot_nt(q_ref[rows, :], kt)
            if masked:
                qpos = lax.broadcasted_iota(jnp.int32, (r1 - r0, LANES), 0) + (q0 + r0)
                kpos = lax.broadcasted_iota(jnp.int32, (r1 - r0, LANES), 1) + k0
            zs, sps, his, los, valids = [], [], [], [], []
            rowsum = None
            for c in range(nl):
                zc = z[:, c * LANES:(c + 1) * LANES]
                sp = jnp.maximum(zc, 0.0) + jnp.log2(1.0 + jnp.exp2(_neg_abs(zc)))
                if masked:
                    valid = (kpos + c * LANES) < qpos
                    sp = jnp.where(valid, sp, 0.0)
                    valids.append(valid)
                hi = sp.astype(BF16)
                lo = (sp - hi.astype(F32)).astype(BF16)
                zs.append(zc); sps.append(sp); his.append(hi); los.append(lo)
                rowsum = sp if rowsum is None else rowsum + sp
            tail = _dot(jnp.concatenate(his + los, axis=1), uu_ref[...])
            run = run_scr[rows, :]
            for c in range(nl):
                w = jnp.exp2(zs[c] - (sps[c] + tail[:, c * LANES:(c + 1) * LANES] + run))
                if masked:
                    w = jnp.where(valids[c], w, 0.0)
                w_scr[slot, rows, c * LANES:(c + 1) * LANES] = w.astype(BF16)
            run_scr[rows, :] = run + jnp.sum(rowsum, axis=-1, keepdims=True)

    assert nd % 2 == 0
    for d in range(nd - 1, 0, -2):
        score(0, q0 + d * tk, d * tk, True)
        if d + 1 < nd:
            apply_pending(1, q0 + (d + 1) * tk, (d + 1) * tk)
        score(1, q0 + (d - 1) * tk, (d - 1) * tk, True)
        apply_pending(0, q0 + d * tk, d * tk)

    def before(carry):
        jj, kp, _ = carry
        for u in range(unroll):
            k_a = q0 - (2 * (unroll * jj + u) + 1) * tk
            score(0, k_a, 0, False)
            apply_pending(1, kp, 0)
            score(1, k_a - tk, 0, False)
            apply_pending(0, k_a, 0)
            kp = k_a - tk
        return jj + 1, kp, jnp.min(run_scr[...]) < F32_UNDERFLOW_LOG2

    unroll = 1
    assert nd % (2 * unroll) == 0
    n_trips = (i * nd) // (2 * unroll)
    _, k_pending, _ = lax.while_loop(lambda carry: (carry[0] < n_trips) & carry[2], before,
                                     (jnp.int32(0), q0, jnp.bool_(True)))
    apply_pending(1, k_pending, 0)
    o_ref[...] = (acc_scr[...] * _silu(g_ref[...])).astype(o_ref.dtype)


def _tail_matrix(n):
    sp = np.arange(n)[:, None]
    s = np.arange(n)[None, :]
    return (sp > s).astype(np.float32)


def sb_prompt(q, k, v, gate, tq, tk, rc, out_dtype):
    t, width = q.shape
    h = width // HEAD_DIM
    u = _tail_matrix(tk)
    uu = jnp.asarray(np.concatenate([u, u], axis=0), BF16)
    q_spec = pl.BlockSpec((tq, HEAD_DIM), lambda hh, i: (i, hh))
    kv_spec = pl.BlockSpec((t, HEAD_DIM), lambda hh, i: (0, hh))
    return pl.pallas_call(
        functools.partial(_sb_prompt_body, tq=tq, tk=tk, rc=rc),
        grid=(h, t // tq),
        in_specs=[q_spec, kv_spec, kv_spec, q_spec, pl.BlockSpec(uu.shape, lambda hh, i: (0, 0))],
        out_specs=q_spec,
        out_shape=jax.ShapeDtypeStruct((t, width), out_dtype),
        scratch_shapes=[pltpu.VMEM((tq, LANES), F32), pltpu.VMEM((tq, HEAD_DIM), F32),
                        pltpu.VMEM((2, tq, tk), BF16)],
        compiler_params=_params("parallel", "arbitrary"),
        name="sb_prompt",
    )(q, k, v, gate, uu)


def _logf_cumsum_body(fl_ref, b_ref, tri_ref, lf_ref, f2_ref, carry_scr):
    @pl.when(pl.program_id(0) == 0)
    def _():
        carry_scr[...] = jnp.zeros_like(carry_scr)

    lf = _log_sigmoid(fl_ref[...] + b_ref[...])
    lf_ref[...] = lf
    f = carry_scr[...] + _dot_exact_lhs01(tri_ref[...], lf)
    f2_ref[...] = f * LOG2E
    carry_scr[...] = f[f.shape[0] - 1:, :]


def logf_cumsum(fl, b_forget, blk):
    t, h = fl.shape
    tri = jnp.asarray(np.tril(np.ones((blk, blk), np.float32)), BF16)
    spec = pl.BlockSpec((blk, h), lambda i: (i, 0))
    return pl.pallas_call(
        _logf_cumsum_body,
        grid=(t // blk,),
        in_specs=[spec, pl.BlockSpec((1, h), lambda i: (0, 0)), pl.BlockSpec((blk, blk), lambda i: (0, 0))],
        out_specs=[spec, spec],
        out_shape=[jax.ShapeDtypeStruct((t, h), F32), jax.ShapeDtypeStruct((t, h), F32)],
        scratch_shapes=[pltpu.VMEM((1, h), F32)],
        compiler_params=_params("arbitrary"),
        name="logf_cumsum",
    )(fl, b_forget.reshape(1, h), tri)


def _logf_body(fl_ref, b_ref, lf_ref):
    lf_ref[...] = _log_sigmoid(fl_ref[...] + b_ref[...])


def logf_only(fl, b_forget):
    t, h = fl.shape
    return pl.pallas_call(
        _logf_body,
        grid=(1,),
        in_specs=[pl.BlockSpec((t, h), lambda i: (0, 0)), pl.BlockSpec((1, h), lambda i: (0, 0))],
        out_specs=pl.BlockSpec((t, h), lambda i: (0, 0)),
        out_shape=jax.ShapeDtypeStruct((t, h), F32),
        name="logf",
    )(fl, b_forget.reshape(1, h))


_BIAS_PIECES = 3


def _bias_selectors(h):
    sel = np.zeros((2, h, _BIAS_PIECES, h, LANES), np.float32)
    for hh in range(h):
        for p in range(_BIAS_PIECES):
            sel[0, hh, p, hh, p] = 1.0
            sel[1, hh, p, hh, _BIAS_PIECES + p] = -1.0
    return sel


def _bias_columns(f, sel_ref, query_side):
    out = None
    for p, piece in enumerate(_split3(f)):
        d = _dot(piece, sel_ref[0, p])
        out = d if out is None else out + d
    lane = lax.broadcasted_iota(jnp.int32, out.shape, 1)
    ones_at = (lane >= _BIAS_PIECES) & (lane < 2 * _BIAS_PIECES) if query_side else lane < _BIAS_PIECES
    return jnp.where(ones_at, 1.0, out).astype(BF16)


def _fox_prompt_body(q_ref, k_ref, v_ref, g_ref, f_ref, selq_ref, selk_ref, o_ref,
                     kx_scr, vx_scr, m_scr, acc_scr, p_scr, alpha_scr, kmax_scr, ub_scr, *, tq, tk, rc):
    i = pl.program_id(1)
    nd = tq // tk
    nl = tk // LANES
    t_all = k_ref.shape[0]

    head = pl.program_id(0)

    def own_column(f_rows):
        lane = lax.broadcasted_iota(jnp.int32, f_rows.shape, 1)
        return jnp.sum(jnp.where(lane == head, f_rows, 0.0), axis=-1, keepdims=True)

    @pl.when(i == 0)
    def _():
        kb = k_ref[...].astype(BF16)
        kx_scr[:, 0:HEAD_DIM] = kb
        kx_scr[:, HEAD_DIM:] = _bias_columns(f_ref[...], selk_ref, False)
        vx_scr[:, 0:HEAD_DIM] = v_ref[...].astype(BF16)
        vx_scr[:, HEAD_DIM:] = jnp.ones((t_all, LANES), BF16)
        kf = kb.astype(F32)
        knorm2 = jnp.max(jnp.sum(kf * kf, axis=-1, keepdims=True), axis=0, keepdims=True)
        kmax_scr[...] = jnp.broadcast_to(jnp.sqrt(knorm2), kmax_scr.shape)

    q0 = pl.multiple_of(i * tq, tq)
    qx = jnp.concatenate([q_ref[...], _bias_columns(f_ref[pl.ds(q0, tq), :], selq_ref, True)], axis=1)
    m_scr[...] = jnp.full(m_scr.shape, -jnp.inf, F32)
    acc_scr[...] = jnp.zeros(acc_scr.shape, F32)
    qf = q_ref[...].astype(F32)
    qnorm = jnp.sqrt(jnp.sum(qf * qf, axis=-1, keepdims=True))
    ub_scr[...] = qnorm * kmax_scr[...] * (1.0 + 2.0 ** -10) + own_column(f_ref[pl.ds(q0, tq), :])

    def apply_pending(slot, k_prev, first_row):
        vxt = vx_scr[pl.ds(pl.multiple_of(k_prev, tk), tk), :]
        for r0 in range(first_row, tq, rc):
            rows = slice(r0, r0 + rc)
            pv = _dot(p_scr[slot, rows, :], vxt)
            alpha = alpha_scr[slot, rows, :]
            acc_scr[rows, 0:HEAD_DIM] = alpha * acc_scr[rows, 0:HEAD_DIM] + pv[:, 0:HEAD_DIM]
            acc_scr[rows, HEAD_DIM:] = alpha * acc_scr[rows, HEAD_DIM:] + pv[:, HEAD_DIM:]

    def score(slot, k0, first_row, masked):
        kxt = kx_scr[pl.ds(pl.multiple_of(k0, tk), tk), :]
        for r0 in range(first_row, tq, rc):
            rows = slice(r0, r0 + rc)
            s = _dot_nt(qx[r0:r0 + rc, :], kxt)
            sb = [s[:, c * LANES:(c + 1) * LANES] for c in range(nl)]
            if masked:
                qpos = lax.broadcasted_iota(jnp.int32, (rc, LANES), 0) + (q0 + r0)
                kpos = lax.broadcasted_iota(jnp.int32, (rc, LANES), 1) + k0
                sb = [jnp.where((kpos + c * LANES) <= qpos, sb[c], -jnp.inf) for c in range(nl)]
            mx = sb[0]
            for c in range(1, nl):
                mx = jnp.maximum(mx, sb[c])
            m_old = m_scr[rows, :]
            m_new = jnp.maximum(m_old, jnp.max(mx, axis=-1, keepdims=True))
            alpha_scr[slot, rows, :] = jnp.exp2(m_old - m_new)
            for c in range(nl):
                p_scr[slot, rows, c * LANES:(c + 1) * LANES] = jnp.exp2(sb[c] - m_new).astype(BF16)
            m_scr[rows, :] = m_new

    assert nd % 2 == 0 and tk % rc == 0
    for d in range(0, nd, 2):
        score(0, q0 + d * tk, d * tk, True)
        if d > 0:
            apply_pending(1, q0 + (d - 1) * tk, (d - 1) * tk)
        score(1, q0 + (d + 1) * tk, (d + 1) * tk, True)
        apply_pending(0, q0 + d * tk, d * tk)

    p_scr[1, 0:(nd - 1) * tk, :] = jnp.zeros(((nd - 1) * tk, tk), BF16)
    alpha_scr[1, 0:(nd - 1) * tk, :] = jnp.ones(((nd - 1) * tk, LANES), F32)

    def pair(k_a, kp):
        score(0, k_a, 0, False)
        apply_pending(1, kp, 0)
        score(1, k_a - tk, 0, False)
        apply_pending(0, k_a, 0)
        return k_a - tk

    def visible(k_hi):
        f_hi = own_column(f_ref[pl.ds(jnp.maximum(k_hi, 0), 1), :])
        return jnp.max(ub_scr[...] - f_hi - m_scr[...]) > -(F32_UNDERFLOW_LOG2 + 2.0)

    def trip(carry):
        jj, kp, _ = carry
        kp = pair(q0 - (2 * jj + 1) * tk, kp)
        return jj + 1, kp, visible(kp - 1)

    n_pairs = (i * nd) // 2
    _, k_pending, _ = lax.while_loop(lambda carry: (carry[0] < n_pairs) & carry[2], trip,
                                     (jnp.int32(0), q0 + (nd - 1) * tk, visible(q0 - 1)))
    apply_pending(1, k_pending, 0)
    o_ref[...] = (acc_scr[:, 0:HEAD_DIM] / acc_scr[:, HEAD_DIM:] * _silu(g_ref[...])).astype(o_ref.dtype)


def fox_prompt(q, k, v, gate, f2, tq, tk, rc, out_dtype):
    t, width = q.shape
    h = width // HEAD_DIM
    sel = jnp.asarray(_bias_selectors(h), BF16)
    q_spec = pl.BlockSpec((tq, HEAD_DIM), lambda hh, i: (i, hh))
    kv_spec = pl.BlockSpec((t, HEAD_DIM), lambda hh, i: (0, hh))
    sel_spec = pl.BlockSpec((1, _BIAS_PIECES, h, LANES), lambda hh, i: (hh, 0, 0, 0))
    return pl.pallas_call(
        functools.partial(_fox_prompt_body, tq=tq, tk=tk, rc=rc),
        grid=(h, t // tq),
        in_specs=[q_spec, kv_spec, kv_spec, q_spec, pl.BlockSpec((t, h), lambda hh, i: (0, 0)),
                  sel_spec, sel_spec],
        out_specs=q_spec,
        out_shape=jax.ShapeDtypeStruct((t, width), out_dtype),
        scratch_shapes=[pltpu.VMEM((t, 2 * HEAD_DIM), BF16), pltpu.VMEM((t, 2 * HEAD_DIM), BF16),
                        pltpu.VMEM((tq, LANES), F32), pltpu.VMEM((tq, 2 * HEAD_DIM), F32),
                        pltpu.VMEM((2, tq, tk), BF16), pltpu.VMEM((2, tq, LANES), F32),
                        pltpu.VMEM((1, LANES), F32), pltpu.VMEM((tq, LANES), F32)],
        compiler_params=_params("parallel", "arbitrary"),
        name="fox_prompt",
    )(q, k, v, gate, f2, sel[0], sel[1])


HEADS_PER_GROUP = 8


def _expand_queries(q, tq):
    gw = q.shape[1]
    rep = jnp.concatenate([q] * HEADS_PER_GROUP, axis=0)
    r = lax.broadcasted_iota(jnp.int32, (HEADS_PER_GROUP * tq, gw), 0) // tq
    cidx = lax.broadcasted_iota(jnp.int32, (HEADS_PER_GROUP * tq, gw), 1) // HEAD_DIM
    return jnp.where(r == cidx, rep, 0.0)


def _collect_heads(full, tq):
    return jnp.concatenate(
        [full[hh * tq:(hh + 1) * tq, hh * HEAD_DIM:(hh + 1) * HEAD_DIM] for hh in range(HEADS_PER_GROUP)],
        axis=1)


def _gather_keys(cache_ref, new_ref, scr, past, tk):
    g = HEADS_PER_GROUP
    x = cache_ref[0].reshape(past // g, g, g, HEAD_DIM)
    x = jnp.swapaxes(x, 1, 2)
    for hh in range(g):
        scr[0:past, hh * HEAD_DIM:(hh + 1) * HEAD_DIM] = x[:, hh].reshape(past, HEAD_DIM).astype(BF16)
    scr[past:tk, :] = new_ref[...].astype(BF16)


def _row_blocks(n):
    return [(r0, min(r0 + LANES, n)) for r0 in range(0, n, LANES)]


def _cumsum_rows(pieces, tri_ref):
    n = pieces[0].shape[0]
    out, carry = [], None
    for r0, r1 in _row_blocks(n):
        tri = tri_ref[0:r1 - r0, 0:r1 - r0]
        local = None
        for piece in pieces:
            d = _dot(tri, piece[r0:r1])
            local = d if local is None else local + d
        if carry is not None:
            local = local + carry
        carry = local[r1 - r0 - 1:r1 - r0]
        out.append(local)
    return jnp.concatenate(out, axis=0)


def _tailsum_rows(pieces, tri_ref):
    n = pieces[0].shape[0]
    out, carry = [], None
    for r0, r1 in reversed(_row_blocks(n)):
        tri = tri_ref[0:r1 - r0, 0:r1 - r0]
        local, total = None, None
        for piece in pieces:
            blk = piece[r0:r1]
            d = _dot_tn(tri, blk)
            local = d if local is None else local + d
            t = blk.astype(F32)
            total = t if total is None else total + t
        local = local - total
        if carry is not None:
            local = local + carry
        carry = local[0:1] + total[0:1]
        out.append(local)
    return jnp.concatenate(out[::-1], axis=0)


def _sb_decode_body(q_ref, kn_ref, vn_ref, kc_ref, vc_ref, g_ref, tri_ref, o_ref, k_scr, v_scr, *, past, tq):
    tk = past + tq
    _gather_keys(kc_ref, kn_ref, k_scr, past, tk)
    _gather_keys(vc_ref, vn_ref, v_scr, past, tk)

    lanes = HEADS_PER_GROUP * tq
    qx = _expand_queries(q_ref[...] * (HEAD_DIM ** -0.5), tq).astype(BF16)
    z = _dot_nt(k_scr[...], qx)
    kpos = lax.broadcasted_iota(jnp.int32, (tk, lanes), 0)
    qpos = past + lax.broadcasted_iota(jnp.int32, (tk, lanes), 1) % tq
    valid = kpos < qpos
    ls_neg = -(jnp.maximum(z, 0.0) + jnp.log(1.0 + jnp.exp(-jnp.abs(z))))
    lm = jnp.where(valid, ls_neg, 0.0)
    hi = lm.astype(BF16)
    lo = (lm - hi.astype(F32)).astype(BF16)
    tail = _tailsum_rows((hi, lo), tri_ref)
    w = jnp.where(valid, jnp.exp(z + ls_neg + tail), 0.0)
    full = _dot_tn(w.astype(BF16), v_scr[...])
    o_ref[...] = (_collect_heads(full, tq) * _silu(g_ref[...])).astype(o_ref.dtype)


def _cache_spec(layer, past):
    return pl.BlockSpec((None, 1, past, HEADS_PER_GROUP, HEAD_DIM), lambda b, g: (layer, b, 0, g, 0))


def _decode_row_specs(tq, gw, col0):
    assert all(c0 % gw == 0 for c0 in col0)
    return [pl.BlockSpec((tq, gw), functools.partial(lambda b, g, off: (b, off + g), off=c0 // gw)) for c0 in col0]


def sb_decode(q, k_new, v_new, k_cache, v_cache, layer, gate, batch, tq, out_dtype, col0=(0, 0, 0, 0)):
    n = q.shape[0]
    width = k_cache.shape[3] * HEAD_DIM
    past = k_cache.shape[2]
    tk = past + tq
    gw = HEADS_PER_GROUP * HEAD_DIM
    groups = width // gw
    tri = jnp.asarray(np.tril(np.ones((LANES, LANES), np.float32)), BF16)
    row_spec = pl.BlockSpec((tq, gw), lambda b, g: (b, g))
    rs = _decode_row_specs(tq, gw, col0)
    cache_spec = _cache_spec(layer, past)
    return pl.pallas_call(
        functools.partial(_sb_decode_body, past=past, tq=tq),
        grid=(batch, groups),
        in_specs=[rs[0], rs[1], rs[2], cache_spec, cache_spec, rs[3],
                  pl.BlockSpec((LANES, LANES), lambda b, g: (0, 0))],
        out_specs=row_spec,
        out_shape=jax.ShapeDtypeStruct((n, width), out_dtype),
        scratch_shapes=[pltpu.VMEM((tk, gw), BF16), pltpu.VMEM((tk, gw), BF16)],
        compiler_params=_params("parallel", "parallel"),
        name="sb_decode",
    )(q, k_new, v_new, k_cache, v_cache, gate, tri)


def _fox_decode_body(q_ref, kn_ref, vn_ref, kc_ref, vc_ref, g_ref, lfn_ref, lfc_ref, ex_ref, tri_ref, o_ref,
                     k_scr, v_scr, lf_scr, *, past, tq):
    tk = past + tq
    _gather_keys(kc_ref, kn_ref, k_scr, past, tk)
    _gather_keys(vc_ref, vn_ref, v_scr, past, tk)
    lf_scr[0:past, :] = lfc_ref[0]
    lf_scr[past:tk, :] = lfn_ref[...]

    lanes = HEADS_PER_GROUP * tq
    hi, mid, lo = _split3(lf_scr[...])
    ex = ex_ref[0]
    lfx = _dot(hi, ex) + _dot(mid, ex) + _dot(lo, ex)
    f_key = _cumsum_rows(_split3(lfx), tri_ref)
    kpos = lax.broadcasted_iota(jnp.int32, (tk, lanes), 0)
    qpos = past + lax.broadcasted_iota(jnp.int32, (tk, lanes), 1) % tq
    f_query = jnp.sum(jnp.where(kpos == qpos, f_key, 0.0), axis=0, keepdims=True)

    qx = _expand_queries(q_ref[...] * (HEAD_DIM ** -0.5), tq).astype(BF16)
    s = _dot_nt(k_scr[...], qx) + (f_query - f_key)
    s = jnp.where(kpos <= qpos, s, -jnp.inf)
    p = jnp.exp(s - jnp.max(s, axis=0, keepdims=True))
    p = p / jnp.sum(p, axis=0, keepdims=True)
    full = _dot_tn(p.astype(BF16), v_scr[...])
    o_ref[...] = (_collect_heads(full, tq) * _silu(g_ref[...])).astype(o_ref.dtype)


def fox_decode(q, k_new, v_new, k_cache, v_cache, layer, gate, lf_new, lf_cache, batch, tq, out_dtype,
               col0=(0, 0, 0, 0)):
    n = q.shape[0]
    h = k_cache.shape[3]
    width = h * HEAD_DIM
    past = k_cache.shape[2]
    tk = past + tq
    gw = HEADS_PER_GROUP * HEAD_DIM
    groups = width // gw
    lanes = HEADS_PER_GROUP * tq
    tri = jnp.asarray(np.tril(np.ones((LANES, LANES), np.float32)), BF16)
    head_of_lane = np.arange(lanes)[None, None, :] // tq + HEADS_PER_GROUP * np.arange(groups)[:, None, None]
    expand = jnp.asarray((np.arange(h)[None, :, None] == head_of_lane).astype(np.float32), BF16)
    row_spec = pl.BlockSpec((tq, gw), lambda b, g: (b, g))
    rs = _decode_row_specs(tq, gw, col0)
    cache_spec = _cache_spec(layer, past)
    return pl.pallas_call(
        functools.partial(_fox_decode_body, past=past, tq=tq),
        grid=(batch, groups),
        in_specs=[rs[0], rs[1], rs[2], cache_spec, cache_spec, rs[3],
                  pl.BlockSpec((tq, h), lambda b, g: (b, 0)),
                  pl.BlockSpec((None, 1, past, h), lambda b, g: (layer, b, 0, 0)),
                  pl.BlockSpec((1, h, lanes), lambda b, g: (g, 0, 0)),
                  pl.BlockSpec((LANES, LANES), lambda b, g: (0, 0))],
        out_specs=row_spec,
        out_shape=jax.ShapeDtypeStruct((n, width), out_dtype),
        scratch_shapes=[pltpu.VMEM((tk, gw), BF16), pltpu.VMEM((tk, gw), BF16), pltpu.VMEM((tk, h), F32)],
        compiler_params=_params("parallel", "parallel"),
        name="fox_decode",
    )(q, k_new, v_new, k_cache, v_cache, gate, lf_new, lf_cache, expand, tri)


QK_SCALE_LOG2 = LOG2E * HEAD_DIM ** -0.5
FLAT32 = (F32, 1.0)
FLAT16 = (BF16, 1.0)
QUERY16 = (BF16, QK_SCALE_LOG2)


def _even_prompt(x, norm_w, w_in, w_out, lb, a_norm_w, s0, seq):
    n, d = x.shape
    half = d // 2
    hn = rmsnorm(x, norm_w, BF16, NORM_ROWS)

    def p(group, *outs):
        return proj([hn], w_in, group * half, half, PROJ_TILE, PROJ_TILE, outs)

    (qa,), (fa,), (ia,), (ga,), (gb,) = p(0, FLAT32), p(1, FLAT32), p(2, FLAT32), p(3, FLAT32), p(7, FLAT32)
    (qb,) = p(4, QUERY16)
    kb_leaf, kb = p(5, FLAT32, FLAT16)
    vb_leaf, vb = p(6, FLAT32, FLAT16)
    oa, s_new = hgrn2(qa, fa, ia, ga, lb, a_norm_w, s0, 1, seq, HGRN_ROWS, HGRN_CHUNK, 1, BF16)
    ob = sb_prompt(qb, kb, vb, gb, ATTN_ROWS, SB_KEYS, ATTN_ROWS, BF16)
    (y,) = proj([oa, ob], w_out, 0, d, PROJ_TILE, PROJ_TILE, (FLAT32,), residual=x)
    return y, s_new, kb_leaf, vb_leaf


def _even_decode(x, norm_w, w_in_f32, layer, w_out, lb, a_norm_w, s0, k_cache, v_cache, batch, seq):
    n, d = x.shape
    half = d // 2
    hn = rmsnorm(x, norm_w, BF16, n)
    cols, w_in = round_and_proj(hn, w_in_f32, layer, 8 * half, DECODE_PROJ_COLS)
    oa, s_new = hgrn2(cols, cols, cols, cols, lb, a_norm_w, s0, batch, seq, seq, seq, half // HEAD_DIM, BF16,
                      col0=(0, half, 2 * half, 3 * half))
    ob = sb_decode(cols, cols, cols, k_cache, v_cache, layer, cols, batch, seq, BF16,
                   col0=(4 * half, 5 * half, 6 * half, 7 * half))
    (y,) = proj([oa, ob], w_out, 0, d, n, DECODE_PROJ_COLS, (FLAT32,), residual=x)
    return y, s_new, cols[:, 5 * half:6 * half], cols[:, 6 * half:7 * half], w_in


def _odd_prompt(x, norm_w, w_in, w_fl, b_forget, w_out, seq):
    n, d = x.shape
    heads = d // HEAD_DIM
    hn = rmsnorm(x, norm_w, BF16, NORM_ROWS)

    def p(group, *outs):
        return proj([hn], w_in, group * d, d, PROJ_TILE, PROJ_TILE, outs)

    (q,), (k,), (v,), (gate,) = p(0, QUERY16), p(1, FLAT32), p(2, FLAT32), p(3, FLAT32)
    (fl,) = proj([hn], w_fl, 0, heads, PROJ_TILE, heads, (FLAT32,))
    logf, f2 = logf_cumsum(fl, b_forget, CUMSUM_ROWS)
    o = fox_prompt(q, k, v, gate, f2, ATTN_ROWS, FOX_KEYS, FOX_ROW_CHUNK, BF16)
    (y,) = proj([o], w_out, 0, d, PROJ_TILE, PROJ_TILE, (FLAT32,), residual=x)
    return y, k, v, logf


def _odd_decode(x, norm_w, w_in, w_fl, b_forget, w_out, k_cache, v_cache, lf_cache, layer, batch, seq):
    n, d = x.shape
    heads = d // HEAD_DIM
    hn = rmsnorm(x, norm_w, BF16, n)
    (cols,) = proj([hn], w_in, 0, 4 * d, n, DECODE_PROJ_COLS, (FLAT32,))
    (fl,) = proj([hn], w_fl, 0, heads, n, heads, (FLAT32,))
    logf = logf_only(fl, b_forget)
    o = fox_decode(cols, cols, cols, k_cache, v_cache, layer, cols, logf, lf_cache, batch, seq, BF16,
                   col0=(0, d, 2 * d, 3 * d))
    (y,) = proj([o], w_out, 0, d, n, DECODE_PROJ_COLS, (FLAT32,), residual=x)
    return y, cols[:, d:2 * d], cols[:, 2 * d:3 * d], logf


def kernel(x_prompt, x_sample, state_a_hgrn, cache_b_k, cache_b_v, cache_c_k, cache_c_v, cache_c_logf,
           norm_w, final_norm_w, w_in_even, w_out_even, lb_logits, a_norm_w, w_in_odd, b_forget, w_out_odd):
    bp, tp, d = x_prompt.shape
    bs, ts, _ = x_sample.shape
    assert bp == 1
    depth = norm_w.shape[0]
    n_even = w_in_even.shape[0]
    lb_all = jnp.cumsum(jax.nn.softmax(lb_logits.astype(F32), axis=0), axis=0)[:n_even]

    hp = x_prompt.reshape(bp * tp, d)
    hs = x_sample.reshape(bs * ts, d)
    outs = {name: [] for name in ("sa_p", "sa_s", "bk_p", "bv_p", "bk_s", "bv_s",
                                  "ck_p", "cv_p", "cf_p", "ck_s", "cv_s", "cf_s")}
    for layer in range(depth):
        j = layer // 2
        if layer % 2 == 0:
            a_heads = state_a_hgrn.shape[2]
            b_heads = cache_b_k.shape[3]
            zeros = jnp.zeros((bp, a_heads) + state_a_hgrn.shape[3:], F32)
            w_out = round_weights(w_out_even, j, ROUND_ROWS)
            hs, ss, ksm, vsm, w_in = _even_decode(hs, norm_w[layer], w_in_even, j, w_out, lb_all[j], a_norm_w[j],
                                                  state_a_hgrn[j], cache_b_k, cache_b_v, bs, ts)
            hp, sp, kp, vp = _even_prompt(hp, norm_w[layer], w_in, w_out, lb_all[j], a_norm_w[j], zeros, tp)
            outs["sa_p"].append(sp); outs["sa_s"].append(ss)
            outs["bk_p"].append(kp.reshape(bp, tp, b_heads, HEAD_DIM))
            outs["bv_p"].append(vp.reshape(bp, tp, b_heads, HEAD_DIM))
            outs["bk_s"].append(ksm.reshape(bs, ts, b_heads, HEAD_DIM))
            outs["bv_s"].append(vsm.reshape(bs, ts, b_heads, HEAD_DIM))
        else:
            c_heads = cache_c_k.shape[3]
            w_in = w_in_odd[j].astype(BF16)
            w_fl = w_in[:, 4 * c_heads * HEAD_DIM:]
            w_out = round_weights(w_out_odd, j, ROUND_ROWS)
            hp, kp, vp, fp = _odd_prompt(hp, norm_w[layer], w_in, w_fl, b_forget[j], w_out, tp)
            hs, ksm, vsm, fsm = _odd_decode(hs, norm_w[layer], w_in, w_fl, b_forget[j], w_out,
                                            cache_c_k, cache_c_v, cache_c_logf, j, bs, ts)
            outs["ck_p"].append(kp.reshape(bp, tp, c_heads, HEAD_DIM))
            outs["cv_p"].append(vp.reshape(bp, tp, c_heads, HEAD_DIM))
            outs["cf_p"].append(fp.reshape(bp, tp, c_heads))
            outs["ck_s"].append(ksm.reshape(bs, ts, c_heads, HEAD_DIM))
            outs["cv_s"].append(vsm.reshape(bs, ts, c_heads, HEAD_DIM))
            outs["cf_s"].append(fsm.reshape(bs, ts, c_heads))
    y_prompt = rmsnorm(hp, final_norm_w, F32, NORM_ROWS).reshape(bp, tp, d)
    y_sample = rmsnorm(hs, final_norm_w, F32, bs * ts).reshape(bs, ts, d)
    st = {k: jnp.stack(v) for k, v in outs.items()}
    return (y_prompt, y_sample, st["sa_p"], st["sa_s"], st["bk_p"], st["bv_p"], st["bk_s"], st["bv_s"],
            st["ck_p"], st["cv_p"], st["cf_p"], st["ck_s"], st["cv_s"], st["cf_s"])
```

```python
import functools

import numpy as np
import jax
import jax.numpy as jnp
from jax import lax
from jax.experimental import pallas as pl
from jax.experimental.pallas import tpu as pltpu

F32 = jnp.float32
BF16 = jnp.bfloat16

EPS = 1e-6
HEAD_DIM = 128
LANES = 128
LOG2E = 1.4426950408889634
F32_UNDERFLOW_LOG2 = 150.0
HGRN_CHUNK = 64
VMEM_LIMIT_BYTES = 56 * 1024 * 1024

PROJ_TILE = 1024
DECODE_PROJ_COLS = 512
NORM_ROWS = 256
ROUND_ROWS = 512
ATTN_ROWS = 1024
SB_KEYS = 256
FOX_KEYS = 512
FOX_ROW_CHUNK = 256
HGRN_ROWS = 1024
HGRN_HEADS = 2
CUMSUM_ROWS = 512

_NT = (((1,), (1,)), ((), ()))
_TN = (((0,), (0,)), ((), ()))


def _params(*sem):
    return pltpu.CompilerParams(dimension_semantics=sem, vmem_limit_bytes=VMEM_LIMIT_BYTES)


def _dot(a, b):
    return jnp.dot(a, b, preferred_element_type=F32)


def _dot_nt(a, b):
    return lax.dot_general(a, b, _NT, preferred_element_type=F32)


def _dot_tn(a, b):
    return lax.dot_general(a, b, _TN, preferred_element_type=F32)


def _split3(x):
    hi = x.astype(BF16)
    r1 = x - hi.astype(F32)
    mid = r1.astype(BF16)
    lo = (r1 - mid.astype(F32)).astype(BF16)
    return hi, mid, lo


def _dot_exact_lhs01(a01, x):
    hi, mid, lo = _split3(x)
    return _dot(a01, hi) + _dot(a01, mid) + _dot(a01, lo)


def _dot_exact_lhs01x3(a01x3, x):
    return _dot(a01x3, jnp.concatenate(_split3(x), axis=0))


def _sigmoid_pair(z):
    e = jnp.exp(-jnp.abs(z))
    r = 1.0 / (1.0 + e)
    er = e * r
    pos = z >= 0
    return jnp.where(pos, r, er), jnp.where(pos, er, r)


def _silu(x):
    return x * _sigmoid_pair(x)[0]


def _log_sigmoid(x):
    return jnp.minimum(x, 0.0) - jnp.log(1.0 + jnp.exp(-jnp.abs(x)))


def _rmsnorm_body(x_ref, w_ref, o_ref):
    x = x_ref[...]
    ms = jnp.mean(x * x, axis=-1, keepdims=True)
    o_ref[...] = (x * lax.rsqrt(ms + EPS) * w_ref[...]).astype(o_ref.dtype)


def rmsnorm(x, w, out_dtype, tm):
    m, d = x.shape
    return pl.pallas_call(
        _rmsnorm_body,
        grid=(m // tm,),
        in_specs=[pl.BlockSpec((tm, d), lambda i: (i, 0)), pl.BlockSpec((1, d), lambda i: (0, 0))],
        out_specs=pl.BlockSpec((tm, d), lambda i: (i, 0)),
        out_shape=jax.ShapeDtypeStruct((m, d), out_dtype),
        compiler_params=_params("parallel"),
        name="rmsnorm",
    )(x, w.reshape(1, d))


def _round_body(w_ref, o_ref):
    o_ref[...] = w_ref[...].astype(o_ref.dtype)


def round_weights(w, layer, rows):
    _, k, n = w.shape
    return pl.pallas_call(
        _round_body,
        grid=(k // rows,),
        in_specs=[pl.BlockSpec((None, rows, n), lambda i: (layer, i, 0))],
        out_specs=pl.BlockSpec((rows, n), lambda i: (i, 0)),
        out_shape=jax.ShapeDtypeStruct((k, n), BF16),
        compiler_params=_params("parallel"),
        name="round_weights",
    )(w)


def _round_proj_body(a_ref, w_ref, y_ref, wb_ref):
    wb = w_ref[...].astype(BF16)
    wb_ref[...] = wb
    y_ref[...] = _dot(a_ref[...], wb)


def round_and_proj(a, w, layer, ncols, tn):
    m, kk = a.shape
    return pl.pallas_call(
        _round_proj_body,
        grid=(ncols // tn,),
        in_specs=[pl.BlockSpec((m, kk), lambda j: (0, 0)), pl.BlockSpec((None, kk, tn), lambda j: (layer, 0, j))],
        out_specs=[pl.BlockSpec((m, tn), lambda j: (0, j)), pl.BlockSpec((kk, tn), lambda j: (0, j))],
        out_shape=[jax.ShapeDtypeStruct((m, ncols), F32), jax.ShapeDtypeStruct((kk, ncols), BF16)],
        compiler_params=_params("parallel"),
        name="round_and_proj",
    )(a, w)


def _proj_body(*refs, n_in, has_residual, outs):
    a_refs = refs[:n_in]
    w_refs = refs[n_in:2 * n_in]
    pos = 2 * n_in
    r_ref = refs[pos] if has_residual else None
    o_refs = refs[pos + int(has_residual):]
    acc = None
    for a_ref, w_ref in zip(a_refs, w_refs):
        d = _dot(a_ref[...], w_ref[...])
        acc = d if acc is None else acc + d
    if has_residual:
        acc = r_ref[...] + acc
    for o_ref, (_, scale) in zip(o_refs, outs):
        val = acc if scale == 1.0 else acc * scale
        o_ref[...] = val.astype(o_ref.dtype)


def proj(a_list, w, col0, ncols, tm, tn, outs, residual=None):
    m = a_list[0].shape[0]
    kk = a_list[0].shape[1]
    assert all(a.shape == (m, kk) for a in a_list) and w.shape[0] == kk * len(a_list)
    assert m % tm == 0 and ncols % tn == 0 and col0 % tn == 0
    cb = col0 // tn
    in_specs = [pl.BlockSpec((tm, kk), lambda i, j: (i, 0)) for _ in a_list]
    in_specs += [pl.BlockSpec((kk, tn), functools.partial(lambda i, j, r: (r, cb + j), r=r)) for r in range(len(a_list))]
    args = list(a_list) + [w] * len(a_list)
    if residual is not None:
        in_specs.append(pl.BlockSpec((tm, tn), lambda i, j: (i, j)))
        args.append(residual)
    out_specs = [pl.BlockSpec((tm, tn), lambda i, j: (i, j)) for _ in outs]
    out_shape = [jax.ShapeDtypeStruct((m, ncols), dtype) for dtype, _ in outs]
    return pl.pallas_call(
        functools.partial(_proj_body, n_in=len(a_list), has_residual=residual is not None, outs=tuple(outs)),
        grid=(m // tm, ncols // tn),
        in_specs=in_specs,
        out_specs=out_specs,
        out_shape=out_shape,
        compiler_params=_params("parallel", "parallel"),
        name="proj",
    )(*args)


def _hgrn_maps(c):
    levels = int(np.log2(c))
    assert 2 ** levels == c
    t = np.arange(c)[:, None]
    s = np.arange(c)[None, :]
    mats = [(s <= t), (s > t)]
    for l in range(levels):
        b = 2 ** l
        start = (t // (2 * b)) * (2 * b)
        upper = (t // b) % 2 == 1
        mats.append((upper & (s >= start + b) & (s <= t)) | ((~upper) & (s > t) & (s <= start + b - 1)))
    return np.concatenate(mats, axis=0).astype(np.float32), levels


def _hgrn_body(q_ref, z_ref, v_ref, g_ref, lb_ref, nw_ref, a_ref, s0_ref, o_ref, sout_ref, st_scr,
               *, c, n_chunks, levels, heads):
    tb = pl.program_id(2)

    @pl.when(tb == 0)
    def _():
        for hh in range(heads):
            st_scr[hh] = s0_ref[0, hh].T

    nw = nw_ref[...]
    amat = a_ref[...]
    row = lax.broadcasted_iota(jnp.int32, (c, c), 0)
    col = lax.broadcasted_iota(jnp.int32, (c, c), 1)
    xor = row ^ col

    units = [(slice(ci * c, (ci + 1) * c), slice(hh * HEAD_DIM, (hh + 1) * HEAD_DIM))
             for hh in range(heads) for ci in range(n_chunks)]
    nu = len(units)
    lbs = [lb_ref[:, cols] for _, cols in units]
    qs = [_silu(q_ref[rows, cols]) for rows, cols in units]
    sigs = [_sigmoid_pair(z_ref[rows, cols]) for rows, cols in units]
    gs = [jnp.log(lbs[u] + (1.0 - lbs[u]) * sigs[u][0]) for u in range(nu)]
    ks = [(1.0 - lbs[u]) * sigs[u][1] for u in range(nu)]
    vs = [v_ref[rows, cols] for rows, cols in units]
    vbs = [v.astype(BF16) for v in vs]
    es = [jnp.exp(_dot_exact_lhs01x3(amat, g)) for g in gs]
    atts = [None] * nu
    for l in range(levels - 1, -1, -1):
        for u in range(nu):
            el = es[u][(2 + l) * c:(3 + l) * c]
            al = _dot_nt((qs[u] * el).astype(BF16), (ks[u] * el).astype(BF16))
            atts[u] = al if atts[u] is None else jnp.where(xor < 2 ** (l + 1), al, atts[u])
    atts = [jnp.where(row > col, att, 0.0).astype(BF16) for att in atts]
    o_intras = [_dot(atts[u], vbs[u]) + jnp.sum(qs[u] * ks[u], axis=-1, keepdims=True) * vs[u] for u in range(nu)]
    upds = [_dot_tn(vbs[u], (ks[u] * es[u][c:2 * c]).astype(BF16)) for u in range(nu)]
    qgs = [(qs[u] * es[u][0:c]).astype(BF16) for u in range(nu)]

    for hh in range(heads):
        st = st_scr[hh]
        for ci in range(n_chunks):
            u = hh * n_chunks + ci
            rows, cols = units[u]
            o = _dot_nt(qgs[u], st.astype(BF16)) + o_intras[u]
            st = st * es[u][c - 1:c] + upds[u]
            ms = jnp.mean(o * o, axis=-1, keepdims=True)
            y = o * lax.rsqrt(ms + EPS) * nw
            o_ref[rows, cols] = (y * _silu(g_ref[rows, cols])).astype(o_ref.dtype)
        st_scr[hh] = st

    @pl.when(tb == pl.num_programs(2) - 1)
    def _():
        for hh in range(heads):
            sout_ref[0, hh] = st_scr[hh].T


def hgrn2(qa, fa, ia, ga, lb, a_norm_w, s0, batch, seq, rows_per_step, c, heads_per_step, out_dtype,
          col0=(0, 0, 0, 0)):
    n = qa.shape[0]
    width = lb.shape[0]
    h = width // HEAD_DIM
    assert n == batch * seq and seq % rows_per_step == 0 and rows_per_step % c == 0 and h % heads_per_step == 0
    nb = seq // rows_per_step
    gw = heads_per_step * HEAD_DIM
    amat_np, levels = _hgrn_maps(c)
    amat = jnp.asarray(np.concatenate([amat_np] * 3, axis=1), BF16)
    row_spec = pl.BlockSpec((rows_per_step, gw), lambda b, hh, t: (b * nb + t, hh))
    in_row_specs = [pl.BlockSpec((rows_per_step, gw), functools.partial(lambda b, hh, t, off: (b * nb + t, off + hh), off=c0 // gw))
                    for c0 in col0]
    assert all(c0 % gw == 0 for c0 in col0)
    state_spec = pl.BlockSpec((1, heads_per_step, HEAD_DIM, HEAD_DIM), lambda b, hh, t: (b, hh, 0, 0))
    body = functools.partial(_hgrn_body, c=c, n_chunks=rows_per_step // c, levels=levels, heads=heads_per_step)
    return pl.pallas_call(
        body,
        grid=(batch, h // heads_per_step, nb),
        in_specs=in_row_specs + [
                  pl.BlockSpec((1, gw), lambda b, hh, t: (0, hh)),
                  pl.BlockSpec((1, HEAD_DIM), lambda b, hh, t: (0, 0)),
                  pl.BlockSpec(amat.shape, lambda b, hh, t: (0, 0)),
                  state_spec],
        out_specs=[row_spec, state_spec],
        out_shape=[jax.ShapeDtypeStruct((n, width), out_dtype),
                   jax.ShapeDtypeStruct((batch, h, HEAD_DIM, HEAD_DIM), F32)],
        scratch_shapes=[pltpu.VMEM((heads_per_step, HEAD_DIM, HEAD_DIM), F32)],
        compiler_params=_params("parallel", "parallel", "arbitrary"),
        name="hgrn2",
    )(qa, fa, ia, ga, lb.reshape(1, width), a_norm_w.reshape(1, HEAD_DIM), amat, s0)


def _neg_abs(x):
    return -jnp.abs(x)


def _sb_prompt_body(q_ref, k_ref, v_ref, g_ref, uu_ref, o_ref, run_scr, acc_scr, w_scr, *, tq, tk, rc):
    i = pl.program_id(1)
    nd = tq // tk
    nl = tk // LANES
    q0 = pl.multiple_of(i * tq, tq)
    run_scr[...] = jnp.zeros(run_scr.shape, F32)
    acc_scr[...] = jnp.zeros(acc_scr.shape, F32)

    def apply_pending(slot, k_prev, first_row):
        vt = v_ref[pl.ds(pl.multiple_of(k_prev, tk), tk), :]
        for r0 in range(first_row, tq, rc):
            rows = slice(r0, min(r0 + rc, tq))
            acc_scr[rows, :] = acc_scr[rows, :] + _dot(w_scr[slot, rows, :], vt)

    def score(slot, k0, first_row, masked):
        kt = k_ref[pl.ds(pl.multiple_of(k0, tk), tk), :]
        for r0 in range(first_row, tq, rc):
            r1 = min(r0 + rc, tq)
            rows = slice(r0, r1)
            z = _dot_nt(q_ref[rows, :], kt)
            if masked:
                qpos = lax.broadcasted_iota(jnp.int32, (r1 - r0, LANES), 0) + (q0 + r0)
                kpos = lax.broadcasted_iota(jnp.int32, (r1 - r0, LANES), 1) + k0
            zs, sps, his, los, valids = [], [], [], [], []
            rowsum = None
            for c in range(nl):
                zc = z[:, c * LANES:(c + 1) * LANES]
                sp = jnp.maximum(zc, 0.0) + jnp.log2(1.0 + jnp.exp2(_neg_abs(zc)))
                if masked:
                    valid = (kpos + c * LANES) < qpos
                    sp = jnp.where(valid, sp, 0.0)
                    valids.append(valid)
                hi = sp.astype(BF16)
                lo = (sp - hi.astype(F32)).astype(BF16)
                zs.append(zc); sps.append(sp); his.append(hi); los.append(lo)
                rowsum = sp if rowsum is None else rowsum + sp
            tail = _dot(jnp.concatenate(his + los, axis=1), uu_ref[...])
            run = run_scr[rows, :]
            for c in range(nl):
                w = jnp.exp2(zs[c] - (sps[c] + tail[:, c * LANES:(c + 1) * LANES] + run))
                if masked:
                    w = jnp.where(valids[c], w, 0.0)
                w_scr[slot, rows, c * LANES:(c + 1) * LANES] = w.astype(BF16)
            run_scr[rows, :] = run + jnp.sum(rowsum, axis=-1, keepdims=True)

    assert nd % 2 == 0
    for d in range(nd - 1, 0, -2):
        score(0, q0 + d * tk, d * tk, True)
        if d + 1 < nd:
            apply_pending(1, q0 + (d + 1) * tk, (d + 1) * tk)
        score(1, q0 + (d - 1) * tk, (d - 1) * tk, True)
        apply_pending(0, q0 + d * tk, d * tk)

    def before(carry):
        jj, kp, _ = carry
        for u in range(unroll):
            k_a = q0 - (2 * (unroll * jj + u) + 1) * tk
            score(0, k_a, 0, False)
            apply_pending(1, kp, 0)
            score(1, k_a - tk, 0, False)
            apply_pending(0, k_a, 0)
            kp = k_a - tk
        return jj + 1, kp, jnp.min(run_scr[...]) < F32_UNDERFLOW_LOG2

    unroll = 1
    assert nd % (2 * unroll) == 0
    n_trips = (i * nd) // (2 * unroll)
    _, k_pending, _ = lax.while_loop(lambda carry: (carry[0] < n_trips) & carry[2], before,
                                     (jnp.int32(0), q0, jnp.bool_(True)))
    apply_pending(1, k_pending, 0)
    o_ref[...] = (acc_scr[...] * _silu(g_ref[...])).astype(o_ref.dtype)


def _tail_matrix(n):
    sp = np.arange(n)[:, None]
    s = np.arange(n)[None, :]
    return (sp > s).astype(np.float32)


def sb_prompt(q, k, v, gate, tq, tk, rc, out_dtype):
    t, width = q.shape
    h = width // HEAD_DIM
    u = _tail_matrix(tk)
    uu = jnp.asarray(np.concatenate([u, u], axis=0), BF16)
    q_spec = pl.BlockSpec((tq, HEAD_DIM), lambda hh, i: (i, hh))
    kv_spec = pl.BlockSpec((t, HEAD_DIM), lambda hh, i: (0, hh))
    return pl.pallas_call(
        functools.partial(_sb_prompt_body, tq=tq, tk=tk, rc=rc),
        grid=(h, t // tq),
        in_specs=[q_spec, kv_spec, kv_spec, q_spec, pl.BlockSpec(uu.shape, lambda hh, i: (0, 0))],
        out_specs=q_spec,
        out_shape=jax.ShapeDtypeStruct((t, width), out_dtype),
        scratch_shapes=[pltpu.VMEM((tq, LANES), F32), pltpu.VMEM((tq, HEAD_DIM), F32),
                        pltpu.VMEM((2, tq, tk), BF16)],
        compiler_params=_params("parallel", "arbitrary"),
        name="sb_prompt",
    )(q, k, v, gate, uu)


def _logf_cumsum_body(fl_ref, b_ref, tri_ref, lf_ref, f2_ref, carry_scr):
    @pl.when(pl.program_id(0) == 0)
    def _():
        carry_scr[...] = jnp.zeros_like(carry_scr)

    lf = _log_sigmoid(fl_ref[...] + b_ref[...])
    lf_ref[...] = lf
    f = carry_scr[...] + _dot_exact_lhs01(tri_ref[...], lf)
    f2_ref[...] = f * LOG2E
    carry_scr[...] = f[f.shape[0] - 1:, :]


def logf_cumsum(fl, b_forget, blk):
    t, h = fl.shape
    tri = jnp.asarray(np.tril(np.ones((blk, blk), np.float32)), BF16)
    spec = pl.BlockSpec((blk, h), lambda i: (i, 0))
    return pl.pallas_call(
        _logf_cumsum_body,
        grid=(t // blk,),
        in_specs=[spec, pl.BlockSpec((1, h), lambda i: (0, 0)), pl.BlockSpec((blk, blk), lambda i: (0, 0))],
        out_specs=[spec, spec],
        out_shape=[jax.ShapeDtypeStruct((t, h), F32), jax.ShapeDtypeStruct((t, h), F32)],
        scratch_shapes=[pltpu.VMEM((1, h), F32)],
        compiler_params=_params("arbitrary"),
        name="logf_cumsum",
    )(fl, b_forget.reshape(1, h), tri)


def _logf_body(fl_ref, b_ref, lf_ref):
    lf_ref[...] = _log_sigmoid(fl_ref[...] + b_ref[...])


def logf_only(fl, b_forget):
    t, h = fl.shape
    return pl.pallas_call(
        _logf_body,
        grid=(1,),
        in_specs=[pl.BlockSpec((t, h), lambda i: (0, 0)), pl.BlockSpec((1, h), lambda i: (0, 0))],
        out_specs=pl.BlockSpec((t, h), lambda i: (0, 0)),
        out_shape=jax.ShapeDtypeStruct((t, h), F32),
        name="logf",
    )(fl, b_forget.reshape(1, h))


_BIAS_PIECES = 3


def _bias_selectors(h):
    sel = np.zeros((2, h, _BIAS_PIECES, h, LANES), np.float32)
    for hh in range(h):
        for p in range(_BIAS_PIECES):
            sel[0, hh, p, hh, p] = 1.0
            sel[1, hh, p, hh, _BIAS_PIECES + p] = -1.0
    return sel


def _bias_columns(f, sel_ref, query_side):
    out = None
    for p, piece in enumerate(_split3(f)):
        d = _dot(piece, sel_ref[0, p])
        out = d if out is None else out + d
    lane = lax.broadcasted_iota(jnp.int32, out.shape, 1)
    ones_at = (lane >= _BIAS_PIECES) & (lane < 2 * _BIAS_PIECES) if query_side else lane < _BIAS_PIECES
    return jnp.where(ones_at, 1.0, out).astype(BF16)


def _fox_prompt_body(q_ref, k_ref, v_ref, g_ref, f_ref, selq_ref, selk_ref, o_ref,
                     kx_scr, vx_scr, m_scr, acc_scr, p_scr, alpha_scr, kmax_scr, ub_scr, *, tq, tk, rc):
    i = pl.program_id(1)
    nd = tq // tk
    nl = tk // LANES
    t_all = k_ref.shape[0]

    head = pl.program_id(0)

    def own_column(f_rows):
        lane = lax.broadcasted_iota(jnp.int32, f_rows.shape, 1)
        return jnp.sum(jnp.where(lane == head, f_rows, 0.0), axis=-1, keepdims=True)

    @pl.when(i == 0)
    def _():
        kb = k_ref[...].astype(BF16)
        kx_scr[:, 0:HEAD_DIM] = kb
        kx_scr[:, HEAD_DIM:] = _bias_columns(f_ref[...], selk_ref, False)
        vx_scr[:, 0:HEAD_DIM] = v_ref[...].astype(BF16)
        vx_scr[:, HEAD_DIM:] = jnp.ones((t_all, LANES), BF16)
        kf = kb.astype(F32)
        knorm2 = jnp.max(jnp.sum(kf * kf, axis=-1, keepdims=True), axis=0, keepdims=True)
        kmax_scr[...] = jnp.broadcast_to(jnp.sqrt(knorm2), kmax_scr.shape)

    q0 = pl.multiple_of(i * tq, tq)
    qx = jnp.concatenate([q_ref[...], _bias_columns(f_ref[pl.ds(q0, tq), :], selq_ref, True)], axis=1)
    m_scr[...] = jnp.full(m_scr.shape, -jnp.inf, F32)
    acc_scr[...] = jnp.zeros(acc_scr.shape, F32)
    qf = q_ref[...].astype(F32)
    qnorm = jnp.sqrt(jnp.sum(qf * qf, axis=-1, keepdims=True))
    ub_scr[...] = qnorm * kmax_scr[...] * (1.0 + 2.0 ** -10) + own_column(f_ref[pl.ds(q0, tq), :])

    def apply_pending(slot, k_prev, first_row):
        vxt = vx_scr[pl.ds(pl.multiple_of(k_prev, tk), tk), :]
        for r0 in range(first_row, tq, rc):
            rows = slice(r0, r0 + rc)
            pv = _dot(p_scr[slot, rows, :], vxt)
            alpha = alpha_scr[slot, rows, :]
            acc_scr[rows, 0:HEAD_DIM] = alpha * acc_scr[rows, 0:HEAD_DIM] + pv[:, 0:HEAD_DIM]
            acc_scr[rows, HEAD_DIM:] = alpha * acc_scr[rows, HEAD_DIM:] + pv[:, HEAD_DIM:]

    def score(slot, k0, first_row, masked):
        kxt = kx_scr[pl.ds(pl.multiple_of(k0, tk), tk), :]
        for r0 in range(first_row, tq, rc):
            rows = slice(r0, r0 + rc)
            s = _dot_nt(qx[r0:r0 + rc, :], kxt)
            sb = [s[:, c * LANES:(c + 1) * LANES] for c in range(nl)]
            if masked:
                qpos = lax.broadcasted_iota(jnp.int32, (rc, LANES), 0) + (q0 + r0)
                kpos = lax.broadcasted_iota(jnp.int32, (rc, LANES), 1) + k0
                sb = [jnp.where((kpos + c * LANES) <= qpos, sb[c], -jnp.inf) for c in range(nl)]
            mx = sb[0]
            for c in range(1, nl):
                mx = jnp.maximum(mx, sb[c])
            m_old = m_scr[rows, :]
            m_new = jnp.maximum(m_old, jnp.max(mx, axis=-1, keepdims=True))
            alpha_scr[slot, rows, :] = jnp.exp2(m_old - m_new)
            for c in range(nl):
                p_scr[slot, rows, c * LANES:(c + 1) * LANES] = jnp.exp2(sb[c] - m_new).astype(BF16)
            m_scr[rows, :] = m_new

    assert nd % 2 == 0 and tk % rc == 0
    for d in range(0, nd, 2):
        score(0, q0 + d * tk, d * tk, True)
        if d > 0:
            apply_pending(1, q0 + (d - 1) * tk, (d - 1) * tk)
        score(1, q0 + (d + 1) * tk, (d + 1) * tk, True)
        apply_pending(0, q0 + d * tk, d * tk)

    p_scr[1, 0:(nd - 1) * tk, :] = jnp.zeros(((nd - 1) * tk, tk), BF16)
    alpha_scr[1, 0:(nd - 1) * tk, :] = jnp.ones(((nd - 1) * tk, LANES), F32)

    def pair(k_a, kp):
        score(0, k_a, 0, False)
        apply_pending(1, kp, 0)
        score(1, k_a - tk, 0, False)
        apply_pending(0, k_a, 0)
        return k_a - tk

    def visible(k_hi):
        f_hi = own_column(f_ref[pl.ds(jnp.maximum(k_hi, 0), 1), :])
        return jnp.max(ub_scr[...] - f_hi - m_scr[...]) > -(F32_UNDERFLOW_LOG2 + 2.0)

    def trip(carry):
        jj, kp, _ = carry
        kp = pair(q0 - (2 * jj + 1) * tk, kp)
        return jj + 1, kp, visible(kp - 1)

    n_pairs = (i * nd) // 2
    _, k_pending, _ = lax.while_loop(lambda carry: (carry[0] < n_pairs) & carry[2], trip,
                                     (jnp.int32(0), q0 + (nd - 1) * tk, visible(q0 - 1)))
    apply_pending(1, k_pending, 0)
    o_ref[...] = (acc_scr[:, 0:HEAD_DIM] / acc_scr[:, HEAD_DIM:] * _silu(g_ref[...])).astype(o_ref.dtype)


def fox_prompt(q, k, v, gate, f2, tq, tk, rc, out_dtype):
    t, width = q.shape
    h = width // HEAD_DIM
    sel = jnp.asarray(_bias_selectors(h), BF16)
    q_spec = pl.BlockSpec((tq, HEAD_DIM), lambda hh, i: (i, hh))
    kv_spec = pl.BlockSpec((t, HEAD_DIM), lambda hh, i: (0, hh))
    sel_spec = pl.BlockSpec((1, _BIAS_PIECES, h, LANES), lambda hh, i: (hh, 0, 0, 0))
    return pl.pallas_call(
        functools.partial(_fox_prompt_body, tq=tq, tk=tk, rc=rc),
        grid=(h, t // tq),
        in_specs=[q_spec, kv_spec, kv_spec, q_spec, pl.BlockSpec((t, h), lambda hh, i: (0, 0)),
                  sel_spec, sel_spec],
        out_specs=q_spec,
        out_shape=jax.ShapeDtypeStruct((t, width), out_dtype),
        scratch_shapes=[pltpu.VMEM((t, 2 * HEAD_DIM), BF16), pltpu.VMEM((t, 2 * HEAD_DIM), BF16),
                        pltpu.VMEM((tq, LANES), F32), pltpu.VMEM((tq, 2 * HEAD_DIM), F32),
                        pltpu.VMEM((2, tq, tk), BF16), pltpu.VMEM((2, tq, LANES), F32),
                        pltpu.VMEM((1, LANES), F32), pltpu.VMEM((tq, LANES), F32)],
        compiler_params=_params("parallel", "arbitrary"),
        name="fox_prompt",
    )(q, k, v, gate, f2, sel[0], sel[1])


HEADS_PER_GROUP = 8


def _expand_queries(q, tq):
    gw = q.shape[1]
    rep = jnp.concatenate([q] * HEADS_PER_GROUP, axis=0)
    r = lax.broadcasted_iota(jnp.int32, (HEADS_PER_GROUP * tq, gw), 0) // tq
    cidx = lax.broadcasted_iota(jnp.int32, (HEADS_PER_GROUP * tq, gw), 1) // HEAD_DIM
    return jnp.where(r == cidx, rep, 0.0)


def _collect_heads(full, tq):
    return jnp.concatenate(
        [full[hh * tq:(hh + 1) * tq, hh * HEAD_DIM:(hh + 1) * HEAD_DIM] for hh in range(HEADS_PER_GROUP)],
        axis=1)


def _gather_keys(cache_ref, new_ref, scr, past, tk):
    g = HEADS_PER_GROUP
    x = cache_ref[0].reshape(past // g, g, g, HEAD_DIM)
    x = jnp.swapaxes(x, 1, 2)
    for hh in range(g):
        scr[0:past, hh * HEAD_DIM:(hh + 1) * HEAD_DIM] = x[:, hh].reshape(past, HEAD_DIM).astype(BF16)
    scr[past:tk, :] = new_ref[...].astype(BF16)


def _row_blocks(n):
    return [(r0, min(r0 + LANES, n)) for r0 in range(0, n, LANES)]


def _cumsum_rows(pieces, tri_ref):
    n = pieces[0].shape[0]
    out, carry = [], None
    for r0, r1 in _row_blocks(n):
        tri = tri_ref[0:r1 - r0, 0:r1 - r0]
        local = None
        for piece in pieces:
            d = _dot(tri, piece[r0:r1])
            local = d if local is None else local + d
        if carry is not None:
            local = local + carry
        carry = local[r1 - r0 - 1:r1 - r0]
        out.append(local)
    return jnp.concatenate(out, axis=0)


def _tailsum_rows(pieces, tri_ref):
    n = pieces[0].shape[0]
    out, carry = [], None
    for r0, r1 in reversed(_row_blocks(n)):
        tri = tri_ref[0:r1 - r0, 0:r1 - r0]
        local, total = None, None
        for piece in pieces:
            blk = piece[r0:r1]
            d = _dot_tn(tri, blk)
            local = d if local is None else local + d
            t = blk.astype(F32)
            total = t if total is None else total + t
        local = local - total
        if carry is not None:
            local = local + carry
        carry = local[0:1] + total[0:1]
        out.append(local)
    return jnp.concatenate(out[::-1], axis=0)


def _sb_decode_body(q_ref, kn_ref, vn_ref, kc_ref, vc_ref, g_ref, tri_ref, o_ref, k_scr, v_scr, *, past, tq):
    tk = past + tq
    _gather_keys(kc_ref, kn_ref, k_scr, past, tk)
    _gather_keys(vc_ref, vn_ref, v_scr, past, tk)

    lanes = HEADS_PER_GROUP * tq
    qx = _expand_queries(q_ref[...] * (HEAD_DIM ** -0.5), tq).astype(BF16)
    z = _dot_nt(k_scr[...], qx)
    kpos = lax.broadcasted_iota(jnp.int32, (tk, lanes), 0)
    qpos = past + lax.broadcasted_iota(jnp.int32, (tk, lanes), 1) % tq
    valid = kpos < qpos
    ls_neg = -(jnp.maximum(z, 0.0) + jnp.log(1.0 + jnp.exp(-jnp.abs(z))))
    lm = jnp.where(valid, ls_neg, 0.0)
    hi = lm.astype(BF16)
    lo = (lm - hi.astype(F32)).astype(BF16)
    tail = _tailsum_rows((hi, lo), tri_ref)
    w = jnp.where(valid, jnp.exp(z + ls_neg + tail), 0.0)
    full = _dot_tn(w.astype(BF16), v_scr[...])
    o_ref[...] = (_collect_heads(full, tq) * _silu(g_ref[...])).astype(o_ref.dtype)


def _cache_spec(layer, past):
    return pl.BlockSpec((None, 1, past, HEADS_PER_GROUP, HEAD_DIM), lambda b, g: (layer, b, 0, g, 0))


def _decode_row_specs(tq, gw, col0):
    assert all(c0 % gw == 0 for c0 in col0)
    return [pl.BlockSpec((tq, gw), functools.partial(lambda b, g, off: (b, off + g), off=c0 // gw)) for c0 in col0]


def sb_decode(q, k_new, v_new, k_cache, v_cache, layer, gate, batch, tq, out_dtype, col0=(0, 0, 0, 0)):
    n = q.shape[0]
    width = k_cache.shape[3] * HEAD_DIM
    past = k_cache.shape[2]
    tk = past + tq
    gw = HEADS_PER_GROUP * HEAD_DIM
    groups = width // gw
    tri = jnp.asarray(np.tril(np.ones((LANES, LANES), np.float32)), BF16)
    row_spec = pl.BlockSpec((tq, gw), lambda b, g: (b, g))
    rs = _decode_row_specs(tq, gw, col0)
    cache_spec = _cache_spec(layer, past)
    return pl.pallas_call(
        functools.partial(_sb_decode_body, past=past, tq=tq),
        grid=(batch, groups),
        in_specs=[rs[0], rs[1], rs[2], cache_spec, cache_spec, rs[3],
                  pl.BlockSpec((LANES, LANES), lambda b, g: (0, 0))],
        out_specs=row_spec,
        out_shape=jax.ShapeDtypeStruct((n, width), out_dtype),
        scratch_shapes=[pltpu.VMEM((tk, gw), BF16), pltpu.VMEM((tk, gw), BF16)],
        compiler_params=_params("parallel", "parallel"),
        name="sb_decode",
    )(q, k_new, v_new, k_cache, v_cache, gate, tri)


def _fox_decode_body(q_ref, kn_ref, vn_ref, kc_ref, vc_ref, g_ref, lfn_ref, lfc_ref, ex_ref, tri_ref, o_ref,
                     k_scr, v_scr, lf_scr, *, past, tq):
    tk = past + tq
    _gather_keys(kc_ref, kn_ref, k_scr, past, tk)
    _gather_keys(vc_ref, vn_ref, v_scr, past, tk)
    lf_scr[0:past, :] = lfc_ref[0]
    lf_scr[past:tk, :] = lfn_ref[...]

    lanes = HEADS_PER_GROUP * tq
    hi, mid, lo = _split3(lf_scr[...])
    ex = ex_ref[0]
    lfx = _dot(hi, ex) + _dot(mid, ex) + _dot(lo, ex)
    f_key = _cumsum_rows(_split3(lfx), tri_ref)
    kpos = lax.broadcasted_iota(jnp.int32, (tk, lanes), 0)
    qpos = past + lax.broadcasted_iota(jnp.int32, (tk, lanes), 1) % tq
    f_query = jnp.sum(jnp.where(kpos == qpos, f_key, 0.0), axis=0, keepdims=True)

    qx = _expand_queries(q_ref[...] * (HEAD_DIM ** -0.5), tq).astype(BF16)
    s = _dot_nt(k_scr[...], qx) + (f_query - f_key)
    s = jnp.where(kpos <= qpos, s, -jnp.inf)
    p = jnp.exp(s - jnp.max(s, axis=0, keepdims=True))
    p = p / jnp.sum(p, axis=0, keepdims=True)
    full = _dot_tn(p.astype(BF16), v_scr[...])
    o_ref[...] = (_collect_heads(full, tq) * _silu(g_ref[...])).astype(o_ref.dtype)


def fox_decode(q, k_new, v_new, k_cache, v_cache, layer, gate, lf_new, lf_cache, batch, tq, out_dtype,
               col0=(0, 0, 0, 0)):
    n = q.shape[0]
    h = k_cache.shape[3]
    width = h * HEAD_DIM
    past = k_cache.shape[2]
    tk = past + tq
    gw = HEADS_PER_GROUP * HEAD_DIM
    groups = width // gw
    lanes = HEADS_PER_GROUP * tq
    tri = jnp.asarray(np.tril(np.ones((LANES, LANES), np.float32)), BF16)
    head_of_lane = np.arange(lanes)[None, None, :] // tq + HEADS_PER_GROUP * np.arange(groups)[:, None, None]
    expand = jnp.asarray((np.arange(h)[None, :, None] == head_of_lane).astype(np.float32), BF16)
    row_spec = pl.BlockSpec((tq, gw), lambda b, g: (b, g))
    rs = _decode_row_specs(tq, gw, col0)
    cache_spec = _cache_spec(layer, past)
    return pl.pallas_call(
        functools.partial(_fox_decode_body, past=past, tq=tq),
        grid=(batch, groups),
        in_specs=[rs[0], rs[1], rs[2], cache_spec, cache_spec, rs[3],
                  pl.BlockSpec((tq, h), lambda b, g: (b, 0)),
                  pl.BlockSpec((None, 1, past, h), lambda b, g: (layer, b, 0, 0)),
                  pl.BlockSpec((1, h, lanes), lambda b, g: (g, 0, 0)),
                  pl.BlockSpec((LANES, LANES), lambda b, g: (0, 0))],
        out_specs=row_spec,
        out_shape=jax.ShapeDtypeStruct((n, width), out_dtype),
        scratch_shapes=[pltpu.VMEM((tk, gw), BF16), pltpu.VMEM((tk, gw), BF16), pltpu.VMEM((tk, h), F32)],
        compiler_params=_params("parallel", "parallel"),
        name="fox_decode",
    )(q, k_new, v_new, k_cache, v_cache, gate, lf_new, lf_cache, expand, tri)


QK_SCALE_LOG2 = LOG2E * HEAD_DIM ** -0.5
FLAT32 = (F32, 1.0)
FLAT16 = (BF16, 1.0)
QUERY16 = (BF16, QK_SCALE_LOG2)


def _even_prompt(x, norm_w, w_in, w_out, lb, a_norm_w, s0, seq):
    n, d = x.shape
    half = d // 2
    hn = rmsnorm(x, norm_w, BF16, NORM_ROWS)

    def p(group, *outs):
        return proj([hn], w_in, group * half, half, PROJ_TILE, PROJ_TILE, outs)

    (rec,) = proj([hn], w_in, 0, 4 * half, PROJ_TILE, PROJ_TILE, (FLAT32,))
    (qb,), (gb,) = p(4, QUERY16), p(7, FLAT32)
    kb_leaf, kb = p(5, FLAT32, FLAT16)
    vb_leaf, vb = p(6, FLAT32, FLAT16)
    oa, s_new = hgrn2(rec, rec, rec, rec, lb, a_norm_w, s0, 1, seq, HGRN_ROWS, HGRN_CHUNK, HGRN_HEADS, BF16,
                      col0=(0, half, 2 * half, 3 * half))
    ob = sb_prompt(qb, kb, vb, gb, ATTN_ROWS, SB_KEYS, ATTN_ROWS, BF16)
    (y,) = proj([oa, ob], w_out, 0, d, PROJ_TILE, PROJ_TILE, (FLAT32,), residual=x)
    return y, s_new, kb_leaf, vb_leaf


def _even_decode(x, norm_w, w_in_f32, layer, w_out, lb, a_norm_w, s0, k_cache, v_cache, batch, seq):
    n, d = x.shape
    half = d // 2
    hn = rmsnorm(x, norm_w, BF16, n)
    cols, w_in = round_and_proj(hn, w_in_f32, layer, 8 * half, DECODE_PROJ_COLS)
    oa, s_new = hgrn2(cols, cols, cols, cols, lb, a_norm_w, s0, batch, seq, seq, seq, half // HEAD_DIM, BF16,
                      col0=(0, half, 2 * half, 3 * half))
    ob = sb_decode(cols, cols, cols, k_cache, v_cache, layer, cols, batch, seq, BF16,
                   col0=(4 * half, 5 * half, 6 * half, 7 * half))
    (y,) = proj([oa, ob], w_out, 0, d, n, DECODE_PROJ_COLS, (FLAT32,), residual=x)
    return y, s_new, cols[:, 5 * half:6 * half], cols[:, 6 * half:7 * half], w_in


def _odd_prompt(x, norm_w, w_in, w_fl, b_forget, w_out, seq):
    n, d = x.shape
    heads = d // HEAD_DIM
    hn = rmsnorm(x, norm_w, BF16, NORM_ROWS)

    def p(group, *outs):
        return proj([hn], w_in, group * d, d, PROJ_TILE, PROJ_TILE, outs)

    (q,), (gate,) = p(0, QUERY16), p(3, FLAT32)
    (fl,) = proj([hn], w_fl, 0, heads, PROJ_TILE, heads, (FLAT32,))
    logf, f2 = logf_cumsum(fl, b_forget, CUMSUM_ROWS)
    (k,), (v,) = p(1, FLAT32), p(2, FLAT32)
    o = fox_prompt(q, k, v, gate, f2, ATTN_ROWS, FOX_KEYS, FOX_ROW_CHUNK, BF16)
    (y,) = proj([o], w_out, 0, d, PROJ_TILE, PROJ_TILE, (FLAT32,), residual=x)
    return y, k, v, logf


def _odd_decode(x, norm_w, w_in, w_fl, b_forget, w_out, k_cache, v_cache, lf_cache, layer, batch, seq):
    n, d = x.shape
    heads = d // HEAD_DIM
    hn = rmsnorm(x, norm_w, BF16, n)
    (cols,) = proj([hn], w_in, 0, 4 * d, n, DECODE_PROJ_COLS, (FLAT32,))
    (fl,) = proj([hn], w_fl, 0, heads, n, heads, (FLAT32,))
    logf = logf_only(fl, b_forget)
    o = fox_decode(cols, cols, cols, k_cache, v_cache, layer, cols, logf, lf_cache, batch, seq, BF16,
                   col0=(0, d, 2 * d, 3 * d))
    (y,) = proj([o], w_out, 0, d, n, DECODE_PROJ_COLS, (FLAT32,), residual=x)
    return y, cols[:, d:2 * d], cols[:, 2 * d:3 * d], logf


def kernel(x_prompt, x_sample, state_a_hgrn, cache_b_k, cache_b_v, cache_c_k, cache_c_v, cache_c_logf,
           norm_w, final_norm_w, w_in_even, w_out_even, lb_logits, a_norm_w, w_in_odd, b_forget, w_out_odd):
    bp, tp, d = x_prompt.shape
    bs, ts, _ = x_sample.shape
    assert bp == 1
    depth = norm_w.shape[0]
    n_even = w_in_even.shape[0]
    lb_all = jnp.cumsum(jax.nn.softmax(lb_logits.astype(F32), axis=0), axis=0)[:n_even]

    hp = x_prompt.reshape(bp * tp, d)
    hs = x_sample.reshape(bs * ts, d)
    outs = {name: [] for name in ("sa_p", "sa_s", "bk_p", "bv_p", "bk_s", "bv_s",
                                  "ck_p", "cv_p", "cf_p", "ck_s", "cv_s", "cf_s")}
    for layer in range(depth):
        j = layer // 2
        if layer % 2 == 0:
            a_heads = state_a_hgrn.shape[2]
            b_heads = cache_b_k.shape[3]
            zeros = jnp.zeros((bp, a_heads) + state_a_hgrn.shape[3:], F32)
            w_out = round_weights(w_out_even, j, ROUND_ROWS)
            hs, ss, ksm, vsm, w_in = _even_decode(hs, norm_w[layer], w_in_even, j, w_out, lb_all[j], a_norm_w[j],
                                                  state_a_hgrn[j], cache_b_k, cache_b_v, bs, ts)
            hp, sp, kp, vp = _even_prompt(hp, norm_w[layer], w_in, w_out, lb_all[j], a_norm_w[j], zeros, tp)
            outs["sa_p"].append(sp); outs["sa_s"].append(ss)
            outs["bk_p"].append(kp.reshape(bp, tp, b_heads, HEAD_DIM))
            outs["bv_p"].append(vp.reshape(bp, tp, b_heads, HEAD_DIM))
            outs["bk_s"].append(ksm.reshape(bs, ts, b_heads, HEAD_DIM))
            outs["bv_s"].append(vsm.reshape(bs, ts, b_heads, HEAD_DIM))
        else:
            c_heads = cache_c_k.shape[3]
            w_in = w_in_odd[j].astype(BF16)
            w_fl = w_in[:, 4 * c_heads * HEAD_DIM:]
            w_out = round_weights(w_out_odd, j, ROUND_ROWS)
            hp, kp, vp, fp = _odd_prompt(hp, norm_w[layer], w_in, w_fl, b_forget[j], w_out, tp)
            hs, ksm, vsm, fsm = _odd_decode(hs, norm_w[layer], w_in, w_fl, b_forget[j], w_out,
                                            cache_c_k, cache_c_v, cache_c_logf, j, bs, ts)
            outs["ck_p"].append(kp.reshape(bp, tp, c_heads, HEAD_DIM))
            outs["cv_p"].append(vp.reshape(bp, tp, c_heads, HEAD_DIM))
            outs["cf_p"].append(fp.reshape(bp, tp, c_heads))
            outs["ck_s"].append(ksm.reshape(bs, ts, c_heads, HEAD_DIM))
            outs["cv_s"].append(vsm.reshape(bs, ts, c_heads, HEAD_DIM))
            outs["cf_s"].append(fsm.reshape(bs, ts, c_heads))
    y_prompt = rmsnorm(hp, final_norm_w, F32, NORM_ROWS).reshape(bp, tp, d)
    y_sample = rmsnorm(hs, final_norm_w, F32, bs * ts).reshape(bs, ts, d)
    st = {k: jnp.stack(v) for k, v in outs.items()}
    return (y_prompt, y_sample, st["sa_p"], st["sa_s"], st["bk_p"], st["bv_p"], st["bk_s"], st["bv_s"],
            st["ck_p"], st["cv_p"], st["cf_p"], st["ck_s"], st["cv_s"], st["cf_s"])
```

```python
import functools

import numpy as np
import jax
import jax.numpy as jnp
from jax import lax
from jax.experimental import pallas as pl
from jax.experimental.pallas import tpu as pltpu

F32 = jnp.float32
BF16 = jnp.bfloat16

EPS = 1e-6
HEAD_DIM = 128
LANES = 128
LOG2E = 1.4426950408889634
F32_UNDERFLOW_LOG2 = 150.0
HGRN_CHUNK = 64
VMEM_LIMIT_BYTES = 56 * 1024 * 1024

PROJ_TILE = 1024
DECODE_PROJ_COLS = 512
NORM_ROWS = 256
ROUND_ROWS = 512
ATTN_ROWS = 1024
SB_KEYS = 256
FOX_KEYS = 512
FOX_ROW_CHUNK = 256
HGRN_ROWS = 1024
HGRN_HEADS = 2
CUMSUM_ROWS = 512

_NT = (((1,), (1,)), ((), ()))
_TN = (((0,), (0,)), ((), ()))


def _params(*sem):
    return pltpu.CompilerParams(dimension_semantics=sem, vmem_limit_bytes=VMEM_LIMIT_BYTES)


def _dot(a, b):
    return jnp.dot(a, b, preferred_element_type=F32)


def _dot_nt(a, b):
    return lax.dot_general(a, b, _NT, preferred_element_type=F32)


def _dot_tn(a, b):
    return lax.dot_general(a, b, _TN, preferred_element_type=F32)


def _split3(x):
    hi = x.astype(BF16)
    r1 = x - hi.astype(F32)
    mid = r1.astype(BF16)
    lo = (r1 - mid.astype(F32)).astype(BF16)
    return hi, mid, lo


def _dot_exact_lhs01(a01, x):
    hi, mid, lo = _split3(x)
    return _dot(a01, hi) + _dot(a01, mid) + _dot(a01, lo)


def _dot_exact_lhs01x3(a01x3, x):
    return _dot(a01x3, jnp.concatenate(_split3(x), axis=0))


def _sigmoid_pair(z):
    e = jnp.exp(-jnp.abs(z))
    r = 1.0 / (1.0 + e)
    er = e * r
    pos = z >= 0
    return jnp.where(pos, r, er), jnp.where(pos, er, r)


def _silu(x):
    return x * _sigmoid_pair(x)[0]


def _log_sigmoid(x):
    return jnp.minimum(x, 0.0) - jnp.log(1.0 + jnp.exp(-jnp.abs(x)))


def _rmsnorm_body(x_ref, w_ref, o_ref):
    x = x_ref[...]
    ms = jnp.mean(x * x, axis=-1, keepdims=True)
    o_ref[...] = (x * lax.rsqrt(ms + EPS) * w_ref[...]).astype(o_ref.dtype)


def rmsnorm(x, w, out_dtype, tm):
    m, d = x.shape
    return pl.pallas_call(
        _rmsnorm_body,
        grid=(m // tm,),
        in_specs=[pl.BlockSpec((tm, d), lambda i: (i, 0)), pl.BlockSpec((1, d), lambda i: (0, 0))],
        out_specs=pl.BlockSpec((tm, d), lambda i: (i, 0)),
        out_shape=jax.ShapeDtypeStruct((m, d), out_dtype),
        compiler_params=_params("parallel"),
        name="rmsnorm",
    )(x, w.reshape(1, d))


def _round_body(w_ref, o_ref):
    o_ref[...] = w_ref[...].astype(o_ref.dtype)


def round_weights(w, layer, rows):
    _, k, n = w.shape
    return pl.pallas_call(
        _round_body,
        grid=(k // rows,),
        in_specs=[pl.BlockSpec((None, rows, n), lambda i: (layer, i, 0))],
        out_specs=pl.BlockSpec((rows, n), lambda i: (i, 0)),
        out_shape=jax.ShapeDtypeStruct((k, n), BF16),
        compiler_params=_params("parallel"),
        name="round_weights",
    )(w)


def _round_proj_body(a_ref, w_ref, y_ref, wb_ref):
    wb = w_ref[...].astype(BF16)
    wb_ref[...] = wb
    y_ref[...] = _dot(a_ref[...], wb)


def round_and_proj(a, w, layer, ncols, tn):
    m, kk = a.shape
    return pl.pallas_call(
        _round_proj_body,
        grid=(ncols // tn,),
        in_specs=[pl.BlockSpec((m, kk), lambda j: (0, 0)), pl.BlockSpec((None, kk, tn), lambda j: (layer, 0, j))],
        out_specs=[pl.BlockSpec((m, tn), lambda j: (0, j)), pl.BlockSpec((kk, tn), lambda j: (0, j))],
        out_shape=[jax.ShapeDtypeStruct((m, ncols), F32), jax.ShapeDtypeStruct((kk, ncols), BF16)],
        compiler_params=_params("parallel"),
        name="round_and_proj",
    )(a, w)


def _proj_body(*refs, n_in, has_residual, outs):
    a_refs = refs[:n_in]
    w_refs = refs[n_in:2 * n_in]
    pos = 2 * n_in
    r_ref = refs[pos] if has_residual else None
    o_refs = refs[pos + int(has_residual):]
    acc = None
    for a_ref, w_ref in zip(a_refs, w_refs):
        d = _dot(a_ref[...], w_ref[...])
        acc = d if acc is None else acc + d
    if has_residual:
        acc = r_ref[...] + acc
    for o_ref, (_, scale) in zip(o_refs, outs):
        val = acc if scale == 1.0 else acc * scale
        o_ref[...] = val.astype(o_ref.dtype)


def proj(a_list, w, col0, ncols, tm, tn, outs, residual=None):
    m = a_list[0].shape[0]
    kk = a_list[0].shape[1]
    assert all(a.shape == (m, kk) for a in a_list) and w.shape[0] == kk * len(a_list)
    assert m % tm == 0 and ncols % tn == 0 and col0 % tn == 0
    cb = col0 // tn
    in_specs = [pl.BlockSpec((tm, kk), lambda i, j: (i, 0)) for _ in a_list]
    in_specs += [pl.BlockSpec((kk, tn), functools.partial(lambda i, j, r: (r, cb + j), r=r)) for r in range(len(a_list))]
    args = list(a_list) + [w] * len(a_list)
    if residual is not None:
        in_specs.append(pl.BlockSpec((tm, tn), lambda i, j: (i, j)))
        args.append(residual)
    out_specs = [pl.BlockSpec((tm, tn), lambda i, j: (i, j)) for _ in outs]
    out_shape = [jax.ShapeDtypeStruct((m, ncols), dtype) for dtype, _ in outs]
    return pl.pallas_call(
        functools.partial(_proj_body, n_in=len(a_list), has_residual=residual is not None, outs=tuple(outs)),
        grid=(m // tm, ncols // tn),
        in_specs=in_specs,
        out_specs=out_specs,
        out_shape=out_shape,
        compiler_params=_params("parallel", "parallel"),
        name="proj",
    )(*args)


def _hgrn_maps(c):
    levels = int(np.log2(c))
    assert 2 ** levels == c
    t = np.arange(c)[:, None]
    s = np.arange(c)[None, :]
    mats = [(s <= t), (s > t)]
    for l in range(levels):
        b = 2 ** l
        start = (t // (2 * b)) * (2 * b)
        upper = (t // b) % 2 == 1
        mats.append((upper & (s >= start + b) & (s <= t)) | ((~upper) & (s > t) & (s <= start + b - 1)))
    return np.concatenate(mats, axis=0).astype(np.float32), levels


def _hgrn_body(q_ref, z_ref, v_ref, g_ref, lb_ref, nw_ref, a_ref, s0_ref, o_ref, sout_ref, st_scr,
               *, c, n_chunks, levels, heads):
    tb = pl.program_id(2)

    @pl.when(tb == 0)
    def _():
        for hh in range(heads):
            st_scr[hh] = s0_ref[0, hh].T

    nw = nw_ref[...]
    amat = a_ref[...]
    row = lax.broadcasted_iota(jnp.int32, (c, c), 0)
    col = lax.broadcasted_iota(jnp.int32, (c, c), 1)
    xor = row ^ col

    units = [(slice(ci * c, (ci + 1) * c), slice(hh * HEAD_DIM, (hh + 1) * HEAD_DIM))
             for hh in range(heads) for ci in range(n_chunks)]
    nu = len(units)
    lbs = [lb_ref[:, cols] for _, cols in units]
    qs = [_silu(q_ref[rows, cols]) for rows, cols in units]
    sigs = [_sigmoid_pair(z_ref[rows, cols]) for rows, cols in units]
    gs = [jnp.log(lbs[u] + (1.0 - lbs[u]) * sigs[u][0]) for u in range(nu)]
    ks = [(1.0 - lbs[u]) * sigs[u][1] for u in range(nu)]
    vs = [v_ref[rows, cols] for rows, cols in units]
    vbs = [v.astype(BF16) for v in vs]
    es = [jnp.exp(_dot_exact_lhs01x3(amat, g)) for g in gs]
    atts = [None] * nu
    for l in range(levels - 1, -1, -1):
        for u in range(nu):
            el = es[u][(2 + l) * c:(3 + l) * c]
            al = _dot_nt((qs[u] * el).astype(BF16), (ks[u] * el).astype(BF16))
            atts[u] = al if atts[u] is None else jnp.where(xor < 2 ** (l + 1), al, atts[u])
    atts = [jnp.where(row > col, att, 0.0).astype(BF16) for att in atts]
    o_intras = [_dot(atts[u], vbs[u]) + jnp.sum(qs[u] * ks[u], axis=-1, keepdims=True) * vs[u] for u in range(nu)]
    upds = [_dot_tn(vbs[u], (ks[u] * es[u][c:2 * c]).astype(BF16)) for u in range(nu)]
    qgs = [(qs[u] * es[u][0:c]).astype(BF16) for u in range(nu)]

    for hh in range(heads):
        st = st_scr[hh]
        for ci in range(n_chunks):
            u = hh * n_chunks + ci
            rows, cols = units[u]
            o = _dot_nt(qgs[u], st.astype(BF16)) + o_intras[u]
            st = st * es[u][c - 1:c] + upds[u]
            ms = jnp.mean(o * o, axis=-1, keepdims=True)
            y = o * lax.rsqrt(ms + EPS) * nw
            o_ref[rows, cols] = (y * _silu(g_ref[rows, cols])).astype(o_ref.dtype)
        st_scr[hh] = st

    @pl.when(tb == pl.num_programs(2) - 1)
    def _():
        for hh in range(heads):
            sout_ref[0, hh] = st_scr[hh].T


def hgrn2(qa, fa, ia, ga, lb, a_norm_w, s0, batch, seq, rows_per_step, c, heads_per_step, out_dtype,
          col0=(0, 0, 0, 0)):
    n = qa.shape[0]
    width = lb.shape[0]
    h = width // HEAD_DIM
    assert n == batch * seq and seq % rows_per_step == 0 and rows_per_step % c == 0 and h % heads_per_step == 0
    nb = seq // rows_per_step
    gw = heads_per_step * HEAD_DIM
    amat_np, levels = _hgrn_maps(c)
    amat = jnp.asarray(np.concatenate([amat_np] * 3, axis=1), BF16)
    row_spec = pl.BlockSpec((rows_per_step, gw), lambda b, hh, t: (b * nb + t, hh))
    in_row_specs = [pl.BlockSpec((rows_per_step, gw), functools.partial(lambda b, hh, t, off: (b * nb + t, off + hh), off=c0 // gw))
                    for c0 in col0]
    assert all(c0 % gw == 0 for c0 in col0)
    state_spec = pl.BlockSpec((1, heads_per_step, HEAD_DIM, HEAD_DIM), lambda b, hh, t: (b, hh, 0, 0))
    body = functools.partial(_hgrn_body, c=c, n_chunks=rows_per_step // c, levels=levels, heads=heads_per_step)
    return pl.pallas_call(
        body,
        grid=(batch, h // heads_per_step, nb),
        in_specs=in_row_specs + [
                  pl.BlockSpec((1, gw), lambda b, hh, t: (0, hh)),
                  pl.BlockSpec((1, HEAD_DIM), lambda b, hh, t: (0, 0)),
                  pl.BlockSpec(amat.shape, lambda b, hh, t: (0, 0)),
                  state_spec],
        out_specs=[row_spec, state_spec],
        out_shape=[jax.ShapeDtypeStruct((n, width), out_dtype),
                   jax.ShapeDtypeStruct((batch, h, HEAD_DIM, HEAD_DIM), F32)],
        scratch_shapes=[pltpu.VMEM((heads_per_step, HEAD_DIM, HEAD_DIM), F32)],
        compiler_params=_params("parallel", "parallel", "arbitrary"),
        name="hgrn2",
    )(qa, fa, ia, ga, lb.reshape(1, width), a_norm_w.reshape(1, HEAD_DIM), amat, s0)


def _neg_abs(x):
    return -jnp.abs(x)


def _sb_prompt_body(q_ref, k_ref, v_ref, g_ref, uu_ref, o_ref, run_scr, acc_scr, w_scr, *, tq, tk, rc):
    i = pl.program_id(1)
    nd = tq // tk
    nl = tk // LANES
    q0 = pl.multiple_of(i * tq, tq)
    run_scr[...] = jnp.zeros(run_scr.shape, F32)
    acc_scr[...] = jnp.zeros(acc_scr.shape, F32)

    def apply_pending(slot, k_prev, first_row):
        vt = v_ref[pl.ds(pl.multiple_of(k_prev, tk), tk), :]
        for r0 in range(first_row, tq, rc):
            rows = slice(r0, min(r0 + rc, tq))
            acc_scr[rows, :] = acc_scr[rows, :] + _dot(w_scr[slot, rows, :], vt)

    def score(slot, k0, first_row, masked):
        kt = k_ref[pl.ds(pl.multiple_of(k0, tk), tk), :]
        for r0 in range(first_row, tq, rc):
            r1 = min(r0 + rc, tq)
            rows = slice(r0, r1)
            z = _dot_nt(q_ref[rows, :], kt)
            if masked:
                qpos = lax.broadcasted_iota(jnp.int32, (r1 - r0, LANES), 0) + (q0 + r0)
                kpos = lax.broadcasted_iota(jnp.int32, (r1 - r0, LANES), 1) + k0
            zs, sps, his, los, valids = [], [], [], [], []
            rowsum = None
            for c in range(nl):
                zc = z[:, c * LANES:(c + 1) * LANES]
                sp = jnp.maximum(zc, 0.0) + jnp.log2(1.0 + jnp.exp2(_neg_abs(zc)))
                if masked:
                    valid = (kpos + c * LANES) < qpos
                    sp = jnp.where(valid, sp, 0.0)
                    valids.append(valid)
                hi = sp.astype(BF16)
                lo = (sp - hi.astype(F32)).astype(BF16)
                zs.append(zc); sps.append(sp); his.append(hi); los.append(lo)
                rowsum = sp if rowsum is None else rowsum + sp
            tail = _dot(jnp.concatenate(his + los, axis=1), uu_ref[...])
            run = run_scr[rows, :]
            for c in range(nl):
                w = jnp.exp2(zs[c] - (sps[c] + tail[:, c * LANES:(c + 1) * LANES] + run))
                if masked:
                    w = jnp.where(valids[c], w, 0.0)
                w_scr[slot, rows, c * LANES:(c + 1) * LANES] = w.astype(BF16)
            run_scr[rows, :] = run + jnp.sum(rowsum, axis=-1, keepdims=True)

    assert nd % 2 == 0
    for d in range(nd - 1, 0, -2):
        score(0, q0 + d * tk, d * tk, True)
        if d + 1 < nd:
            apply_pending(1, q0 + (d + 1) * tk, (d + 1) * tk)
        score(1, q0 + (d - 1) * tk, (d - 1) * tk, True)
        apply_pending(0, q0 + d * tk, d * tk)

    def before(carry):
        jj, kp, _ = carry
        for u in range(unroll):
            k_a = q0 - (2 * (unroll * jj + u) + 1) * tk
            score(0, k_a, 0, False)
            apply_pending(1, kp, 0)
            score(1, k_a - tk, 0, False)
            more = jnp.min(run_scr[...]) < F32_UNDERFLOW_LOG2
            apply_pending(0, k_a, 0)
            kp = k_a - tk
        return jj + 1, kp, more

    unroll = 1
    assert nd % (2 * unroll) == 0
    n_trips = (i * nd) // (2 * unroll)
    _, k_pending, _ = lax.while_loop(lambda carry: (carry[0] < n_trips) & carry[2], before,
                                     (jnp.int32(0), q0, jnp.bool_(True)))
    apply_pending(1, k_pending, 0)
    o_ref[...] = (acc_scr[...] * _silu(g_ref[...])).astype(o_ref.dtype)


def _tail_matrix(n):
    sp = np.arange(n)[:, None]
    s = np.arange(n)[None, :]
    return (sp > s).astype(np.float32)


def sb_prompt(q, k, v, gate, tq, tk, rc, out_dtype):
    t, width = q.shape
    h = width // HEAD_DIM
    u = _tail_matrix(tk)
    uu = jnp.asarray(np.concatenate([u, u], axis=0), BF16)
    q_spec = pl.BlockSpec((tq, HEAD_DIM), lambda hh, i: (i, hh))
    kv_spec = pl.BlockSpec((t, HEAD_DIM), lambda hh, i: (0, hh))
    return pl.pallas_call(
        functools.partial(_sb_prompt_body, tq=tq, tk=tk, rc=rc),
        grid=(h, t // tq),
        in_specs=[q_spec, kv_spec, kv_spec, q_spec, pl.BlockSpec(uu.shape, lambda hh, i: (0, 0))],
        out_specs=q_spec,
        out_shape=jax.ShapeDtypeStruct((t, width), out_dtype),
        scratch_shapes=[pltpu.VMEM((tq, LANES), F32), pltpu.VMEM((tq, HEAD_DIM), F32),
                        pltpu.VMEM((2, tq, tk), BF16)],
        compiler_params=_params("parallel", "arbitrary"),
        name="sb_prompt",
    )(q, k, v, gate, uu)


def _logf_cumsum_body(fl_ref, b_ref, tri_ref, lf_ref, f2_ref, carry_scr):
    @pl.when(pl.program_id(0) == 0)
    def _():
        carry_scr[...] = jnp.zeros_like(carry_scr)

    lf = _log_sigmoid(fl_ref[...] + b_ref[...])
    lf_ref[...] = lf
    f = carry_scr[...] + _dot_exact_lhs01(tri_ref[...], lf)
    f2_ref[...] = f * LOG2E
    carry_scr[...] = f[f.shape[0] - 1:, :]


def logf_cumsum(fl, b_forget, blk):
    t, h = fl.shape
    tri = jnp.asarray(np.tril(np.ones((blk, blk), np.float32)), BF16)
    spec = pl.BlockSpec((blk, h), lambda i: (i, 0))
    return pl.pallas_call(
        _logf_cumsum_body,
        grid=(t // blk,),
        in_specs=[spec, pl.BlockSpec((1, h), lambda i: (0, 0)), pl.BlockSpec((blk, blk), lambda i: (0, 0))],
        out_specs=[spec, spec],
        out_shape=[jax.ShapeDtypeStruct((t, h), F32), jax.ShapeDtypeStruct((t, h), F32)],
        scratch_shapes=[pltpu.VMEM((1, h), F32)],
        compiler_params=_params("arbitrary"),
        name="logf_cumsum",
    )(fl, b_forget.reshape(1, h), tri)


def _logf_body(fl_ref, b_ref, lf_ref):
    lf_ref[...] = _log_sigmoid(fl_ref[...] + b_ref[...])


def logf_only(fl, b_forget):
    t, h = fl.shape
    return pl.pallas_call(
        _logf_body,
        grid=(1,),
        in_specs=[pl.BlockSpec((t, h), lambda i: (0, 0)), pl.BlockSpec((1, h), lambda i: (0, 0))],
        out_specs=pl.BlockSpec((t, h), lambda i: (0, 0)),
        out_shape=jax.ShapeDtypeStruct((t, h), F32),
        name="logf",
    )(fl, b_forget.reshape(1, h))


_BIAS_PIECES = 3


def _bias_selectors(h):
    sel = np.zeros((2, h, _BIAS_PIECES, h, LANES), np.float32)
    for hh in range(h):
        for p in range(_BIAS_PIECES):
            sel[0, hh, p, hh, p] = 1.0
            sel[1, hh, p, hh, _BIAS_PIECES + p] = -1.0
    return sel


def _bias_columns(f, sel_ref, query_side):
    out = None
    for p, piece in enumerate(_split3(f)):
        d = _dot(piece, sel_ref[0, p])
        out = d if out is None else out + d
    lane = lax.broadcasted_iota(jnp.int32, out.shape, 1)
    ones_at = (lane >= _BIAS_PIECES) & (lane < 2 * _BIAS_PIECES) if query_side else lane < _BIAS_PIECES
    return jnp.where(ones_at, 1.0, out).astype(BF16)


def _fox_prompt_body(q_ref, k_ref, v_ref, g_ref, f_ref, selq_ref, selk_ref, o_ref,
                     kx_scr, vx_scr, m_scr, acc_scr, p_scr, alpha_scr, kmax_scr, ub_scr, *, tq, tk, rc):
    i = pl.program_id(1)
    nd = tq // tk
    nl = tk // LANES
    t_all = k_ref.shape[0]

    head = pl.program_id(0)

    def own_column(f_rows):
        lane = lax.broadcasted_iota(jnp.int32, f_rows.shape, 1)
        return jnp.sum(jnp.where(lane == head, f_rows, 0.0), axis=-1, keepdims=True)

    @pl.when(i == 0)
    def _():
        kb = k_ref[...].astype(BF16)
        kx_scr[:, 0:HEAD_DIM] = kb
        kx_scr[:, HEAD_DIM:] = _bias_columns(f_ref[...], selk_ref, False)
        vx_scr[:, 0:HEAD_DIM] = v_ref[...].astype(BF16)
        vx_scr[:, HEAD_DIM:] = jnp.ones((t_all, LANES), BF16)
        kf = kb.astype(F32)
        knorm2 = jnp.max(jnp.sum(kf * kf, axis=-1, keepdims=True), axis=0, keepdims=True)
        kmax_scr[...] = jnp.broadcast_to(jnp.sqrt(knorm2), kmax_scr.shape)

    q0 = pl.multiple_of(i * tq, tq)
    qx = jnp.concatenate([q_ref[...], _bias_columns(f_ref[pl.ds(q0, tq), :], selq_ref, True)], axis=1)
    m_scr[...] = jnp.full(m_scr.shape, -jnp.inf, F32)
    acc_scr[...] = jnp.zeros(acc_scr.shape, F32)
    qf = q_ref[...].astype(F32)
    qnorm = jnp.sqrt(jnp.sum(qf * qf, axis=-1, keepdims=True))
    ub_scr[...] = qnorm * kmax_scr[...] * (1.0 + 2.0 ** -10) + own_column(f_ref[pl.ds(q0, tq), :])

    def apply_pending(slot, k_prev, first_row):
        vxt = vx_scr[pl.ds(pl.multiple_of(k_prev, tk), tk), :]
        for r0 in range(first_row, tq, rc):
            rows = slice(r0, r0 + rc)
            pv = _dot(p_scr[slot, rows, :], vxt)
            alpha = alpha_scr[slot, rows, :]
            acc_scr[rows, 0:HEAD_DIM] = alpha * acc_scr[rows, 0:HEAD_DIM] + pv[:, 0:HEAD_DIM]
            acc_scr[rows, HEAD_DIM:] = alpha * acc_scr[rows, HEAD_DIM:] + pv[:, HEAD_DIM:]

    def score(slot, k0, first_row, masked):
        kxt = kx_scr[pl.ds(pl.multiple_of(k0, tk), tk), :]
        for r0 in range(first_row, tq, rc):
            rows = slice(r0, r0 + rc)
            s = _dot_nt(qx[r0:r0 + rc, :], kxt)
            sb = [s[:, c * LANES:(c + 1) * LANES] for c in range(nl)]
            if masked:
                qpos = lax.broadcasted_iota(jnp.int32, (rc, LANES), 0) + (q0 + r0)
                kpos = lax.broadcasted_iota(jnp.int32, (rc, LANES), 1) + k0
                sb = [jnp.where((kpos + c * LANES) <= qpos, sb[c], -jnp.inf) for c in range(nl)]
            mx = sb[0]
            for c in range(1, nl):
                mx = jnp.maximum(mx, sb[c])
            m_old = m_scr[rows, :]
            m_new = jnp.maximum(m_old, jnp.max(mx, axis=-1, keepdims=True))
            alpha_scr[slot, rows, :] = jnp.exp2(m_old - m_new)
            for c in range(nl):
                p_scr[slot, rows, c * LANES:(c + 1) * LANES] = jnp.exp2(sb[c] - m_new).astype(BF16)
            m_scr[rows, :] = m_new

    assert nd % 2 == 0 and tk % rc == 0
    for d in range(0, nd, 2):
        score(0, q0 + d * tk, d * tk, True)
        if d > 0:
            apply_pending(1, q0 + (d - 1) * tk, (d - 1) * tk)
        score(1, q0 + (d + 1) * tk, (d + 1) * tk, True)
        apply_pending(0, q0 + d * tk, d * tk)

    p_scr[1, 0:(nd - 1) * tk, :] = jnp.zeros(((nd - 1) * tk, tk), BF16)
    alpha_scr[1, 0:(nd - 1) * tk, :] = jnp.ones(((nd - 1) * tk, LANES), F32)

    def pair(k_a, kp):
        score(0, k_a, 0, False)
        apply_pending(1, kp, 0)
        score(1, k_a - tk, 0, False)
        more = visible(k_a - tk - 1)
        apply_pending(0, k_a, 0)
        return k_a - tk, more

    def visible(k_hi):
        f_hi = own_column(f_ref[pl.ds(jnp.maximum(k_hi, 0), 1), :])
        return jnp.max(ub_scr[...] - f_hi - m_scr[...]) > -(F32_UNDERFLOW_LOG2 + 2.0)

    def trip(carry):
        jj, kp, _ = carry
        kp, more = pair(q0 - (2 * jj + 1) * tk, kp)
        return jj + 1, kp, more

    n_pairs = (i * nd) // 2
    _, k_pending, _ = lax.while_loop(lambda carry: (carry[0] < n_pairs) & carry[2], trip,
                                     (jnp.int32(0), q0 + (nd - 1) * tk, visible(q0 - 1)))
    apply_pending(1, k_pending, 0)
    o_ref[...] = (acc_scr[:, 0:HEAD_DIM] / acc_scr[:, HEAD_DIM:] * _silu(g_ref[...])).astype(o_ref.dtype)


def fox_prompt(q, k, v, gate, f2, tq, tk, rc, out_dtype):
    t, width = q.shape
    h = width // HEAD_DIM
    sel = jnp.asarray(_bias_selectors(h), BF16)
    q_spec = pl.BlockSpec((tq, HEAD_DIM), lambda hh, i: (i, hh))
    kv_spec = pl.BlockSpec((t, HEAD_DIM), lambda hh, i: (0, hh))
    sel_spec = pl.BlockSpec((1, _BIAS_PIECES, h, LANES), lambda hh, i: (hh, 0, 0, 0))
    return pl.pallas_call(
        functools.partial(_fox_prompt_body, tq=tq, tk=tk, rc=rc),
        grid=(h, t // tq),
        in_specs=[q_spec, kv_spec, kv_spec, q_spec, pl.BlockSpec((t, h), lambda hh, i: (0, 0)),
                  sel_spec, sel_spec],
        out_specs=q_spec,
        out_shape=jax.ShapeDtypeStruct((t, width), out_dtype),
        scratch_shapes=[pltpu.VMEM((t, 2 * HEAD_DIM), BF16), pltpu.VMEM((t, 2 * HEAD_DIM), BF16),
                        pltpu.VMEM((tq, LANES), F32), pltpu.VMEM((tq, 2 * HEAD_DIM), F32),
                        pltpu.VMEM((2, tq, tk), BF16), pltpu.VMEM((2, tq, LANES), F32),
                        pltpu.VMEM((1, LANES), F32), pltpu.VMEM((tq, LANES), F32)],
        compiler_params=_params("parallel", "arbitrary"),
        name="fox_prompt",
    )(q, k, v, gate, f2, sel[0], sel[1])


HEADS_PER_GROUP = 8


def _expand_queries(q, tq):
    gw = q.shape[1]
    rep = jnp.concatenate([q] * HEADS_PER_GROUP, axis=0)
    r = lax.broadcasted_iota(jnp.int32, (HEADS_PER_GROUP * tq, gw), 0) // tq
    cidx = lax.broadcasted_iota(jnp.int32, (HEADS_PER_GROUP * tq, gw), 1) // HEAD_DIM
    return jnp.where(r == cidx, rep, 0.0)


def _collect_heads(full, tq):
    return jnp.concatenate(
        [full[hh * tq:(hh + 1) * tq, hh * HEAD_DIM:(hh + 1) * HEAD_DIM] for hh in range(HEADS_PER_GROUP)],
        axis=1)


def _gather_keys(cache_ref, new_ref, scr, past, tk):
    g = HEADS_PER_GROUP
    x = cache_ref[0].reshape(past // g, g, g, HEAD_DIM)
    x = jnp.swapaxes(x, 1, 2)
    for hh in range(g):
        scr[0:past, hh * HEAD_DIM:(hh + 1) * HEAD_DIM] = x[:, hh].reshape(past, HEAD_DIM).astype(BF16)
    scr[past:tk, :] = new_ref[...].astype(BF16)


def _row_blocks(n):
    return [(r0, min(r0 + LANES, n)) for r0 in range(0, n, LANES)]


def _cumsum_rows(pieces, tri_ref):
    n = pieces[0].shape[0]
    out, carry = [], None
    for r0, r1 in _row_blocks(n):
        tri = tri_ref[0:r1 - r0, 0:r1 - r0]
        local = None
        for piece in pieces:
            d = _dot(tri, piece[r0:r1])
            local = d if local is None else local + d
        if carry is not None:
            local = local + carry
        carry = local[r1 - r0 - 1:r1 - r0]
        out.append(local)
    return jnp.concatenate(out, axis=0)


def _tailsum_rows(pieces, tri_ref):
    n = pieces[0].shape[0]
    out, carry = [], None
    for r0, r1 in reversed(_row_blocks(n)):
        tri = tri_ref[0:r1 - r0, 0:r1 - r0]
        local, total = None, None
        for piece in pieces:
            blk = piece[r0:r1]
            d = _dot_tn(tri, blk)
            local = d if local is None else local + d
            t = blk.astype(F32)
            total = t if total is None else total + t
        local = local - total
        if carry is not None:
            local = local + carry
        carry = local[0:1] + total[0:1]
        out.append(local)
    return jnp.concatenate(out[::-1], axis=0)


def _sb_decode_body(q_ref, kn_ref, vn_ref, kc_ref, vc_ref, g_ref, tri_ref, o_ref, k_scr, v_scr, *, past, tq):
    tk = past + tq
    _gather_keys(kc_ref, kn_ref, k_scr, past, tk)
    _gather_keys(vc_ref, vn_ref, v_scr, past, tk)

    lanes = HEADS_PER_GROUP * tq
    qx = _expand_queries(q_ref[...] * (HEAD_DIM ** -0.5), tq).astype(BF16)
    z = _dot_nt(k_scr[...], qx)
    kpos = lax.broadcasted_iota(jnp.int32, (tk, lanes), 0)
    qpos = past + lax.broadcasted_iota(jnp.int32, (tk, lanes), 1) % tq
    valid = kpos < qpos
    ls_neg = -(jnp.maximum(z, 0.0) + jnp.log(1.0 + jnp.exp(-jnp.abs(z))))
    lm = jnp.where(valid, ls_neg, 0.0)
    hi = lm.astype(BF16)
    lo = (lm - hi.astype(F32)).astype(BF16)
    tail = _tailsum_rows((hi, lo), tri_ref)
    w = jnp.where(valid, jnp.exp(z + ls_neg + tail), 0.0)
    full = _dot_tn(w.astype(BF16), v_scr[...])
    o_ref[...] = (_collect_heads(full, tq) * _silu(g_ref[...])).astype(o_ref.dtype)


def _cache_spec(layer, past):
    return pl.BlockSpec((None, 1, past, HEADS_PER_GROUP, HEAD_DIM), lambda b, g: (layer, b, 0, g, 0))


def _decode_row_specs(tq, gw, col0):
    assert all(c0 % gw == 0 for c0 in col0)
    return [pl.BlockSpec((tq, gw), functools.partial(lambda b, g, off: (b, off + g), off=c0 // gw)) for c0 in col0]


def sb_decode(q, k_new, v_new, k_cache, v_cache, layer, gate, batch, tq, out_dtype, col0=(0, 0, 0, 0)):
    n = q.shape[0]
    width = k_cache.shape[3] * HEAD_DIM
    past = k_cache.shape[2]
    tk = past + tq
    gw = HEADS_PER_GROUP * HEAD_DIM
    groups = width // gw
    tri = jnp.asarray(np.tril(np.ones((LANES, LANES), np.float32)), BF16)
    row_spec = pl.BlockSpec((tq, gw), lambda b, g: (b, g))
    rs = _decode_row_specs(tq, gw, col0)
    cache_spec = _cache_spec(layer, past)
    return pl.pallas_call(
        functools.partial(_sb_decode_body, past=past, tq=tq),
        grid=(batch, groups),
        in_specs=[rs[0], rs[1], rs[2], cache_spec, cache_spec, rs[3],
                  pl.BlockSpec((LANES, LANES), lambda b, g: (0, 0))],
        out_specs=row_spec,
        out_shape=jax.ShapeDtypeStruct((n, width), out_dtype),
        scratch_shapes=[pltpu.VMEM((tk, gw), BF16), pltpu.VMEM((tk, gw), BF16)],
        compiler_params=_params("parallel", "parallel"),
        name="sb_decode",
    )(q, k_new, v_new, k_cache, v_cache, gate, tri)


def _fox_decode_body(q_ref, kn_ref, vn_ref, kc_ref, vc_ref, g_ref, lfn_ref, lfc_ref, ex_ref, tri_ref, o_ref,
                     k_scr, v_scr, lf_scr, *, past, tq):
    tk = past + tq
    _gather_keys(kc_ref, kn_ref, k_scr, past, tk)
    _gather_keys(vc_ref, vn_ref, v_scr, past, tk)
    lf_scr[0:past, :] = lfc_ref[0]
    lf_scr[past:tk, :] = lfn_ref[...]

    lanes = HEADS_PER_GROUP * tq
    hi, mid, lo = _split3(lf_scr[...])
    ex = ex_ref[0]
    lfx = _dot(hi, ex) + _dot(mid, ex) + _dot(lo, ex)
    f_key = _cumsum_rows(_split3(lfx), tri_ref)
    kpos = lax.broadcasted_iota(jnp.int32, (tk, lanes), 0)
    qpos = past + lax.broadcasted_iota(jnp.int32, (tk, lanes), 1) % tq
    f_query = jnp.sum(jnp.where(kpos == qpos, f_key, 0.0), axis=0, keepdims=True)

    qx = _expand_queries(q_ref[...] * (HEAD_DIM ** -0.5), tq).astype(BF16)
    s = _dot_nt(k_scr[...], qx) + (f_query - f_key)
    s = jnp.where(kpos <= qpos, s, -jnp.inf)
    p = jnp.exp(s - jnp.max(s, axis=0, keepdims=True))
    p = p / jnp.sum(p, axis=0, keepdims=True)
    full = _dot_tn(p.astype(BF16), v_scr[...])
    o_ref[...] = (_collect_heads(full, tq) * _silu(g_ref[...])).astype(o_ref.dtype)


def fox_decode(q, k_new, v_new, k_cache, v_cache, layer, gate, lf_new, lf_cache, batch, tq, out_dtype,
               col0=(0, 0, 0, 0)):
    n = q.shape[0]
    h = k_cache.shape[3]
    width = h * HEAD_DIM
    past = k_cache.shape[2]
    tk = past + tq
    gw = HEADS_PER_GROUP * HEAD_DIM
    groups = width // gw
    lanes = HEADS_PER_GROUP * tq
    tri = jnp.asarray(np.tril(np.ones((LANES, LANES), np.float32)), BF16)
    head_of_lane = np.arange(lanes)[None, None, :] // tq + HEADS_PER_GROUP * np.arange(groups)[:, None, None]
    expand = jnp.asarray((np.arange(h)[None, :, None] == head_of_lane).astype(np.float32), BF16)
    row_spec = pl.BlockSpec((tq, gw), lambda b, g: (b, g))
    rs = _decode_row_specs(tq, gw, col0)
    cache_spec = _cache_spec(layer, past)
    return pl.pallas_call(
        functools.partial(_fox_decode_body, past=past, tq=tq),
        grid=(batch, groups),
        in_specs=[rs[0], rs[1], rs[2], cache_spec, cache_spec, rs[3],
                  pl.BlockSpec((tq, h), lambda b, g: (b, 0)),
                  pl.BlockSpec((None, 1, past, h), lambda b, g: (layer, b, 0, 0)),
                  pl.BlockSpec((1, h, lanes), lambda b, g: (g, 0, 0)),
                  pl.BlockSpec((LANES, LANES), lambda b, g: (0, 0))],
        out_specs=row_spec,
        out_shape=jax.ShapeDtypeStruct((n, width), out_dtype),
        scratch_shapes=[pltpu.VMEM((tk, gw), BF16), pltpu.VMEM((tk, gw), BF16), pltpu.VMEM((tk, h), F32)],
        compiler_params=_params("parallel", "parallel"),
        name="fox_decode",
    )(q, k_new, v_new, k_cache, v_cache, gate, lf_new, lf_cache, expand, tri)


QK_SCALE_LOG2 = LOG2E * HEAD_DIM ** -0.5
FLAT32 = (F32, 1.0)
FLAT16 = (BF16, 1.0)
QUERY16 = (BF16, QK_SCALE_LOG2)


def _even_prompt(x, norm_w, w_in, w_out, lb, a_norm_w, s0, seq):
    n, d = x.shape
    half = d // 2
    hn = rmsnorm(x, norm_w, BF16, NORM_ROWS)

    def p(group, *outs):
        return proj([hn], w_in, group * half, half, PROJ_TILE, PROJ_TILE, outs)

    (rec,) = proj([hn], w_in, 0, 4 * half, PROJ_TILE, PROJ_TILE, (FLAT32,))
    (qb,), (gb,) = p(4, QUERY16), p(7, FLAT32)
    kb_leaf, kb = p(5, FLAT32, FLAT16)
    vb_leaf, vb = p(6, FLAT32, FLAT16)
    oa, s_new = hgrn2(rec, rec, rec, rec, lb, a_norm_w, s0, 1, seq, HGRN_ROWS, HGRN_CHUNK, HGRN_HEADS, BF16,
                      col0=(0, half, 2 * half, 3 * half))
    ob = sb_prompt(qb, kb, vb, gb, ATTN_ROWS, SB_KEYS, ATTN_ROWS, BF16)
    (y,) = proj([oa, ob], w_out, 0, d, PROJ_TILE, PROJ_TILE, (FLAT32,), residual=x)
    return y, s_new, kb_leaf, vb_leaf


def _even_decode(x, norm_w, w_in_f32, layer, w_out, lb, a_norm_w, s0, k_cache, v_cache, batch, seq):
    n, d = x.shape
    half = d // 2
    hn = rmsnorm(x, norm_w, BF16, n)
    cols, w_in = round_and_proj(hn, w_in_f32, layer, 8 * half, DECODE_PROJ_COLS)
    oa, s_new = hgrn2(cols, cols, cols, cols, lb, a_norm_w, s0, batch, seq, seq, seq, half // HEAD_DIM, BF16,
                      col0=(0, half, 2 * half, 3 * half))
    ob = sb_decode(cols, cols, cols, k_cache, v_cache, layer, cols, batch, seq, BF16,
                   col0=(4 * half, 5 * half, 6 * half, 7 * half))
    (y,) = proj([oa, ob], w_out, 0, d, n, DECODE_PROJ_COLS, (FLAT32,), residual=x)
    return y, s_new, cols[:, 5 * half:6 * half], cols[:, 6 * half:7 * half], w_in


def _odd_prompt(x, norm_w, w_in, w_fl, b_forget, w_out, seq):
    n, d = x.shape
    heads = d // HEAD_DIM
    hn = rmsnorm(x, norm_w, BF16, NORM_ROWS)

    def p(group, *outs):
        return proj([hn], w_in, group * d, d, PROJ_TILE, PROJ_TILE, outs)

    (q,), (gate,) = p(0, QUERY16), p(3, FLAT32)
    (fl,) = proj([hn], w_fl, 0, heads, PROJ_TILE, heads, (FLAT32,))
    logf, f2 = logf_cumsum(fl, b_forget, CUMSUM_ROWS)
    (k,), (v,) = p(1, FLAT32), p(2, FLAT32)
    o = fox_prompt(q, k, v, gate, f2, ATTN_ROWS, FOX_KEYS, FOX_ROW_CHUNK, BF16)
    (y,) = proj([o], w_out, 0, d, PROJ_TILE, PROJ_TILE, (FLAT32,), residual=x)
    return y, k, v, logf


def _odd_decode(x, norm_w, w_in, w_fl, b_forget, w_out, k_cache, v_cache, lf_cache, layer, batch, seq):
    n, d = x.shape
    heads = d // HEAD_DIM
    hn = rmsnorm(x, norm_w, BF16, n)
    (cols,) = proj([hn], w_in, 0, 4 * d, n, DECODE_PROJ_COLS, (FLAT32,))
    (fl,) = proj([hn], w_fl, 0, heads, n, heads, (FLAT32,))
    logf = logf_only(fl, b_forget)
    o = fox_decode(cols, cols, cols, k_cache, v_cache, layer, cols, logf, lf_cache, batch, seq, BF16,
                   col0=(0, d, 2 * d, 3 * d))
    (y,) = proj([o], w_out, 0, d, n, DECODE_PROJ_COLS, (FLAT32,), residual=x)
    return y, cols[:, d:2 * d], cols[:, 2 * d:3 * d], logf


def kernel(x_prompt, x_sample, state_a_hgrn, cache_b_k, cache_b_v, cache_c_k, cache_c_v, cache_c_logf,
           norm_w, final_norm_w, w_in_even, w_out_even, lb_logits, a_norm_w, w_in_odd, b_forget, w_out_odd):
    bp, tp, d = x_prompt.shape
    bs, ts, _ = x_sample.shape
    assert bp == 1
    depth = norm_w.shape[0]
    n_even = w_in_even.shape[0]
    lb_all = jnp.cumsum(jax.nn.softmax(lb_logits.astype(F32), axis=0), axis=0)[:n_even]

    hp = x_prompt.reshape(bp * tp, d)
    hs = x_sample.reshape(bs * ts, d)
    outs = {name: [] for name in ("sa_p", "sa_s", "bk_p", "bv_p", "bk_s", "bv_s",
                                  "ck_p", "cv_p", "cf_p", "ck_s", "cv_s", "cf_s")}
    for layer in range(depth):
        j = layer // 2
        if layer % 2 == 0:
            a_heads = state_a_hgrn.shape[2]
            b_heads = cache_b_k.shape[3]
            zeros = jnp.zeros((bp, a_heads) + state_a_hgrn.shape[3:], F32)
            w_out = round_weights(w_out_even, j, ROUND_ROWS)
            hs, ss, ksm, vsm, w_in = _even_decode(hs, norm_w[layer], w_in_even, j, w_out, lb_all[j], a_norm_w[j],
                                                  state_a_hgrn[j], cache_b_k, cache_b_v, bs, ts)
            hp, sp, kp, vp = _even_prompt(hp, norm_w[layer], w_in, w_out, lb_all[j], a_norm_w[j], zeros, tp)
            outs["sa_p"].append(sp); outs["sa_s"].append(ss)
            outs["bk_p"].append(kp.reshape(bp, tp, b_heads, HEAD_DIM))
            outs["bv_p"].append(vp.reshape(bp, tp, b_heads, HEAD_DIM))
            outs["bk_s"].append(ksm.reshape(bs, ts, b_heads, HEAD_DIM))
            outs["bv_s"].append(vsm.reshape(bs, ts, b_heads, HEAD_DIM))
        else:
            c_heads = cache_c_k.shape[3]
            w_in = w_in_odd[j].astype(BF16)
            w_fl = w_in[:, 4 * c_heads * HEAD_DIM:]
            w_out = round_weights(w_out_odd, j, ROUND_ROWS)
            hp, kp, vp, fp = _odd_prompt(hp, norm_w[layer], w_in, w_fl, b_forget[j], w_out, tp)
            hs, ksm, vsm, fsm = _odd_decode(hs, norm_w[layer], w_in, w_fl, b_forget[j], w_out,
                                            cache_c_k, cache_c_v, cache_c_logf, j, bs, ts)
            outs["ck_p"].append(kp.reshape(bp, tp, c_heads, HEAD_DIM))
            outs["cv_p"].append(vp.reshape(bp, tp, c_heads, HEAD_DIM))
            outs["cf_p"].append(fp.reshape(bp, tp, c_heads))
            outs["ck_s"].append(ksm.reshape(bs, ts, c_heads, HEAD_DIM))
            outs["cv_s"].append(vsm.reshape(bs, ts, c_heads, HEAD_DIM))
            outs["cf_s"].append(fsm.reshape(bs, ts, c_heads))
    y_prompt = rmsnorm(hp, final_norm_w, F32, NORM_ROWS).reshape(bp, tp, d)
    y_sample = rmsnorm(hs, final_norm_w, F32, bs * ts).reshape(bs, ts, d)
    st = {k: jnp.stack(v) for k, v in outs.items()}
    return (y_prompt, y_sample, st["sa_p"], st["sa_s"], st["bk_p"], st["bv_p"], st["bk_s"], st["bv_s"],
            st["ck_p"], st["cv_p"], st["cf_p"], st["ck_s"], st["cv_s"], st["cf_s"])
```

```python
import functools

import numpy as np
import jax
import jax.numpy as jnp
from jax import lax
from jax.experimental import pallas as pl
from jax.experimental.pallas import tpu as pltpu

F32 = jnp.float32
BF16 = jnp.bfloat16

EPS = 1e-6
HEAD_DIM = 128
LANES = 128
LOG2E = 1.4426950408889634
F32_UNDERFLOW_LOG2 = 150.0
HGRN_CHUNK = 64
VMEM_LIMIT_BYTES = 56 * 1024 * 1024

PROJ_TILE = 1024
DECODE_PROJ_COLS = 512
NORM_ROWS = 256
ROUND_ROWS = 512
ATTN_ROWS = 1024
SB_KEYS = 256
FOX_KEYS = 512
FOX_ROW_CHUNK = 256
HGRN_ROWS = 1024
HGRN_HEADS = 2
CUMSUM_ROWS = 512

_NT = (((1,), (1,)), ((), ()))
_TN = (((0,), (0,)), ((), ()))


def _params(*sem):
    return pltpu.CompilerParams(dimension_semantics=sem, vmem_limit_bytes=VMEM_LIMIT_BYTES)


def _dot(a, b):
    return jnp.dot(a, b, preferred_element_type=F32)


def _dot_nt(a, b):
    return lax.dot_general(a, b, _NT, preferred_element_type=F32)


def _dot_tn(a, b):
    return lax.dot_general(a, b, _TN, preferred_element_type=F32)


def _split3(x):
    hi = x.astype(BF16)
    r1 = x - hi.astype(F32)
    mid = r1.astype(BF16)
    lo = (r1 - mid.astype(F32)).astype(BF16)
    return hi, mid, lo


def _dot_exact_lhs01(a01, x):
    hi, mid, lo = _split3(x)
    return _dot(a01, hi) + _dot(a01, mid) + _dot(a01, lo)


def _dot_exact_lhs01x3(a01x3, x):
    return _dot(a01x3, jnp.concatenate(_split3(x), axis=0))


def _sigmoid_pair(z):
    e = jnp.exp(-jnp.abs(z))
    r = 1.0 / (1.0 + e)
    er = e * r
    pos = z >= 0
    return jnp.where(pos, r, er), jnp.where(pos, er, r)


def _silu(x):
    return x * _sigmoid_pair(x)[0]


def _log_sigmoid(x):
    return jnp.minimum(x, 0.0) - jnp.log(1.0 + jnp.exp(-jnp.abs(x)))


def _rmsnorm_body(x_ref, w_ref, o_ref):
    x = x_ref[...]
    ms = jnp.mean(x * x, axis=-1, keepdims=True)
    o_ref[...] = (x * lax.rsqrt(ms + EPS) * w_ref[...]).astype(o_ref.dtype)


def rmsnorm(x, w, out_dtype, tm):
    m, d = x.shape
    return pl.pallas_call(
        _rmsnorm_body,
        grid=(m // tm,),
        in_specs=[pl.BlockSpec((tm, d), lambda i: (i, 0)), pl.BlockSpec((1, d), lambda i: (0, 0))],
        out_specs=pl.BlockSpec((tm, d), lambda i: (i, 0)),
        out_shape=jax.ShapeDtypeStruct((m, d), out_dtype),
        compiler_params=_params("parallel"),
        name="rmsnorm",
    )(x, w.reshape(1, d))


def _round_body(w_ref, o_ref):
    o_ref[...] = w_ref[...].astype(o_ref.dtype)


def round_weights(w, layer, rows):
    _, k, n = w.shape
    return pl.pallas_call(
        _round_body,
        grid=(k // rows,),
        in_specs=[pl.BlockSpec((None, rows, n), lambda i: (layer, i, 0))],
        out_specs=pl.BlockSpec((rows, n), lambda i: (i, 0)),
        out_shape=jax.ShapeDtypeStruct((k, n), BF16),
        compiler_params=_params("parallel"),
        name="round_weights",
    )(w)


def _round_proj_body(a_ref, w_ref, y_ref, wb_ref):
    wb = w_ref[...].astype(BF16)
    wb_ref[...] = wb
    y_ref[...] = _dot(a_ref[...], wb)


def round_and_proj(a, w, layer, ncols, tn):
    m, kk = a.shape
    return pl.pallas_call(
        _round_proj_body,
        grid=(ncols // tn,),
        in_specs=[pl.BlockSpec((m, kk), lambda j: (0, 0)), pl.BlockSpec((None, kk, tn), lambda j: (layer, 0, j))],
        out_specs=[pl.BlockSpec((m, tn), lambda j: (0, j)), pl.BlockSpec((kk, tn), lambda j: (0, j))],
        out_shape=[jax.ShapeDtypeStruct((m, ncols), F32), jax.ShapeDtypeStruct((kk, ncols), BF16)],
        compiler_params=_params("parallel"),
        name="round_and_proj",
    )(a, w)


def _proj_body(*refs, n_in, has_residual, outs):
    a_refs = refs[:n_in]
    w_refs = refs[n_in:2 * n_in]
    pos = 2 * n_in
    r_ref = refs[pos] if has_residual else None
    o_refs = refs[pos + int(has_residual):]
    acc = None
    for a_ref, w_ref in zip(a_refs, w_refs):
        d = _dot(a_ref[...], w_ref[...])
        acc = d if acc is None else acc + d
    if has_residual:
        acc = r_ref[...] + acc
    for o_ref, (_, scale) in zip(o_refs, outs):
        val = acc if scale == 1.0 else acc * scale
        o_ref[...] = val.astype(o_ref.dtype)


def proj(a_list, w, col0, ncols, tm, tn, outs, residual=None):
    m = a_list[0].shape[0]
    kk = a_list[0].shape[1]
    assert all(a.shape == (m, kk) for a in a_list) and w.shape[0] == kk * len(a_list)
    assert m % tm == 0 and ncols % tn == 0 and col0 % tn == 0
    cb = col0 // tn
    in_specs = [pl.BlockSpec((tm, kk), lambda i, j: (i, 0)) for _ in a_list]
    in_specs += [pl.BlockSpec((kk, tn), functools.partial(lambda i, j, r: (r, cb + j), r=r)) for r in range(len(a_list))]
    args = list(a_list) + [w] * len(a_list)
    if residual is not None:
        in_specs.append(pl.BlockSpec((tm, tn), lambda i, j: (i, j)))
        args.append(residual)
    out_specs = [pl.BlockSpec((tm, tn), lambda i, j: (i, j)) for _ in outs]
    out_shape = [jax.ShapeDtypeStruct((m, ncols), dtype) for dtype, _ in outs]
    return pl.pallas_call(
        functools.partial(_proj_body, n_in=len(a_list), has_residual=residual is not None, outs=tuple(outs)),
        grid=(m // tm, ncols // tn),
        in_specs=in_specs,
        out_specs=out_specs,
        out_shape=out_shape,
        compiler_params=_params("parallel", "parallel"),
        name="proj",
    )(*args)


def _hgrn_maps(c):
    levels = int(np.log2(c))
    assert 2 ** levels == c
    t = np.arange(c)[:, None]
    s = np.arange(c)[None, :]
    mats = [(s <= t), (s > t)]
    for l in range(levels):
        b = 2 ** l
        start = (t // (2 * b)) * (2 * b)
        upper = (t // b) % 2 == 1
        mats.append((upper & (s >= start + b) & (s <= t)) | ((~upper) & (s > t) & (s <= start + b - 1)))
    return np.concatenate(mats, axis=0).astype(np.float32), levels


def _hgrn_body(q_ref, z_ref, v_ref, g_ref, lb_ref, nw_ref, a_ref, s0_ref, o_ref, sout_ref, st_scr,
               *, c, n_chunks, levels, heads):
    tb = pl.program_id(2)

    @pl.when(tb == 0)
    def _():
        for hh in range(heads):
            st_scr[hh] = s0_ref[0, hh].T

    nw = nw_ref[...]
    amat = a_ref[...]
    row = lax.broadcasted_iota(jnp.int32, (c, c), 0)
    col = lax.broadcasted_iota(jnp.int32, (c, c), 1)
    xor = row ^ col

    units = [(slice(ci * c, (ci + 1) * c), slice(hh * HEAD_DIM, (hh + 1) * HEAD_DIM))
             for hh in range(heads) for ci in range(n_chunks)]
    nu = len(units)
    lbs = [lb_ref[:, cols] for _, cols in units]
    qs = [_silu(q_ref[rows, cols]) for rows, cols in units]
    sigs = [_sigmoid_pair(z_ref[rows, cols]) for rows, cols in units]
    gs = [jnp.log(lbs[u] + (1.0 - lbs[u]) * sigs[u][0]) for u in range(nu)]
    ks = [(1.0 - lbs[u]) * sigs[u][1] for u in range(nu)]
    vs = [v_ref[rows, cols] for rows, cols in units]
    vbs = [v.astype(BF16) for v in vs]
    es = [jnp.exp(_dot_exact_lhs01x3(amat, g)) for g in gs]
    atts = [None] * nu
    for l in range(levels - 1, -1, -1):
        for u in range(nu):
            el = es[u][(2 + l) * c:(3 + l) * c]
            al = _dot_nt((qs[u] * el).astype(BF16), (ks[u] * el).astype(BF16))
            atts[u] = al if atts[u] is None else jnp.where(xor < 2 ** (l + 1), al, atts[u])
    atts = [jnp.where(row > col, att, 0.0).astype(BF16) for att in atts]
    o_intras = [_dot(atts[u], vbs[u]) + jnp.sum(qs[u] * ks[u], axis=-1, keepdims=True) * vs[u] for u in range(nu)]
    upds = [_dot_tn(vbs[u], (ks[u] * es[u][c:2 * c]).astype(BF16)) for u in range(nu)]
    qgs = [(qs[u] * es[u][0:c]).astype(BF16) for u in range(nu)]

    for hh in range(heads):
        st = st_scr[hh]
        for ci in range(n_chunks):
            u = hh * n_chunks + ci
            rows, cols = units[u]
            o = _dot_nt(qgs[u], st.astype(BF16)) + o_intras[u]
            st = st * es[u][c - 1:c] + upds[u]
            ms = jnp.mean(o * o, axis=-1, keepdims=True)
            y = o * lax.rsqrt(ms + EPS) * nw
            o_ref[rows, cols] = (y * _silu(g_ref[rows, cols])).astype(o_ref.dtype)
        st_scr[hh] = st

    @pl.when(tb == pl.num_programs(2) - 1)
    def _():
        for hh in range(heads):
            sout_ref[0, hh] = st_scr[hh].T


def hgrn2(qa, fa, ia, ga, lb, a_norm_w, s0, batch, seq, rows_per_step, c, heads_per_step, out_dtype,
          col0=(0, 0, 0, 0)):
    n = qa.shape[0]
    width = lb.shape[0]
    h = width // HEAD_DIM
    assert n == batch * seq and seq % rows_per_step == 0 and rows_per_step % c == 0 and h % heads_per_step == 0
    nb = seq // rows_per_step
    gw = heads_per_step * HEAD_DIM
    amat_np, levels = _hgrn_maps(c)
    amat = jnp.asarray(np.concatenate([amat_np] * 3, axis=1), BF16)
    row_spec = pl.BlockSpec((rows_per_step, gw), lambda b, hh, t: (b * nb + t, hh))
    in_row_specs = [pl.BlockSpec((rows_per_step, gw), functools.partial(lambda b, hh, t, off: (b * nb + t, off + hh), off=c0 // gw))
                    for c0 in col0]
    assert all(c0 % gw == 0 for c0 in col0)
    state_spec = pl.BlockSpec((1, heads_per_step, HEAD_DIM, HEAD_DIM), lambda b, hh, t: (b, hh, 0, 0))
    body = functools.partial(_hgrn_body, c=c, n_chunks=rows_per_step // c, levels=levels, heads=heads_per_step)
    return pl.pallas_call(
        body,
        grid=(batch, h // heads_per_step, nb),
        in_specs=in_row_specs + [
                  pl.BlockSpec((1, gw), lambda b, hh, t: (0, hh)),
                  pl.BlockSpec((1, HEAD_DIM), lambda b, hh, t: (0, 0)),
                  pl.BlockSpec(amat.shape, lambda b, hh, t: (0, 0)),
                  state_spec],
        out_specs=[row_spec, state_spec],
        out_shape=[jax.ShapeDtypeStruct((n, width), out_dtype),
                   jax.ShapeDtypeStruct((batch, h, HEAD_DIM, HEAD_DIM), F32)],
        scratch_shapes=[pltpu.VMEM((heads_per_step, HEAD_DIM, HEAD_DIM), F32)],
        compiler_params=_params("parallel", "parallel", "arbitrary"),
        name="hgrn2",
    )(qa, fa, ia, ga, lb.reshape(1, width), a_norm_w.reshape(1, HEAD_DIM), amat, s0)


def _sb_prompt_body(q_ref, k_ref, v_ref, g_ref, uu_ref, o_ref, run_scr, acc_scr, w_scr, *, tq, tk, rc):
    i = pl.program_id(1)
    nd = tq // tk
    nl = tk // LANES
    q0 = pl.multiple_of(i * tq, tq)
    run_scr[...] = jnp.zeros(run_scr.shape, F32)
    acc_scr[...] = jnp.zeros(acc_scr.shape, F32)

    def apply_pending(slot, k_prev, first_row):
        vt = v_ref[pl.ds(pl.multiple_of(k_prev, tk), tk), :]
        for r0 in range(first_row, tq, rc):
            rows = slice(r0, min(r0 + rc, tq))
            acc_scr[rows, :] = acc_scr[rows, :] + _dot(w_scr[slot, rows, :], vt)

    def score(slot, k0, first_row, masked):
        kt = k_ref[pl.ds(pl.multiple_of(k0, tk), tk), :]
        for r0 in range(first_row, tq, rc):
            r1 = min(r0 + rc, tq)
            rows = slice(r0, r1)
            z = _dot_nt(q_ref[rows, :], kt)
            if masked:
                qpos = lax.broadcasted_iota(jnp.int32, (r1 - r0, LANES), 0) + (q0 + r0)
                kpos = lax.broadcasted_iota(jnp.int32, (r1 - r0, LANES), 1) + k0
            zs, sps, his, los, valids = [], [], [], [], []
            rowsum = None
            for c in range(nl):
                zc = z[:, c * LANES:(c + 1) * LANES]
                sp = jnp.maximum(zc, 0.0) + jnp.log2(1.0 + jnp.exp2(-jnp.abs(zc)))
                if masked:
                    valid = (kpos + c * LANES) < qpos
                    sp = jnp.where(valid, sp, 0.0)
                    valids.append(valid)
                hi = sp.astype(BF16)
                lo = (sp - hi.astype(F32)).astype(BF16)
                zs.append(zc); sps.append(sp); his.append(hi); los.append(lo)
                rowsum = sp if rowsum is None else rowsum + sp
            tail = _dot(jnp.concatenate(his + los, axis=1), uu_ref[...])
            run = run_scr[rows, :]
            for c in range(nl):
                w = jnp.exp2(zs[c] - (sps[c] + tail[:, c * LANES:(c + 1) * LANES] + run))
                if masked:
                    w = jnp.where(valids[c], w, 0.0)
                w_scr[slot, rows, c * LANES:(c + 1) * LANES] = w.astype(BF16)
            run_scr[rows, :] = run + jnp.sum(rowsum, axis=-1, keepdims=True)

    assert nd % 2 == 0
    for d in range(nd - 1, 0, -2):
        score(0, q0 + d * tk, d * tk, True)
        if d + 1 < nd:
            apply_pending(1, q0 + (d + 1) * tk, (d + 1) * tk)
        score(1, q0 + (d - 1) * tk, (d - 1) * tk, True)
        apply_pending(0, q0 + d * tk, d * tk)

    def trip(carry):
        jj, kp, _ = carry
        k_a = q0 - (2 * jj + 1) * tk
        score(0, k_a, 0, False)
        apply_pending(1, kp, 0)
        score(1, k_a - tk, 0, False)
        more = jnp.min(run_scr[...]) < F32_UNDERFLOW_LOG2
        apply_pending(0, k_a, 0)
        return jj + 1, k_a - tk, more

    _, k_pending, _ = lax.while_loop(lambda carry: (carry[0] < (i * nd) // 2) & carry[2], trip,
                                     (jnp.int32(0), q0, jnp.bool_(True)))
    apply_pending(1, k_pending, 0)
    o_ref[...] = (acc_scr[...] * _silu(g_ref[...])).astype(o_ref.dtype)


def _tail_matrix(n):
    sp = np.arange(n)[:, None]
    s = np.arange(n)[None, :]
    return (sp > s).astype(np.float32)


def sb_prompt(q, k, v, gate, tq, tk, rc, out_dtype):
    t, width = q.shape
    h = width // HEAD_DIM
    u = _tail_matrix(tk)
    uu = jnp.asarray(np.concatenate([u, u], axis=0), BF16)
    q_spec = pl.BlockSpec((tq, HEAD_DIM), lambda hh, i: (i, hh))
    kv_spec = pl.BlockSpec((t, HEAD_DIM), lambda hh, i: (0, hh))
    return pl.pallas_call(
        functools.partial(_sb_prompt_body, tq=tq, tk=tk, rc=rc),
        grid=(h, t // tq),
        in_specs=[q_spec, kv_spec, kv_spec, q_spec, pl.BlockSpec(uu.shape, lambda hh, i: (0, 0))],
        out_specs=q_spec,
        out_shape=jax.ShapeDtypeStruct((t, width), out_dtype),
        scratch_shapes=[pltpu.VMEM((tq, LANES), F32), pltpu.VMEM((tq, HEAD_DIM), F32),
                        pltpu.VMEM((2, tq, tk), BF16)],
        compiler_params=_params("parallel", "arbitrary"),
        name="sb_prompt",
    )(q, k, v, gate, uu)


def _logf_cumsum_body(fl_ref, b_ref, tri_ref, lf_ref, f2_ref, carry_scr):
    @pl.when(pl.program_id(0) == 0)
    def _():
        carry_scr[...] = jnp.zeros_like(carry_scr)

    lf = _log_sigmoid(fl_ref[...] + b_ref[...])
    lf_ref[...] = lf
    f = carry_scr[...] + _dot_exact_lhs01(tri_ref[...], lf)
    f2_ref[...] = f * LOG2E
    carry_scr[...] = f[f.shape[0] - 1:, :]


def logf_cumsum(fl, b_forget, blk):
    t, h = fl.shape
    tri = jnp.asarray(np.tril(np.ones((blk, blk), np.float32)), BF16)
    spec = pl.BlockSpec((blk, h), lambda i: (i, 0))
    return pl.pallas_call(
        _logf_cumsum_body,
        grid=(t // blk,),
        in_specs=[spec, pl.BlockSpec((1, h), lambda i: (0, 0)), pl.BlockSpec((blk, blk), lambda i: (0, 0))],
        out_specs=[spec, spec],
        out_shape=[jax.ShapeDtypeStruct((t, h), F32), jax.ShapeDtypeStruct((t, h), F32)],
        scratch_shapes=[pltpu.VMEM((1, h), F32)],
        compiler_params=_params("arbitrary"),
        name="logf_cumsum",
    )(fl, b_forget.reshape(1, h), tri)


def _logf_body(fl_ref, b_ref, lf_ref):
    lf_ref[...] = _log_sigmoid(fl_ref[...] + b_ref[...])


def logf_only(fl, b_forget):
    t, h = fl.shape
    return pl.pallas_call(
        _logf_body,
        grid=(1,),
        in_specs=[pl.BlockSpec((t, h), lambda i: (0, 0)), pl.BlockSpec((1, h), lambda i: (0, 0))],
        out_specs=pl.BlockSpec((t, h), lambda i: (0, 0)),
        out_shape=jax.ShapeDtypeStruct((t, h), F32),
        name="logf",
    )(fl, b_forget.reshape(1, h))


_BIAS_PIECES = 3


def _bias_selectors(h):
    sel = np.zeros((2, h, _BIAS_PIECES, h, LANES), np.float32)
    for hh in range(h):
        for p in range(_BIAS_PIECES):
            sel[0, hh, p, hh, p] = 1.0
            sel[1, hh, p, hh, _BIAS_PIECES + p] = -1.0
    return sel


def _bias_columns(f, sel_ref, query_side):
    out = None
    for p, piece in enumerate(_split3(f)):
        d = _dot(piece, sel_ref[0, p])
        out = d if out is None else out + d
    lane = lax.broadcasted_iota(jnp.int32, out.shape, 1)
    ones_at = (lane >= _BIAS_PIECES) & (lane < 2 * _BIAS_PIECES) if query_side else lane < _BIAS_PIECES
    return jnp.where(ones_at, 1.0, out).astype(BF16)


def _fox_prompt_body(q_ref, k_ref, v_ref, g_ref, f_ref, selq_ref, selk_ref, o_ref,
                     kx_scr, vx_scr, m_scr, acc_scr, p_scr, alpha_scr, kmax_scr, ub_scr, *, tq, tk, rc):
    i = pl.program_id(1)
    nd = tq // tk
    nl = tk // LANES
    t_all = k_ref.shape[0]

    head = pl.program_id(0)

    def own_column(f_rows):
        lane = lax.broadcasted_iota(jnp.int32, f_rows.shape, 1)
        return jnp.sum(jnp.where(lane == head, f_rows, 0.0), axis=-1, keepdims=True)

    @pl.when(i == 0)
    def _():
        kb = k_ref[...].astype(BF16)
        kx_scr[:, 0:HEAD_DIM] = kb
        kx_scr[:, HEAD_DIM:] = _bias_columns(f_ref[...], selk_ref, False)
        vx_scr[:, 0:HEAD_DIM] = v_ref[...].astype(BF16)
        vx_scr[:, HEAD_DIM:] = jnp.ones((t_all, LANES), BF16)
        kf = kb.astype(F32)
        knorm2 = jnp.max(jnp.sum(kf * kf, axis=-1, keepdims=True), axis=0, keepdims=True)
        kmax_scr[...] = jnp.broadcast_to(jnp.sqrt(knorm2), kmax_scr.shape)

    q0 = pl.multiple_of(i * tq, tq)
    qx = jnp.concatenate([q_ref[...], _bias_columns(f_ref[pl.ds(q0, tq), :], selq_ref, True)], axis=1)
    m_scr[...] = jnp.full(m_scr.shape, -jnp.inf, F32)
    acc_scr[...] = jnp.zeros(acc_scr.shape, F32)
    qf = q_ref[...].astype(F32)
    qnorm = jnp.sqrt(jnp.sum(qf * qf, axis=-1, keepdims=True))
    ub_scr[...] = qnorm * kmax_scr[...] * (1.0 + 2.0 ** -10) + own_column(f_ref[pl.ds(q0, tq), :])

    def apply_pending(slot, k_prev, first_row):
        vxt = vx_scr[pl.ds(pl.multiple_of(k_prev, tk), tk), :]
        for r0 in range(first_row, tq, rc):
            rows = slice(r0, r0 + rc)
            pv = _dot(p_scr[slot, rows, :], vxt)
            alpha = alpha_scr[slot, rows, :]
            acc_scr[rows, 0:HEAD_DIM] = alpha * acc_scr[rows, 0:HEAD_DIM] + pv[:, 0:HEAD_DIM]
            acc_scr[rows, HEAD_DIM:] = alpha * acc_scr[rows, HEAD_DIM:] + pv[:, HEAD_DIM:]

    def score(slot, k0, first_row, masked):
        kxt = kx_scr[pl.ds(pl.multiple_of(k0, tk), tk), :]
        for r0 in range(first_row, tq, rc):
            rows = slice(r0, r0 + rc)
            s = _dot_nt(qx[r0:r0 + rc, :], kxt)
            sb = [s[:, c * LANES:(c + 1) * LANES] for c in range(nl)]
            if masked:
                qpos = lax.broadcasted_iota(jnp.int32, (rc, LANES), 0) + (q0 + r0)
                kpos = lax.broadcasted_iota(jnp.int32, (rc, LANES), 1) + k0
                sb = [jnp.where((kpos + c * LANES) <= qpos, sb[c], -jnp.inf) for c in range(nl)]
            mx = sb[0]
            for c in range(1, nl):
                mx = jnp.maximum(mx, sb[c])
            m_old = m_scr[rows, :]
            m_new = jnp.maximum(m_old, jnp.max(mx, axis=-1, keepdims=True))
            alpha_scr[slot, rows, :] = jnp.exp2(m_old - m_new)
            for c in range(nl):
                p_scr[slot, rows, c * LANES:(c + 1) * LANES] = jnp.exp2(sb[c] - m_new).astype(BF16)
            m_scr[rows, :] = m_new

    assert nd % 2 == 0 and tk % rc == 0
    for d in range(0, nd, 2):
        score(0, q0 + d * tk, d * tk, True)
        if d > 0:
            apply_pending(1, q0 + (d - 1) * tk, (d - 1) * tk)
        score(1, q0 + (d + 1) * tk, (d + 1) * tk, True)
        apply_pending(0, q0 + d * tk, d * tk)

    p_scr[1, 0:(nd - 1) * tk, :] = jnp.zeros(((nd - 1) * tk, tk), BF16)
    alpha_scr[1, 0:(nd - 1) * tk, :] = jnp.ones(((nd - 1) * tk, LANES), F32)

    def pair(k_a, kp):
        score(0, k_a, 0, False)
        apply_pending(1, kp, 0)
        score(1, k_a - tk, 0, False)
        more = visible(k_a - tk - 1)
        apply_pending(0, k_a, 0)
        return k_a - tk, more

    def visible(k_hi):
        f_hi = own_column(f_ref[pl.ds(jnp.maximum(k_hi, 0), 1), :])
        return jnp.max(ub_scr[...] - f_hi - m_scr[...]) > -(F32_UNDERFLOW_LOG2 + 2.0)

    def trip(carry):
        jj, kp, _ = carry
        kp, more = pair(q0 - (2 * jj + 1) * tk, kp)
        return jj + 1, kp, more

    n_pairs = (i * nd) // 2
    _, k_pending, _ = lax.while_loop(lambda carry: (carry[0] < n_pairs) & carry[2], trip,
                                     (jnp.int32(0), q0 + (nd - 1) * tk, visible(q0 - 1)))
    apply_pending(1, k_pending, 0)
    o_ref[...] = (acc_scr[:, 0:HEAD_DIM] / acc_scr[:, HEAD_DIM:] * _silu(g_ref[...])).astype(o_ref.dtype)


def fox_prompt(q, k, v, gate, f2, tq, tk, rc, out_dtype):
    t, width = q.shape
    h = width // HEAD_DIM
    sel = jnp.asarray(_bias_selectors(h), BF16)
    q_spec = pl.BlockSpec((tq, HEAD_DIM), lambda hh, i: (i, hh))
    kv_spec = pl.BlockSpec((t, HEAD_DIM), lambda hh, i: (0, hh))
    sel_spec = pl.BlockSpec((1, _BIAS_PIECES, h, LANES), lambda hh, i: (hh, 0, 0, 0))
    return pl.pallas_call(
        functools.partial(_fox_prompt_body, tq=tq, tk=tk, rc=rc),
        grid=(h, t // tq),
        in_specs=[q_spec, kv_spec, kv_spec, q_spec, pl.BlockSpec((t, h), lambda hh, i: (0, 0)),
                  sel_spec, sel_spec],
        out_specs=q_spec,
        out_shape=jax.ShapeDtypeStruct((t, width), out_dtype),
        scratch_shapes=[pltpu.VMEM((t, 2 * HEAD_DIM), BF16), pltpu.VMEM((t, 2 * HEAD_DIM), BF16),
                        pltpu.VMEM((tq, LANES), F32), pltpu.VMEM((tq, 2 * HEAD_DIM), F32),
                        pltpu.VMEM((2, tq, tk), BF16), pltpu.VMEM((2, tq, LANES), F32),
                        pltpu.VMEM((1, LANES), F32), pltpu.VMEM((tq, LANES), F32)],
        compiler_params=_params("parallel", "arbitrary"),
        name="fox_prompt",
    )(q, k, v, gate, f2, sel[0], sel[1])


HEADS_PER_GROUP = 8


def _expand_queries(q, tq):
    gw = q.shape[1]
    rep = jnp.concatenate([q] * HEADS_PER_GROUP, axis=0)
    r = lax.broadcasted_iota(jnp.int32, (HEADS_PER_GROUP * tq, gw), 0) // tq
    cidx = lax.broadcasted_iota(jnp.int32, (HEADS_PER_GROUP * tq, gw), 1) // HEAD_DIM
    return jnp.where(r == cidx, rep, 0.0)


def _collect_heads(full, tq):
    return jnp.concatenate(
        [full[hh * tq:(hh + 1) * tq, hh * HEAD_DIM:(hh + 1) * HEAD_DIM] for hh in range(HEADS_PER_GROUP)],
        axis=1)


def _gather_keys(cache_ref, new_ref, scr, past, tk):
    g = HEADS_PER_GROUP
    x = cache_ref[0].reshape(past // g, g, g, HEAD_DIM)
    x = jnp.swapaxes(x, 1, 2)
    for hh in range(g):
        scr[0:past, hh * HEAD_DIM:(hh + 1) * HEAD_DIM] = x[:, hh].reshape(past, HEAD_DIM).astype(BF16)
    scr[past:tk, :] = new_ref[...].astype(BF16)


def _row_blocks(n):
    return [(r0, min(r0 + LANES, n)) for r0 in range(0, n, LANES)]


def _cumsum_rows(pieces, tri_ref):
    n = pieces[0].shape[0]
    out, carry = [], None
    for r0, r1 in _row_blocks(n):
        tri = tri_ref[0:r1 - r0, 0:r1 - r0]
        local = None
        for piece in pieces:
            d = _dot(tri, piece[r0:r1])
            local = d if local is None else local + d
        if carry is not None:
            local = local + carry
        carry = local[r1 - r0 - 1:r1 - r0]
        out.append(local)
    return jnp.concatenate(out, axis=0)


def _tailsum_rows(pieces, tri_ref):
    n = pieces[0].shape[0]
    out, carry = [], None
    for r0, r1 in reversed(_row_blocks(n)):
        tri = tri_ref[0:r1 - r0, 0:r1 - r0]
        local, total = None, None
        for piece in pieces:
            blk = piece[r0:r1]
            d = _dot_tn(tri, blk)
            local = d if local is None else local + d
            t = blk.astype(F32)
            total = t if total is None else total + t
        local = local - total
        if carry is not None:
            local = local + carry
        carry = local[0:1] + total[0:1]
        out.append(local)
    return jnp.concatenate(out[::-1], axis=0)


def _sb_decode_body(q_ref, kn_ref, vn_ref, kc_ref, vc_ref, g_ref, tri_ref, o_ref, k_scr, v_scr, *, past, tq):
    tk = past + tq
    _gather_keys(kc_ref, kn_ref, k_scr, past, tk)
    _gather_keys(vc_ref, vn_ref, v_scr, past, tk)

    lanes = HEADS_PER_GROUP * tq
    qx = _expand_queries(q_ref[...] * (HEAD_DIM ** -0.5), tq).astype(BF16)
    z = _dot_nt(k_scr[...], qx)
    kpos = lax.broadcasted_iota(jnp.int32, (tk, lanes), 0)
    qpos = past + lax.broadcasted_iota(jnp.int32, (tk, lanes), 1) % tq
    valid = kpos < qpos
    ls_neg = -(jnp.maximum(z, 0.0) + jnp.log(1.0 + jnp.exp(-jnp.abs(z))))
    lm = jnp.where(valid, ls_neg, 0.0)
    hi = lm.astype(BF16)
    lo = (lm - hi.astype(F32)).astype(BF16)
    tail = _tailsum_rows((hi, lo), tri_ref)
    w = jnp.where(valid, jnp.exp(z + ls_neg + tail), 0.0)
    full = _dot_tn(w.astype(BF16), v_scr[...])
    o_ref[...] = (_collect_heads(full, tq) * _silu(g_ref[...])).astype(o_ref.dtype)


def _cache_spec(layer, past):
    return pl.BlockSpec((None, 1, past, HEADS_PER_GROUP, HEAD_DIM), lambda b, g: (layer, b, 0, g, 0))


def _decode_row_specs(tq, gw, col0):
    assert all(c0 % gw == 0 for c0 in col0)
    return [pl.BlockSpec((tq, gw), functools.partial(lambda b, g, off: (b, off + g), off=c0 // gw)) for c0 in col0]


def sb_decode(q, k_new, v_new, k_cache, v_cache, layer, gate, batch, tq, out_dtype, col0=(0, 0, 0, 0)):
    n = q.shape[0]
    width = k_cache.shape[3] * HEAD_DIM
    past = k_cache.shape[2]
    tk = past + tq
    gw = HEADS_PER_GROUP * HEAD_DIM
    groups = width // gw
    tri = jnp.asarray(np.tril(np.ones((LANES, LANES), np.float32)), BF16)
    row_spec = pl.BlockSpec((tq, gw), lambda b, g: (b, g))
    rs = _decode_row_specs(tq, gw, col0)
    cache_spec = _cache_spec(layer, past)
    return pl.pallas_call(
        functools.partial(_sb_decode_body, past=past, tq=tq),
        grid=(batch, groups),
        in_specs=[rs[0], rs[1], rs[2], cache_spec, cache_spec, rs[3],
                  pl.BlockSpec((LANES, LANES), lambda b, g: (0, 0))],
        out_specs=row_spec,
        out_shape=jax.ShapeDtypeStruct((n, width), out_dtype),
        scratch_shapes=[pltpu.VMEM((tk, gw), BF16), pltpu.VMEM((tk, gw), BF16)],
        compiler_params=_params("parallel", "parallel"),
        name="sb_decode",
    )(q, k_new, v_new, k_cache, v_cache, gate, tri)


def _fox_decode_body(q_ref, kn_ref, vn_ref, kc_ref, vc_ref, g_ref, lfn_ref, lfc_ref, ex_ref, tri_ref, o_ref,
                     k_scr, v_scr, lf_scr, *, past, tq):
    tk = past + tq
    _gather_keys(kc_ref, kn_ref, k_scr, past, tk)
    _gather_keys(vc_ref, vn_ref, v_scr, past, tk)
    lf_scr[0:past, :] = lfc_ref[0]
    lf_scr[past:tk, :] = lfn_ref[...]

    lanes = HEADS_PER_GROUP * tq
    hi, mid, lo = _split3(lf_scr[...])
    ex = ex_ref[0]
    lfx = _dot(hi, ex) + _dot(mid, ex) + _dot(lo, ex)
    f_key = _cumsum_rows(_split3(lfx), tri_ref)
    kpos = lax.broadcasted_iota(jnp.int32, (tk, lanes), 0)
    qpos = past + lax.broadcasted_iota(jnp.int32, (tk, lanes), 1) % tq
    f_query = jnp.sum(jnp.where(kpos == qpos, f_key, 0.0), axis=0, keepdims=True)

    qx = _expand_queries(q_ref[...] * (HEAD_DIM ** -0.5), tq).astype(BF16)
    s = _dot_nt(k_scr[...], qx) + (f_query - f_key)
    s = jnp.where(kpos <= qpos, s, -jnp.inf)
    p = jnp.exp(s - jnp.max(s, axis=0, keepdims=True))
    p = p / jnp.sum(p, axis=0, keepdims=True)
    full = _dot_tn(p.astype(BF16), v_scr[...])
    o_ref[...] = (_collect_heads(full, tq) * _silu(g_ref[...])).astype(o_ref.dtype)


def fox_decode(q, k_new, v_new, k_cache, v_cache, layer, gate, lf_new, lf_cache, batch, tq, out_dtype,
               col0=(0, 0, 0, 0)):
    n = q.shape[0]
    h = k_cache.shape[3]
    width = h * HEAD_DIM
    past = k_cache.shape[2]
    tk = past + tq
    gw = HEADS_PER_GROUP * HEAD_DIM
    groups = width // gw
    lanes = HEADS_PER_GROUP * tq
    tri = jnp.asarray(np.tril(np.ones((LANES, LANES), np.float32)), BF16)
    head_of_lane = np.arange(lanes)[None, None, :] // tq + HEADS_PER_GROUP * np.arange(groups)[:, None, None]
    expand = jnp.asarray((np.arange(h)[None, :, None] == head_of_lane).astype(np.float32), BF16)
    row_spec = pl.BlockSpec((tq, gw), lambda b, g: (b, g))
    rs = _decode_row_specs(tq, gw, col0)
    cache_spec = _cache_spec(layer, past)
    return pl.pallas_call(
        functools.partial(_fox_decode_body, past=past, tq=tq),
        grid=(batch, groups),
        in_specs=[rs[0], rs[1], rs[2], cache_spec, cache_spec, rs[3],
                  pl.BlockSpec((tq, h), lambda b, g: (b, 0)),
                  pl.BlockSpec((None, 1, past, h), lambda b, g: (layer, b, 0, 0)),
                  pl.BlockSpec((1, h, lanes), lambda b, g: (g, 0, 0)),
                  pl.BlockSpec((LANES, LANES), lambda b, g: (0, 0))],
        out_specs=row_spec,
        out_shape=jax.ShapeDtypeStruct((n, width), out_dtype),
        scratch_shapes=[pltpu.VMEM((tk, gw), BF16), pltpu.VMEM((tk, gw), BF16), pltpu.VMEM((tk, h), F32)],
        compiler_params=_params("parallel", "parallel"),
        name="fox_decode",
    )(q, k_new, v_new, k_cache, v_cache, gate, lf_new, lf_cache, expand, tri)


QK_SCALE_LOG2 = LOG2E * HEAD_DIM ** -0.5
FLAT32 = (F32, 1.0)
FLAT16 = (BF16, 1.0)
QUERY16 = (BF16, QK_SCALE_LOG2)


def _even_prompt(x, norm_w, w_in, w_out, lb, a_norm_w, s0, seq):
    n, d = x.shape
    half = d // 2
    hn = rmsnorm(x, norm_w, BF16, NORM_ROWS)

    def p(group, *outs):
        return proj([hn], w_in, group * half, half, PROJ_TILE, PROJ_TILE, outs)

    (rec,) = proj([hn], w_in, 0, 4 * half, PROJ_TILE, PROJ_TILE, (FLAT32,))
    (qb,), (gb,) = p(4, QUERY16), p(7, FLAT32)
    kb_leaf, kb = p(5, FLAT32, FLAT16)
    vb_leaf, vb = p(6, FLAT32, FLAT16)
    oa, s_new = hgrn2(rec, rec, rec, rec, lb, a_norm_w, s0, 1, seq, HGRN_ROWS, HGRN_CHUNK, HGRN_HEADS, BF16,
                      col0=(0, half, 2 * half, 3 * half))
    ob = sb_prompt(qb, kb, vb, gb, ATTN_ROWS, SB_KEYS, ATTN_ROWS, BF16)
    (y,) = proj([oa, ob], w_out, 0, d, PROJ_TILE, PROJ_TILE, (FLAT32,), residual=x)
    return y, s_new, kb_leaf, vb_leaf


def _even_decode(x, norm_w, w_in_f32, layer, w_out, lb, a_norm_w, s0, k_cache, v_cache, batch, seq):
    n, d = x.shape
    half = d // 2
    hn = rmsnorm(x, norm_w, BF16, n)
    cols, w_in = round_and_proj(hn, w_in_f32, layer, 8 * half, DECODE_PROJ_COLS)
    oa, s_new = hgrn2(cols, cols, cols, cols, lb, a_norm_w, s0, batch, seq, seq, seq, half // HEAD_DIM, BF16,
                      col0=(0, half, 2 * half, 3 * half))
    ob = sb_decode(cols, cols, cols, k_cache, v_cache, layer, cols, batch, seq, BF16,
                   col0=(4 * half, 5 * half, 6 * half, 7 * half))
    (y,) = proj([oa, ob], w_out, 0, d, n, DECODE_PROJ_COLS, (FLAT32,), residual=x)
    return y, s_new, cols[:, 5 * half:6 * half], cols[:, 6 * half:7 * half], w_in


def _odd_prompt(x, norm_w, w_in, w_fl, b_forget, w_out, seq):
    n, d = x.shape
    heads = d // HEAD_DIM
    hn = rmsnorm(x, norm_w, BF16, NORM_ROWS)

    def p(group, *outs):
        return proj([hn], w_in, group * d, d, PROJ_TILE, PROJ_TILE, outs)

    (q,), (gate,) = p(0, QUERY16), p(3, FLAT32)
    (fl,) = proj([hn], w_fl, 0, heads, PROJ_TILE, heads, (FLAT32,))
    logf, f2 = logf_cumsum(fl, b_forget, CUMSUM_ROWS)
    (k,), (v,) = p(1, FLAT32), p(2, FLAT32)
    o = fox_prompt(q, k, v, gate, f2, ATTN_ROWS, FOX_KEYS, FOX_ROW_CHUNK, BF16)
    (y,) = proj([o], w_out, 0, d, PROJ_TILE, PROJ_TILE, (FLAT32,), residual=x)
    return y, k, v, logf


def _odd_decode(x, norm_w, w_in, w_fl, b_forget, w_out, k_cache, v_cache, lf_cache, layer, batch, seq):
    n, d = x.shape
    heads = d // HEAD_DIM
    hn = rmsnorm(x, norm_w, BF16, n)
    (cols,) = proj([hn], w_in, 0, 4 * d, n, DECODE_PROJ_COLS, (FLAT32,))
    (fl,) = proj([hn], w_fl, 0, heads, n, heads, (FLAT32,))
    logf = logf_only(fl, b_forget)
    o = fox_decode(cols, cols, cols, k_cache, v_cache, layer, cols, logf, lf_cache, batch, seq, BF16,
                   col0=(0, d, 2 * d, 3 * d))
    (y,) = proj([o], w_out, 0, d, n, DECODE_PROJ_COLS, (FLAT32,), residual=x)
    return y, cols[:, d:2 * d], cols[:, 2 * d:3 * d], logf


def kernel(x_prompt, x_sample, state_a_hgrn, cache_b_k, cache_b_v, cache_c_k, cache_c_v, cache_c_logf,
           norm_w, final_norm_w, w_in_even, w_out_even, lb_logits, a_norm_w, w_in_odd, b_forget, w_out_odd):
    bp, tp, d = x_prompt.shape
    bs, ts, _ = x_sample.shape
    assert bp == 1
    depth = norm_w.shape[0]
    n_even = w_in_even.shape[0]
    lb_all = jnp.cumsum(jax.nn.softmax(lb_logits.astype(F32), axis=0), axis=0)[:n_even]

    hp = x_prompt.reshape(bp * tp, d)
    hs = x_sample.reshape(bs * ts, d)
    outs = {name: [] for name in ("sa_p", "sa_s", "bk_p", "bv_p", "bk_s", "bv_s",
                                  "ck_p", "cv_p", "cf_p", "ck_s", "cv_s", "cf_s")}
    for layer in range(depth):
        j = layer // 2
        if layer % 2 == 0:
            a_heads = state_a_hgrn.shape[2]
            b_heads = cache_b_k.shape[3]
            zeros = jnp.zeros((bp, a_heads) + state_a_hgrn.shape[3:], F32)
            w_out = round_weights(w_out_even, j, ROUND_ROWS)
            hs, ss, ksm, vsm, w_in = _even_decode(hs, norm_w[layer], w_in_even, j, w_out, lb_all[j], a_norm_w[j],
                                                  state_a_hgrn[j], cache_b_k, cache_b_v, bs, ts)
            hp, sp, kp, vp = _even_prompt(hp, norm_w[layer], w_in, w_out, lb_all[j], a_norm_w[j], zeros, tp)
            outs["sa_p"].append(sp); outs["sa_s"].append(ss)
            outs["bk_p"].append(kp.reshape(bp, tp, b_heads, HEAD_DIM))
            outs["bv_p"].append(vp.reshape(bp, tp, b_heads, HEAD_DIM))
            outs["bk_s"].append(ksm.reshape(bs, ts, b_heads, HEAD_DIM))
            outs["bv_s"].append(vsm.reshape(bs, ts, b_heads, HEAD_DIM))
        else:
            c_heads = cache_c_k.shape[3]
            w_in = w_in_odd[j].astype(BF16)
            w_fl = w_in[:, 4 * c_heads * HEAD_DIM:]
            w_out = round_weights(w_out_odd, j, ROUND_ROWS)
            hp, kp, vp, fp = _odd_prompt(hp, norm_w[layer], w_in, w_fl, b_forget[j], w_out, tp)
            hs, ksm, vsm, fsm = _odd_decode(hs, norm_w[layer], w_in, w_fl, b_forget[j], w_out,
                                            cache_c_k, cache_c_v, cache_c_logf, j, bs, ts)
            outs["ck_p"].append(kp.reshape(bp, tp, c_heads, HEAD_DIM))
            outs["cv_p"].append(vp.reshape(bp, tp, c_heads, HEAD_DIM))
            outs["cf_p"].append(fp.reshape(bp, tp, c_heads))
            outs["ck_s"].append(ksm.reshape(bs, ts, c_heads, HEAD_DIM))
            outs["cv_s"].append(vsm.reshape(bs, ts, c_heads, HEAD_DIM))
            outs["cf_s"].append(fsm.reshape(bs, ts, c_heads))
    y_prompt = rmsnorm(hp, final_norm_w, F32, NORM_ROWS).reshape(bp, tp, d)
    y_sample = rmsnorm(hs, final_norm_w, F32, bs * ts).reshape(bs, ts, d)
    st = {k: jnp.stack(v) for k, v in outs.items()}
    return (y_prompt, y_sample, st["sa_p"], st["sa_s"], st["bk_p"], st["bv_p"], st["bk_s"], st["bv_s"],
            st["ck_p"], st["cv_p"], st["cf_p"], st["ck_s"], st["cv_s"], st["cf_s"])
```

```python
import functools

import numpy as np
import jax
import jax.numpy as jnp
from jax import lax
from jax.experimental import pallas as pl
from jax.experimental.pallas import tpu as pltpu

F32 = jnp.float32
BF16 = jnp.bfloat16

EPS = 1e-6
HEAD_DIM = 128
LANES = 128
LOG2E = 1.4426950408889634
F32_UNDERFLOW_LOG2 = 150.0
HGRN_CHUNK = 64
VMEM_LIMIT_BYTES = 56 * 1024 * 1024

PROJ_TILE = 1024
DECODE_PROJ_COLS = 512
NORM_ROWS = 256
ROUND_ROWS = 512
ATTN_ROWS = 1024
SB_KEYS = 256
FOX_KEYS = 512
FOX_ROW_CHUNK = 256
HGRN_ROWS = 1024
HGRN_HEADS = 2
CUMSUM_ROWS = 512

_NT = (((1,), (1,)), ((), ()))
_TN = (((0,), (0,)), ((), ()))


def _params(*sem):
    return pltpu.CompilerParams(dimension_semantics=sem, vmem_limit_bytes=VMEM_LIMIT_BYTES)


def _dot(a, b):
    return jnp.dot(a, b, preferred_element_type=F32)


def _dot_nt(a, b):
    return lax.dot_general(a, b, _NT, preferred_element_type=F32)


def _dot_tn(a, b):
    return lax.dot_general(a, b, _TN, preferred_element_type=F32)


def _split3(x):
    hi = x.astype(BF16)
    r1 = x - hi.astype(F32)
    mid = r1.astype(BF16)
    lo = (r1 - mid.astype(F32)).astype(BF16)
    return hi, mid, lo


def _dot_exact_lhs01(a01, x):
    hi, mid, lo = _split3(x)
    return _dot(a01, hi) + _dot(a01, mid) + _dot(a01, lo)


def _dot_exact_lhs01x3(a01x3, x):
    return _dot(a01x3, jnp.concatenate(_split3(x), axis=0))


def _sigmoid_pair(z):
    e = jnp.exp(-jnp.abs(z))
    r = 1.0 / (1.0 + e)
    er = e * r
    pos = z >= 0
    return jnp.where(pos, r, er), jnp.where(pos, er, r)


def _silu(x):
    return x * _sigmoid_pair(x)[0]


def _log_sigmoid(x):
    return jnp.minimum(x, 0.0) - jnp.log(1.0 + jnp.exp(-jnp.abs(x)))


def _rmsnorm_body(x_ref, w_ref, o_ref):
    x = x_ref[...]
    ms = jnp.mean(x * x, axis=-1, keepdims=True)
    o_ref[...] = (x * lax.rsqrt(ms + EPS) * w_ref[...]).astype(o_ref.dtype)


def rmsnorm(x, w, out_dtype, tm):
    m, d = x.shape
    return pl.pallas_call(
        _rmsnorm_body,
        grid=(m // tm,),
        in_specs=[pl.BlockSpec((tm, d), lambda i: (i, 0)), pl.BlockSpec((1, d), lambda i: (0, 0))],
        out_specs=pl.BlockSpec((tm, d), lambda i: (i, 0)),
        out_shape=jax.ShapeDtypeStruct((m, d), out_dtype),
        compiler_params=_params("parallel"),
        name="rmsnorm",
    )(x, w.reshape(1, d))


def _round_body(w_ref, o_ref):
    o_ref[...] = w_ref[...].astype(o_ref.dtype)


def round_weights(w, layer, rows):
    _, k, n = w.shape
    return pl.pallas_call(
        _round_body,
        grid=(k // rows,),
        in_specs=[pl.BlockSpec((None, rows, n), lambda i: (layer, i, 0))],
        out_specs=pl.BlockSpec((rows, n), lambda i: (i, 0)),
        out_shape=jax.ShapeDtypeStruct((k, n), BF16),
        compiler_params=_params("parallel"),
        name="round_weights",
    )(w)


def _round_proj_body(a_ref, w_ref, y_ref, wb_ref, *, w_transposed):
    wb = w_ref[...].astype(BF16)
    wb_ref[...] = wb
    y_ref[...] = _dot_nt(a_ref[...], wb) if w_transposed else _dot(a_ref[...], wb)


def round_and_proj(a, w, layer, ncols, tn, w_transposed=False):
    m, kk = a.shape
    if w_transposed:
        w_spec = pl.BlockSpec((None, tn, kk), lambda j: (layer, j, 0))
        wb_spec, wb_shape = pl.BlockSpec((tn, kk), lambda j: (j, 0)), (ncols, kk)
    else:
        w_spec = pl.BlockSpec((None, kk, tn), lambda j: (layer, 0, j))
        wb_spec, wb_shape = pl.BlockSpec((kk, tn), lambda j: (0, j)), (kk, ncols)
    return pl.pallas_call(
        functools.partial(_round_proj_body, w_transposed=w_transposed),
        grid=(pl.cdiv(ncols, tn),),
        in_specs=[pl.BlockSpec((m, kk), lambda j: (0, 0)), w_spec],
        out_specs=[pl.BlockSpec((m, tn), lambda j: (0, j)), wb_spec],
        out_shape=[jax.ShapeDtypeStruct((m, ncols), F32), jax.ShapeDtypeStruct(wb_shape, BF16)],
        compiler_params=_params("parallel"),
        name="round_and_proj",
    )(a, w)


def _proj_body(*refs, n_in, has_residual, outs, w_transposed):
    a_refs = refs[:n_in]
    w_refs = refs[n_in:2 * n_in]
    pos = 2 * n_in
    r_ref = refs[pos] if has_residual else None
    o_refs = refs[pos + int(has_residual):]
    acc = None
    for a_ref, w_ref in zip(a_refs, w_refs):
        d = _dot_nt(a_ref[...], w_ref[...]) if w_transposed else _dot(a_ref[...], w_ref[...])
        acc = d if acc is None else acc + d
    if has_residual:
        acc = r_ref[...] + acc
    for o_ref, (_, scale) in zip(o_refs, outs):
        val = acc if scale == 1.0 else acc * scale
        o_ref[...] = val.astype(o_ref.dtype)


def proj(a_list, w, col0, ncols, tm, tn, outs, residual=None, w_transposed=False):
    m = a_list[0].shape[0]
    kk = a_list[0].shape[1]
    assert all(a.shape == (m, kk) for a in a_list)
    assert m % tm == 0 and ncols % tn == 0 and col0 % tn == 0
    cb = col0 // tn
    in_specs = [pl.BlockSpec((tm, kk), lambda i, j: (i, 0)) for _ in a_list]
    if w_transposed:
        assert len(a_list) == 1 and w.shape[1] == kk
        in_specs.append(pl.BlockSpec((tn, kk), lambda i, j: (cb + j, 0)))
    else:
        assert w.shape[0] == kk * len(a_list)
        in_specs += [pl.BlockSpec((kk, tn), functools.partial(lambda i, j, r: (r, cb + j), r=r))
                     for r in range(len(a_list))]
    args = list(a_list) + [w] * len(a_list)
    if residual is not None:
        in_specs.append(pl.BlockSpec((tm, tn), lambda i, j: (i, j)))
        args.append(residual)
    out_specs = [pl.BlockSpec((tm, tn), lambda i, j: (i, j)) for _ in outs]
    out_shape = [jax.ShapeDtypeStruct((m, ncols), dtype) for dtype, _ in outs]
    return pl.pallas_call(
        functools.partial(_proj_body, n_in=len(a_list), has_residual=residual is not None, outs=tuple(outs),
                          w_transposed=w_transposed),
        grid=(m // tm, ncols // tn),
        in_specs=in_specs,
        out_specs=out_specs,
        out_shape=out_shape,
        compiler_params=_params("parallel", "parallel"),
        name="proj",
    )(*args)


def _hgrn_maps(c):
    levels = int(np.log2(c))
    assert 2 ** levels == c
    t = np.arange(c)[:, None]
    s = np.arange(c)[None, :]
    mats = [(s <= t), (s > t)]
    for l in range(levels):
        b = 2 ** l
        start = (t // (2 * b)) * (2 * b)
        upper = (t // b) % 2 == 1
        mats.append((upper & (s >= start + b) & (s <= t)) | ((~upper) & (s > t) & (s <= start + b - 1)))
    return np.concatenate(mats, axis=0).astype(np.float32), levels


def _hgrn_body(q_ref, z_ref, v_ref, g_ref, lb_ref, nw_ref, a_ref, s0_ref, o_ref, sout_ref, st_scr,
               *, c, n_chunks, levels, heads):
    tb = pl.program_id(2)

    @pl.when(tb == 0)
    def _():
        for hh in range(heads):
            st_scr[hh] = s0_ref[0, hh].T

    nw = nw_ref[...]
    amat = a_ref[...]
    row = lax.broadcasted_iota(jnp.int32, (c, c), 0)
    col = lax.broadcasted_iota(jnp.int32, (c, c), 1)
    xor = row ^ col

    units = [(slice(ci * c, (ci + 1) * c), slice(hh * HEAD_DIM, (hh + 1) * HEAD_DIM))
             for hh in range(heads) for ci in range(n_chunks)]
    nu = len(units)
    lbs = [lb_ref[:, cols] for _, cols in units]
    qs = [_silu(q_ref[rows, cols]) for rows, cols in units]
    sigs = [_sigmoid_pair(z_ref[rows, cols]) for rows, cols in units]
    gs = [jnp.log(lbs[u] + (1.0 - lbs[u]) * sigs[u][0]) for u in range(nu)]
    ks = [(1.0 - lbs[u]) * sigs[u][1] for u in range(nu)]
    vs = [v_ref[rows, cols] for rows, cols in units]
    vbs = [v.astype(BF16) for v in vs]
    es = [jnp.exp(_dot_exact_lhs01x3(amat, g)) for g in gs]
    atts = [None] * nu
    for l in range(levels - 1, -1, -1):
        for u in range(nu):
            el = es[u][(2 + l) * c:(3 + l) * c]
            al = _dot_nt((qs[u] * el).astype(BF16), (ks[u] * el).astype(BF16))
            atts[u] = al if atts[u] is None else jnp.where(xor < 2 ** (l + 1), al, atts[u])
    atts = [jnp.where(row > col, att, 0.0).astype(BF16) for att in atts]
    o_intras = [_dot(atts[u], vbs[u]) + jnp.sum(qs[u] * ks[u], axis=-1, keepdims=True) * vs[u] for u in range(nu)]
    upds = [_dot_tn(vbs[u], (ks[u] * es[u][c:2 * c]).astype(BF16)) for u in range(nu)]
    qgs = [(qs[u] * es[u][0:c]).astype(BF16) for u in range(nu)]

    for hh in range(heads):
        st = st_scr[hh]
        for ci in range(n_chunks):
            u = hh * n_chunks + ci
            rows, cols = units[u]
            o = _dot_nt(qgs[u], st.astype(BF16)) + o_intras[u]
            st = st * es[u][c - 1:c] + upds[u]
            ms = jnp.mean(o * o, axis=-1, keepdims=True)
            y = o * lax.rsqrt(ms + EPS) * nw
            o_ref[rows, cols] = (y * _silu(g_ref[rows, cols])).astype(o_ref.dtype)
        st_scr[hh] = st

    @pl.when(tb == pl.num_programs(2) - 1)
    def _():
        for hh in range(heads):
            sout_ref[0, hh] = st_scr[hh].T


def hgrn2(qa, fa, ia, ga, lb, a_norm_w, s0, batch, seq, rows_per_step, c, heads_per_step, out_dtype,
          col0=(0, 0, 0, 0)):
    n = qa.shape[0]
    width = lb.shape[0]
    h = width // HEAD_DIM
    assert n == batch * seq and seq % rows_per_step == 0 and rows_per_step % c == 0 and h % heads_per_step == 0
    nb = seq // rows_per_step
    gw = heads_per_step * HEAD_DIM
    amat_np, levels = _hgrn_maps(c)
    amat = jnp.asarray(np.concatenate([amat_np] * 3, axis=1), BF16)
    row_spec = pl.BlockSpec((rows_per_step, gw), lambda b, hh, t: (b * nb + t, hh))
    in_row_specs = [pl.BlockSpec((rows_per_step, gw), functools.partial(lambda b, hh, t, off: (b * nb + t, off + hh), off=c0 // gw))
                    for c0 in col0]
    assert all(c0 % gw == 0 for c0 in col0)
    state_spec = pl.BlockSpec((1, heads_per_step, HEAD_DIM, HEAD_DIM), lambda b, hh, t: (b, hh, 0, 0))
    body = functools.partial(_hgrn_body, c=c, n_chunks=rows_per_step // c, levels=levels, heads=heads_per_step)
    return pl.pallas_call(
        body,
        grid=(batch, h // heads_per_step, nb),
        in_specs=in_row_specs + [
                  pl.BlockSpec((1, gw), lambda b, hh, t: (0, hh)),
                  pl.BlockSpec((1, HEAD_DIM), lambda b, hh, t: (0, 0)),
                  pl.BlockSpec(amat.shape, lambda b, hh, t: (0, 0)),
                  state_spec],
        out_specs=[row_spec, state_spec],
        out_shape=[jax.ShapeDtypeStruct((n, width), out_dtype),
                   jax.ShapeDtypeStruct((batch, h, HEAD_DIM, HEAD_DIM), F32)],
        scratch_shapes=[pltpu.VMEM((heads_per_step, HEAD_DIM, HEAD_DIM), F32)],
        compiler_params=_params("parallel", "parallel", "arbitrary"),
        name="hgrn2",
    )(qa, fa, ia, ga, lb.reshape(1, width), a_norm_w.reshape(1, HEAD_DIM), amat, s0)


def _sb_prompt_body(q_ref, k_ref, v_ref, g_ref, uu_ref, o_ref, run_scr, acc_scr, w_scr, *, tq, tk, rc):
    i = pl.program_id(1)
    nd = tq // tk
    nl = tk // LANES
    q0 = pl.multiple_of(i * tq, tq)
    run_scr[...] = jnp.zeros(run_scr.shape, F32)
    acc_scr[...] = jnp.zeros(acc_scr.shape, F32)

    def apply_pending(slot, k_prev, first_row):
        vt = v_ref[pl.ds(pl.multiple_of(k_prev, tk), tk), :]
        for r0 in range(first_row, tq, rc):
            rows = slice(r0, min(r0 + rc, tq))
            acc_scr[rows, :] = acc_scr[rows, :] + _dot(w_scr[slot, rows, :], vt)

    def score(slot, k0, first_row, masked):
        kt = k_ref[pl.ds(pl.multiple_of(k0, tk), tk), :]
        for r0 in range(first_row, tq, rc):
            r1 = min(r0 + rc, tq)
            rows = slice(r0, r1)
            z = _dot_nt(q_ref[rows, :], kt)
            if masked:
                qpos = lax.broadcasted_iota(jnp.int32, (r1 - r0, LANES), 0) + (q0 + r0)
                kpos = lax.broadcasted_iota(jnp.int32, (r1 - r0, LANES), 1) + k0
            zs, sps, his, los, valids = [], [], [], [], []
            rowsum = None
            for c in range(nl):
                zc = z[:, c * LANES:(c + 1) * LANES]
                sp = jnp.maximum(zc, 0.0) + jnp.log2(1.0 + jnp.exp2(-jnp.abs(zc)))
                if masked:
                    valid = (kpos + c * LANES) < qpos
                    sp = jnp.where(valid, sp, 0.0)
                    valids.append(valid)
                hi = sp.astype(BF16)
                lo = (sp - hi.astype(F32)).astype(BF16)
                zs.append(zc); sps.append(sp); his.append(hi); los.append(lo)
                rowsum = sp if rowsum is None else rowsum + sp
            tail = _dot(jnp.concatenate(his + los, axis=1), uu_ref[...])
            run = run_scr[rows, :]
            for c in range(nl):
                w = jnp.exp2(zs[c] - (sps[c] + tail[:, c * LANES:(c + 1) * LANES] + run))
                if masked:
                    w = jnp.where(valids[c], w, 0.0)
                w_scr[slot, rows, c * LANES:(c + 1) * LANES] = w.astype(BF16)
            run_scr[rows, :] = run + jnp.sum(rowsum, axis=-1, keepdims=True)

    assert nd % 2 == 0
    for d in range(nd - 1, 0, -2):
        score(0, q0 + d * tk, d * tk, True)
        if d + 1 < nd:
            apply_pending(1, q0 + (d + 1) * tk, (d + 1) * tk)
        score(1, q0 + (d - 1) * tk, (d - 1) * tk, True)
        apply_pending(0, q0 + d * tk, d * tk)

    def trip(carry):
        jj, kp, _ = carry
        k_a = q0 - (2 * jj + 1) * tk
        score(0, k_a, 0, False)
        apply_pending(1, kp, 0)
        score(1, k_a - tk, 0, False)
        more = jnp.min(run_scr[...]) < F32_UNDERFLOW_LOG2
        apply_pending(0, k_a, 0)
        return jj + 1, k_a - tk, more

    _, k_pending, _ = lax.while_loop(lambda carry: (carry[0] < (i * nd) // 2) & carry[2], trip,
                                     (jnp.int32(0), q0, jnp.bool_(True)))
    apply_pending(1, k_pending, 0)
    o_ref[...] = (acc_scr[...] * _silu(g_ref[...])).astype(o_ref.dtype)


def _tail_matrix(n):
    sp = np.arange(n)[:, None]
    s = np.arange(n)[None, :]
    return (sp > s).astype(np.float32)


def sb_prompt(q, k, v, gate, tq, tk, rc, out_dtype):
    t, width = q.shape
    h = width // HEAD_DIM
    u = _tail_matrix(tk)
    uu = jnp.asarray(np.concatenate([u, u], axis=0), BF16)
    q_spec = pl.BlockSpec((tq, HEAD_DIM), lambda hh, i: (i, hh))
    kv_spec = pl.BlockSpec((t, HEAD_DIM), lambda hh, i: (0, hh))
    return pl.pallas_call(
        functools.partial(_sb_prompt_body, tq=tq, tk=tk, rc=rc),
        grid=(h, t // tq),
        in_specs=[q_spec, kv_spec, kv_spec, q_spec, pl.BlockSpec(uu.shape, lambda hh, i: (0, 0))],
        out_specs=q_spec,
        out_shape=jax.ShapeDtypeStruct((t, width), out_dtype),
        scratch_shapes=[pltpu.VMEM((tq, LANES), F32), pltpu.VMEM((tq, HEAD_DIM), F32),
                        pltpu.VMEM((2, tq, tk), BF16)],
        compiler_params=_params("parallel", "arbitrary"),
        name="sb_prompt",
    )(q, k, v, gate, uu)


def _logf_cumsum_body(fl_ref, b_ref, tri_ref, lf_ref, f2_ref, carry_scr):
    @pl.when(pl.program_id(0) == 0)
    def _():
        carry_scr[...] = jnp.zeros_like(carry_scr)

    lf = _log_sigmoid(fl_ref[...] + b_ref[...])
    lf_ref[...] = lf
    f = carry_scr[...] + _dot_exact_lhs01(tri_ref[...], lf)
    f2_ref[...] = f * LOG2E
    carry_scr[...] = f[f.shape[0] - 1:, :]


def logf_cumsum(fl, b_forget, blk):
    t, h = fl.shape
    tri = jnp.asarray(np.tril(np.ones((blk, blk), np.float32)), BF16)
    spec = pl.BlockSpec((blk, h), lambda i: (i, 0))
    return pl.pallas_call(
        _logf_cumsum_body,
        grid=(t // blk,),
        in_specs=[spec, pl.BlockSpec((1, h), lambda i: (0, 0)), pl.BlockSpec((blk, blk), lambda i: (0, 0))],
        out_specs=[spec, spec],
        out_shape=[jax.ShapeDtypeStruct((t, h), F32), jax.ShapeDtypeStruct((t, h), F32)],
        scratch_shapes=[pltpu.VMEM((1, h), F32)],
        compiler_params=_params("arbitrary"),
        name="logf_cumsum",
    )(fl, b_forget.reshape(1, h), tri)


def _logf_body(fl_ref, b_ref, lf_ref):
    lf_ref[...] = _log_sigmoid(fl_ref[...] + b_ref[...])


def logf_only(fl, b_forget):
    t, h = fl.shape
    return pl.pallas_call(
        _logf_body,
        grid=(1,),
        in_specs=[pl.BlockSpec((t, h), lambda i: (0, 0)), pl.BlockSpec((1, h), lambda i: (0, 0))],
        out_specs=pl.BlockSpec((t, h), lambda i: (0, 0)),
        out_shape=jax.ShapeDtypeStruct((t, h), F32),
        name="logf",
    )(fl, b_forget.reshape(1, h))


_BIAS_PIECES = 3


def _bias_selectors(h):
    sel = np.zeros((2, h, _BIAS_PIECES, h, LANES), np.float32)
    for hh in range(h):
        for p in range(_BIAS_PIECES):
            sel[0, hh, p, hh, p] = 1.0
            sel[1, hh, p, hh, _BIAS_PIECES + p] = -1.0
    return sel


def _bias_columns(f, sel_ref, query_side):
    out = None
    for p, piece in enumerate(_split3(f)):
        d = _dot(piece, sel_ref[0, p])
        out = d if out is None else out + d
    lane = lax.broadcasted_iota(jnp.int32, out.shape, 1)
    ones_at = (lane >= _BIAS_PIECES) & (lane < 2 * _BIAS_PIECES) if query_side else lane < _BIAS_PIECES
    return jnp.where(ones_at, 1.0, out).astype(BF16)


def _fox_prompt_body(q_ref, k_ref, v_ref, g_ref, f_ref, selq_ref, selk_ref, o_ref,
                     kx_scr, vx_scr, m_scr, acc_scr, p_scr, alpha_scr, kmax_scr, ub_scr, *, tq, tk, rc):
    i = pl.program_id(1)
    nd = tq // tk
    nl = tk // LANES
    t_all = k_ref.shape[0]

    head = pl.program_id(0)

    def own_column(f_rows):
        lane = lax.broadcasted_iota(jnp.int32, f_rows.shape, 1)
        return jnp.sum(jnp.where(lane == head, f_rows, 0.0), axis=-1, keepdims=True)

    @pl.when(i == 0)
    def _():
        kb = k_ref[...].astype(BF16)
        kx_scr[:, 0:HEAD_DIM] = kb
        kx_scr[:, HEAD_DIM:] = _bias_columns(f_ref[...], selk_ref, False)
        vx_scr[:, 0:HEAD_DIM] = v_ref[...].astype(BF16)
        vx_scr[:, HEAD_DIM:] = jnp.ones((t_all, LANES), BF16)
        kf = kb.astype(F32)
        knorm2 = jnp.max(jnp.sum(kf * kf, axis=-1, keepdims=True), axis=0, keepdims=True)
        kmax_scr[...] = jnp.broadcast_to(jnp.sqrt(knorm2), kmax_scr.shape)

    q0 = pl.multiple_of(i * tq, tq)
    qx = jnp.concatenate([q_ref[...], _bias_columns(f_ref[pl.ds(q0, tq), :], selq_ref, True)], axis=1)
    m_scr[...] = jnp.full(m_scr.shape, -jnp.inf, F32)
    acc_scr[...] = jnp.zeros(acc_scr.shape, F32)
    qf = q_ref[...].astype(F32)
    qnorm = jnp.sqrt(jnp.sum(qf * qf, axis=-1, keepdims=True))
    ub_scr[...] = qnorm * kmax_scr[...] * (1.0 + 2.0 ** -10) + own_column(f_ref[pl.ds(q0, tq), :])

    def apply_pending(slot, k_prev, first_row):
        vxt = vx_scr[pl.ds(pl.multiple_of(k_prev, tk), tk), :]
        for r0 in range(first_row, tq, rc):
            rows = slice(r0, r0 + rc)
            pv = _dot(p_scr[slot, rows, :], vxt)
            alpha = alpha_scr[slot, rows, :]
            acc_scr[rows, 0:HEAD_DIM] = alpha * acc_scr[rows, 0:HEAD_DIM] + pv[:, 0:HEAD_DIM]
            acc_scr[rows, HEAD_DIM:] = alpha * acc_scr[rows, HEAD_DIM:] + pv[:, HEAD_DIM:]

    def score(slot, k0, first_row, masked):
        kxt = kx_scr[pl.ds(pl.multiple_of(k0, tk), tk), :]
        for r0 in range(first_row, tq, rc):
            rows = slice(r0, r0 + rc)
            s = _dot_nt(qx[r0:r0 + rc, :], kxt)
            sb = [s[:, c * LANES:(c + 1) * LANES] for c in range(nl)]
            if masked:
                qpos = lax.broadcasted_iota(jnp.int32, (rc, LANES), 0) + (q0 + r0)
                kpos = lax.broadcasted_iota(jnp.int32, (rc, LANES), 1) + k0
                sb = [jnp.where((kpos + c * LANES) <= qpos, sb[c], -jnp.inf) for c in range(nl)]
            mx = sb[0]
            for c in range(1, nl):
                mx = jnp.maximum(mx, sb[c])
            m_old = m_scr[rows, :]
            m_new = jnp.maximum(m_old, jnp.max(mx, axis=-1, keepdims=True))
            alpha_scr[slot, rows, :] = jnp.exp2(m_old - m_new)
            for c in range(nl):
                p_scr[slot, rows, c * LANES:(c + 1) * LANES] = jnp.exp2(sb[c] - m_new).astype(BF16)
            m_scr[rows, :] = m_new

    assert nd % 2 == 0 and tk % rc == 0
    for d in range(0, nd, 2):
        score(0, q0 + d * tk, d * tk, True)
        if d > 0:
            apply_pending(1, q0 + (d - 1) * tk, (d - 1) * tk)
        score(1, q0 + (d + 1) * tk, (d + 1) * tk, True)
        apply_pending(0, q0 + d * tk, d * tk)

    p_scr[1, 0:(nd - 1) * tk, :] = jnp.zeros(((nd - 1) * tk, tk), BF16)
    alpha_scr[1, 0:(nd - 1) * tk, :] = jnp.ones(((nd - 1) * tk, LANES), F32)

    def pair(k_a, kp):
        score(0, k_a, 0, False)
        apply_pending(1, kp, 0)
        score(1, k_a - tk, 0, False)
        more = visible(k_a - tk - 1)
        apply_pending(0, k_a, 0)
        return k_a - tk, more

    def visible(k_hi):
        f_hi = own_column(f_ref[pl.ds(jnp.maximum(k_hi, 0), 1), :])
        return jnp.max(ub_scr[...] - f_hi - m_scr[...]) > -(F32_UNDERFLOW_LOG2 + 2.0)

    def trip(carry):
        jj, kp, _ = carry
        kp, more = pair(q0 - (2 * jj + 1) * tk, kp)
        return jj + 1, kp, more

    n_pairs = (i * nd) // 2
    _, k_pending, _ = lax.while_loop(lambda carry: (carry[0] < n_pairs) & carry[2], trip,
                                     (jnp.int32(0), q0 + (nd - 1) * tk, visible(q0 - 1)))
    apply_pending(1, k_pending, 0)
    o_ref[...] = (acc_scr[:, 0:HEAD_DIM] / acc_scr[:, HEAD_DIM:] * _silu(g_ref[...])).astype(o_ref.dtype)


def fox_prompt(q, k, v, gate, f2, tq, tk, rc, out_dtype):
    t, width = q.shape
    h = width // HEAD_DIM
    sel = jnp.asarray(_bias_selectors(h), BF16)
    q_spec = pl.BlockSpec((tq, HEAD_DIM), lambda hh, i: (i, hh))
    kv_spec = pl.BlockSpec((t, HEAD_DIM), lambda hh, i: (0, hh))
    sel_spec = pl.BlockSpec((1, _BIAS_PIECES, h, LANES), lambda hh, i: (hh, 0, 0, 0))
    return pl.pallas_call(
        functools.partial(_fox_prompt_body, tq=tq, tk=tk, rc=rc),
        grid=(h, t // tq),
        in_specs=[q_spec, kv_spec, kv_spec, q_spec, pl.BlockSpec((t, h), lambda hh, i: (0, 0)),
                  sel_spec, sel_spec],
        out_specs=q_spec,
        out_shape=jax.ShapeDtypeStruct((t, width), out_dtype),
        scratch_shapes=[pltpu.VMEM((t, 2 * HEAD_DIM), BF16), pltpu.VMEM((t, 2 * HEAD_DIM), BF16),
                        pltpu.VMEM((tq, LANES), F32), pltpu.VMEM((tq, 2 * HEAD_DIM), F32),
                        pltpu.VMEM((2, tq, tk), BF16), pltpu.VMEM((2, tq, LANES), F32),
                        pltpu.VMEM((1, LANES), F32), pltpu.VMEM((tq, LANES), F32)],
        compiler_params=_params("parallel", "arbitrary"),
        name="fox_prompt",
    )(q, k, v, gate, f2, sel[0], sel[1])


HEADS_PER_GROUP = 8


def _expand_queries(q, tq):
    gw = q.shape[1]
    rep = jnp.concatenate([q] * HEADS_PER_GROUP, axis=0)
    r = lax.broadcasted_iota(jnp.int32, (HEADS_PER_GROUP * tq, gw), 0) // tq
    cidx = lax.broadcasted_iota(jnp.int32, (HEADS_PER_GROUP * tq, gw), 1) // HEAD_DIM
    return jnp.where(r == cidx, rep, 0.0)


def _collect_heads(full, tq):
    return jnp.concatenate(
        [full[hh * tq:(hh + 1) * tq, hh * HEAD_DIM:(hh + 1) * HEAD_DIM] for hh in range(HEADS_PER_GROUP)],
        axis=1)


def _gather_keys(cache_ref, new_ref, scr, past, tk):
    g = HEADS_PER_GROUP
    x = cache_ref[0].reshape(past // g, g, g, HEAD_DIM)
    x = jnp.swapaxes(x, 1, 2)
    for hh in range(g):
        scr[0:past, hh * HEAD_DIM:(hh + 1) * HEAD_DIM] = x[:, hh].reshape(past, HEAD_DIM).astype(BF16)
    scr[past:tk, :] = new_ref[...].astype(BF16)


def _row_blocks(n):
    return [(r0, min(r0 + LANES, n)) for r0 in range(0, n, LANES)]


def _cumsum_rows(pieces, tri_ref):
    n = pieces[0].shape[0]
    out, carry = [], None
    for r0, r1 in _row_blocks(n):
        tri = tri_ref[0:r1 - r0, 0:r1 - r0]
        local = None
        for piece in pieces:
            d = _dot(tri, piece[r0:r1])
            local = d if local is None else local + d
        if carry is not None:
            local = local + carry
        carry = local[r1 - r0 - 1:r1 - r0]
        out.append(local)
    return jnp.concatenate(out, axis=0)


def _tailsum_rows(pieces, tri_ref):
    n = pieces[0].shape[0]
    out, carry = [], None
    for r0, r1 in reversed(_row_blocks(n)):
        tri = tri_ref[0:r1 - r0, 0:r1 - r0]
        local, total = None, None
        for piece in pieces:
            blk = piece[r0:r1]
            d = _dot_tn(tri, blk)
            local = d if local is None else local + d
            t = blk.astype(F32)
            total = t if total is None else total + t
        local = local - total
        if carry is not None:
            local = local + carry
        carry = local[0:1] + total[0:1]
        out.append(local)
    return jnp.concatenate(out[::-1], axis=0)


def _sb_decode_body(q_ref, kn_ref, vn_ref, kc_ref, vc_ref, g_ref, tri_ref, o_ref, k_scr, v_scr, *, past, tq):
    tk = past + tq
    _gather_keys(kc_ref, kn_ref, k_scr, past, tk)
    _gather_keys(vc_ref, vn_ref, v_scr, past, tk)

    lanes = HEADS_PER_GROUP * tq
    qx = _expand_queries(q_ref[...] * (HEAD_DIM ** -0.5), tq).astype(BF16)
    z = _dot_nt(k_scr[...], qx)
    kpos = lax.broadcasted_iota(jnp.int32, (tk, lanes), 0)
    qpos = past + lax.broadcasted_iota(jnp.int32, (tk, lanes), 1) % tq
    valid = kpos < qpos
    ls_neg = -(jnp.maximum(z, 0.0) + jnp.log(1.0 + jnp.exp(-jnp.abs(z))))
    lm = jnp.where(valid, ls_neg, 0.0)
    hi = lm.astype(BF16)
    lo = (lm - hi.astype(F32)).astype(BF16)
    tail = _tailsum_rows((hi, lo), tri_ref)
    w = jnp.where(valid, jnp.exp(z + ls_neg + tail), 0.0)
    full = _dot_tn(w.astype(BF16), v_scr[...])
    o_ref[...] = (_collect_heads(full, tq) * _silu(g_ref[...])).astype(o_ref.dtype)


def _cache_spec(layer, past):
    return pl.BlockSpec((None, 1, past, HEADS_PER_GROUP, HEAD_DIM), lambda b, g: (layer, b, 0, g, 0))


def _decode_row_specs(tq, gw, col0):
    assert all(c0 % gw == 0 for c0 in col0)
    return [pl.BlockSpec((tq, gw), functools.partial(lambda b, g, off: (b, off + g), off=c0 // gw)) for c0 in col0]


def sb_decode(q, k_new, v_new, k_cache, v_cache, layer, gate, batch, tq, out_dtype, col0=(0, 0, 0, 0)):
    n = q.shape[0]
    width = k_cache.shape[3] * HEAD_DIM
    past = k_cache.shape[2]
    tk = past + tq
    gw = HEADS_PER_GROUP * HEAD_DIM
    groups = width // gw
    tri = jnp.asarray(np.tril(np.ones((LANES, LANES), np.float32)), BF16)
    row_spec = pl.BlockSpec((tq, gw), lambda b, g: (b, g))
    rs = _decode_row_specs(tq, gw, col0)
    cache_spec = _cache_spec(layer, past)
    return pl.pallas_call(
        functools.partial(_sb_decode_body, past=past, tq=tq),
        grid=(batch, groups),
        in_specs=[rs[0], rs[1], rs[2], cache_spec, cache_spec, rs[3],
                  pl.BlockSpec((LANES, LANES), lambda b, g: (0, 0))],
        out_specs=row_spec,
        out_shape=jax.ShapeDtypeStruct((n, width), out_dtype),
        scratch_shapes=[pltpu.VMEM((tk, gw), BF16), pltpu.VMEM((tk, gw), BF16)],
        compiler_params=_params("parallel", "parallel"),
        name="sb_decode",
    )(q, k_new, v_new, k_cache, v_cache, gate, tri)


def _fox_decode_body(q_ref, kn_ref, vn_ref, kc_ref, vc_ref, g_ref, lfn_ref, lfc_ref, ex_ref, tri_ref, o_ref,
                     k_scr, v_scr, lf_scr, *, past, tq):
    tk = past + tq
    _gather_keys(kc_ref, kn_ref, k_scr, past, tk)
    _gather_keys(vc_ref, vn_ref, v_scr, past, tk)
    lf_scr[0:past, :] = lfc_ref[0]
    lf_scr[past:tk, :] = lfn_ref[...]

    lanes = HEADS_PER_GROUP * tq
    hi, mid, lo = _split3(lf_scr[...])
    ex = ex_ref[0]
    lfx = _dot(hi, ex) + _dot(mid, ex) + _dot(lo, ex)
    f_key = _cumsum_rows(_split3(lfx), tri_ref)
    kpos = lax.broadcasted_iota(jnp.int32, (tk, lanes), 0)
    qpos = past + lax.broadcasted_iota(jnp.int32, (tk, lanes), 1) % tq
    f_query = jnp.sum(jnp.where(kpos == qpos, f_key, 0.0), axis=0, keepdims=True)

    qx = _expand_queries(q_ref[...] * (HEAD_DIM ** -0.5), tq).astype(BF16)
    s = _dot_nt(k_scr[...], qx) + (f_query - f_key)
    s = jnp.where(kpos <= qpos, s, -jnp.inf)
    p = jnp.exp(s - jnp.max(s, axis=0, keepdims=True))
    p = p / jnp.sum(p, axis=0, keepdims=True)
    full = _dot_tn(p.astype(BF16), v_scr[...])
    o_ref[...] = (_collect_heads(full, tq) * _silu(g_ref[...])).astype(o_ref.dtype)


def fox_decode(q, k_new, v_new, k_cache, v_cache, layer, gate, lf_new, lf_cache, batch, tq, out_dtype,
               col0=(0, 0, 0, 0)):
    n = q.shape[0]
    h = k_cache.shape[3]
    width = h * HEAD_DIM
    past = k_cache.shape[2]
    tk = past + tq
    gw = HEADS_PER_GROUP * HEAD_DIM
    groups = width // gw
    lanes = HEADS_PER_GROUP * tq
    tri = jnp.asarray(np.tril(np.ones((LANES, LANES), np.float32)), BF16)
    head_of_lane = np.arange(lanes)[None, None, :] // tq + HEADS_PER_GROUP * np.arange(groups)[:, None, None]
    expand = jnp.asarray((np.arange(h)[None, :, None] == head_of_lane).astype(np.float32), BF16)
    row_spec = pl.BlockSpec((tq, gw), lambda b, g: (b, g))
    rs = _decode_row_specs(tq, gw, col0)
    cache_spec = _cache_spec(layer, past)
    return pl.pallas_call(
        functools.partial(_fox_decode_body, past=past, tq=tq),
        grid=(batch, groups),
        in_specs=[rs[0], rs[1], rs[2], cache_spec, cache_spec, rs[3],
                  pl.BlockSpec((tq, h), lambda b, g: (b, 0)),
                  pl.BlockSpec((None, 1, past, h), lambda b, g: (layer, b, 0, 0)),
                  pl.BlockSpec((1, h, lanes), lambda b, g: (g, 0, 0)),
                  pl.BlockSpec((LANES, LANES), lambda b, g: (0, 0))],
        out_specs=row_spec,
        out_shape=jax.ShapeDtypeStruct((n, width), out_dtype),
        scratch_shapes=[pltpu.VMEM((tk, gw), BF16), pltpu.VMEM((tk, gw), BF16), pltpu.VMEM((tk, h), F32)],
        compiler_params=_params("parallel", "parallel"),
        name="fox_decode",
    )(q, k_new, v_new, k_cache, v_cache, gate, lf_new, lf_cache, expand, tri)


QK_SCALE_LOG2 = LOG2E * HEAD_DIM ** -0.5
FLAT32 = (F32, 1.0)
FLAT16 = (BF16, 1.0)
QUERY16 = (BF16, QK_SCALE_LOG2)


def _even_prompt(x, norm_w, w_in, w_out, lb, a_norm_w, s0, seq):
    n, d = x.shape
    half = d // 2
    hn = rmsnorm(x, norm_w, BF16, NORM_ROWS)

    def p(group, *outs):
        return proj([hn], w_in, group * half, half, PROJ_TILE, PROJ_TILE, outs)

    (rec,) = proj([hn], w_in, 0, 4 * half, PROJ_TILE, PROJ_TILE, (FLAT32,))
    (qb,), (gb,) = p(4, QUERY16), p(7, FLAT32)
    kb_leaf, kb = p(5, FLAT32, FLAT16)
    vb_leaf, vb = p(6, FLAT32, FLAT16)
    oa, s_new = hgrn2(rec, rec, rec, rec, lb, a_norm_w, s0, 1, seq, HGRN_ROWS, HGRN_CHUNK, HGRN_HEADS, BF16,
                      col0=(0, half, 2 * half, 3 * half))
    ob = sb_prompt(qb, kb, vb, gb, ATTN_ROWS, SB_KEYS, ATTN_ROWS, BF16)
    (y,) = proj([oa, ob], w_out, 0, d, PROJ_TILE, PROJ_TILE, (FLAT32,), residual=x)
    return y, s_new, kb_leaf, vb_leaf


def _even_decode(x, norm_w, w_in_f32, layer, w_out, lb, a_norm_w, s0, k_cache, v_cache, batch, seq):
    n, d = x.shape
    half = d // 2
    hn = rmsnorm(x, norm_w, BF16, n)
    cols, w_in = round_and_proj(hn, w_in_f32, layer, 8 * half, DECODE_PROJ_COLS)
    oa, s_new = hgrn2(cols, cols, cols, cols, lb, a_norm_w, s0, batch, seq, seq, seq, half // HEAD_DIM, BF16,
                      col0=(0, half, 2 * half, 3 * half))
    ob = sb_decode(cols, cols, cols, k_cache, v_cache, layer, cols, batch, seq, BF16,
                   col0=(4 * half, 5 * half, 6 * half, 7 * half))
    (y,) = proj([oa, ob], w_out, 0, d, n, DECODE_PROJ_COLS, (FLAT32,), residual=x)
    return y, s_new, cols[:, 5 * half:6 * half], cols[:, 6 * half:7 * half], w_in


def _odd_prompt(x, norm_w, w_in, w_fl, b_forget, w_out, seq):
    n, d = x.shape
    heads = d // HEAD_DIM
    hn = rmsnorm(x, norm_w, BF16, NORM_ROWS)

    def p(group, *outs):
        return proj([hn], w_in, group * d, d, PROJ_TILE, PROJ_TILE, outs, w_transposed=True)

    (q,), (gate,) = p(0, QUERY16), p(3, FLAT32)
    (fl,) = proj([hn], w_fl, 0, heads, PROJ_TILE, heads, (FLAT32,), w_transposed=True)
    logf, f2 = logf_cumsum(fl, b_forget, CUMSUM_ROWS)
    (k,), (v,) = p(1, FLAT32), p(2, FLAT32)
    o = fox_prompt(q, k, v, gate, f2, ATTN_ROWS, FOX_KEYS, FOX_ROW_CHUNK, BF16)
    (y,) = proj([o], w_out, 0, d, PROJ_TILE, PROJ_TILE, (FLAT32,), residual=x)
    return y, k, v, logf


def _odd_decode(x, norm_w, w_in_t_f32, layer, b_forget, w_out, k_cache, v_cache, lf_cache, batch, seq):
    n, d = x.shape
    heads = d // HEAD_DIM
    hn = rmsnorm(x, norm_w, BF16, n)
    cols, w_in = round_and_proj(hn, w_in_t_f32, layer, 4 * d + heads, DECODE_PROJ_COLS, w_transposed=True)
    logf = logf_only(cols[:, 4 * d:], b_forget)
    o = fox_decode(cols, cols, cols, k_cache, v_cache, layer, cols, logf, lf_cache, batch, seq, BF16,
                   col0=(0, d, 2 * d, 3 * d))
    (y,) = proj([o], w_out, 0, d, n, DECODE_PROJ_COLS, (FLAT32,), residual=x)
    return y, cols[:, d:2 * d], cols[:, 2 * d:3 * d], logf, w_in


def kernel(x_prompt, x_sample, state_a_hgrn, cache_b_k, cache_b_v, cache_c_k, cache_c_v, cache_c_logf,
           norm_w, final_norm_w, w_in_even, w_out_even, lb_logits, a_norm_w, w_in_odd, b_forget, w_out_odd):
    bp, tp, d = x_prompt.shape
    bs, ts, _ = x_sample.shape
    assert bp == 1
    depth = norm_w.shape[0]
    n_even = w_in_even.shape[0]
    lb_all = jnp.cumsum(jax.nn.softmax(lb_logits.astype(F32), axis=0), axis=0)[:n_even]

    hp = x_prompt.reshape(bp * tp, d)
    hs = x_sample.reshape(bs * ts, d)
    outs = {name: [] for name in ("sa_p", "sa_s", "bk_p", "bv_p", "bk_s", "bv_s",
                                  "ck_p", "cv_p", "cf_p", "ck_s", "cv_s", "cf_s")}
    for layer in range(depth):
        j = layer // 2
        if layer % 2 == 0:
            a_heads = state_a_hgrn.shape[2]
            b_heads = cache_b_k.shape[3]
            zeros = jnp.zeros((bp, a_heads) + state_a_hgrn.shape[3:], F32)
            w_out = round_weights(w_out_even, j, ROUND_ROWS)
            hs, ss, ksm, vsm, w_in = _even_decode(hs, norm_w[layer], w_in_even, j, w_out, lb_all[j], a_norm_w[j],
                                                  state_a_hgrn[j], cache_b_k, cache_b_v, bs, ts)
            hp, sp, kp, vp = _even_prompt(hp, norm_w[layer], w_in, w_out, lb_all[j], a_norm_w[j], zeros, tp)
            outs["sa_p"].append(sp); outs["sa_s"].append(ss)
            outs["bk_p"].append(kp.reshape(bp, tp, b_heads, HEAD_DIM))
            outs["bv_p"].append(vp.reshape(bp, tp, b_heads, HEAD_DIM))
            outs["bk_s"].append(ksm.reshape(bs, ts, b_heads, HEAD_DIM))
            outs["bv_s"].append(vsm.reshape(bs, ts, b_heads, HEAD_DIM))
        else:
            c_heads = cache_c_k.shape[3]
            w_in_t = jnp.swapaxes(w_in_odd, 1, 2)
            w_out = round_weights(w_out_odd, j, ROUND_ROWS)
            hs, ksm, vsm, fsm, w_in = _odd_decode(hs, norm_w[layer], w_in_t, j, b_forget[j], w_out,
                                                  cache_c_k, cache_c_v, cache_c_logf, bs, ts)
            w_fl = w_in[4 * c_heads * HEAD_DIM:, :]
            hp, kp, vp, fp = _odd_prompt(hp, norm_w[layer], w_in, w_fl, b_forget[j], w_out, tp)
            outs["ck_p"].append(kp.reshape(bp, tp, c_heads, HEAD_DIM))
            outs["cv_p"].append(vp.reshape(bp, tp, c_heads, HEAD_DIM))
            outs["cf_p"].append(fp.reshape(bp, tp, c_heads))
            outs["ck_s"].append(ksm.reshape(bs, ts, c_heads, HEAD_DIM))
            outs["cv_s"].append(vsm.reshape(bs, ts, c_heads, HEAD_DIM))
            outs["cf_s"].append(fsm.reshape(bs, ts, c_heads))
    y_prompt = rmsnorm(hp, final_norm_w, F32, NORM_ROWS).reshape(bp, tp, d)
    y_sample = rmsnorm(hs, final_norm_w, F32, bs * ts).reshape(bs, ts, d)
    st = {k: jnp.stack(v) for k, v in outs.items()}
    return (y_prompt, y_sample, st["sa_p"], st["sa_s"], st["bk_p"], st["bv_p"], st["bk_s"], st["bv_s"],
            st["ck_p"], st["cv_p"], st["cf_p"], st["ck_s"], st["cv_s"], st["cf_s"])
```

```python
import functools

import numpy as np
import jax
import jax.numpy as jnp
from jax import lax
from jax.experimental import pallas as pl
from jax.experimental.pallas import tpu as pltpu

F32 = jnp.float32
BF16 = jnp.bfloat16

EPS = 1e-6
HEAD_DIM = 128
LANES = 128
LOG2E = 1.4426950408889634
F32_UNDERFLOW_LOG2 = 150.0
HGRN_CHUNK = 64
VMEM_LIMIT_BYTES = 56 * 1024 * 1024

PROJ_TILE = 1024
DECODE_PROJ_COLS = 512
NORM_ROWS = 256
ATTN_ROWS = 1024
SB_KEYS = 256
FOX_KEYS = 512
FOX_ROW_CHUNK = 256
HGRN_ROWS = 1024
HGRN_HEADS = 2
CUMSUM_ROWS = 512

_NT = (((1,), (1,)), ((), ()))
_TN = (((0,), (0,)), ((), ()))


def _params(*sem):
    return pltpu.CompilerParams(dimension_semantics=sem, vmem_limit_bytes=VMEM_LIMIT_BYTES)


def _dot(a, b):
    return jnp.dot(a, b, preferred_element_type=F32)


def _dot_nt(a, b):
    return lax.dot_general(a, b, _NT, preferred_element_type=F32)


def _dot_tn(a, b):
    return lax.dot_general(a, b, _TN, preferred_element_type=F32)


def _split3(x):
    hi = x.astype(BF16)
    r1 = x - hi.astype(F32)
    mid = r1.astype(BF16)
    lo = (r1 - mid.astype(F32)).astype(BF16)
    return hi, mid, lo


def _dot_exact_lhs01(a01, x):
    hi, mid, lo = _split3(x)
    return _dot(a01, hi) + _dot(a01, mid) + _dot(a01, lo)


def _dot_exact_lhs01x3(a01x3, x):
    return _dot(a01x3, jnp.concatenate(_split3(x), axis=0))


def _sigmoid_pair(z):
    e = jnp.exp(-jnp.abs(z))
    r = 1.0 / (1.0 + e)
    er = e * r
    pos = z >= 0
    return jnp.where(pos, r, er), jnp.where(pos, er, r)


def _silu(x):
    return x * _sigmoid_pair(x)[0]


def _log_sigmoid(x):
    return jnp.minimum(x, 0.0) - jnp.log(1.0 + jnp.exp(-jnp.abs(x)))


def _rmsnorm_body(x_ref, w_ref, o_ref):
    x = x_ref[...]
    ms = jnp.mean(x * x, axis=-1, keepdims=True)
    o_ref[...] = (x * lax.rsqrt(ms + EPS) * w_ref[...]).astype(o_ref.dtype)


def rmsnorm(x, w, out_dtype, tm):
    m, d = x.shape
    return pl.pallas_call(
        _rmsnorm_body,
        grid=(m // tm,),
        in_specs=[pl.BlockSpec((tm, d), lambda i: (i, 0)), pl.BlockSpec((1, d), lambda i: (0, 0))],
        out_specs=pl.BlockSpec((tm, d), lambda i: (i, 0)),
        out_shape=jax.ShapeDtypeStruct((m, d), out_dtype),
        compiler_params=_params("parallel"),
        name="rmsnorm",
    )(x, w.reshape(1, d))


def _round_proj_body(a_ref, w_ref, y_ref, wb_ref, *, w_transposed):
    wb = w_ref[...].astype(BF16)
    wb_ref[...] = wb
    y_ref[...] = _dot_nt(a_ref[...], wb) if w_transposed else _dot(a_ref[...], wb)


def round_and_proj(a, w, layer, ncols, tn, w_transposed=False):
    m, kk = a.shape
    if w_transposed:
        w_spec = pl.BlockSpec((None, tn, kk), lambda j: (layer, j, 0))
        wb_spec, wb_shape = pl.BlockSpec((tn, kk), lambda j: (j, 0)), (ncols, kk)
    else:
        w_spec = pl.BlockSpec((None, kk, tn), lambda j: (layer, 0, j))
        wb_spec, wb_shape = pl.BlockSpec((kk, tn), lambda j: (0, j)), (kk, ncols)
    return pl.pallas_call(
        functools.partial(_round_proj_body, w_transposed=w_transposed),
        grid=(pl.cdiv(ncols, tn),),
        in_specs=[pl.BlockSpec((m, kk), lambda j: (0, 0)), w_spec],
        out_specs=[pl.BlockSpec((m, tn), lambda j: (0, j)), wb_spec],
        out_shape=[jax.ShapeDtypeStruct((m, ncols), F32), jax.ShapeDtypeStruct(wb_shape, BF16)],
        compiler_params=_params("parallel"),
        name="round_and_proj",
    )(a, w)


def _round_proj_out_body(*refs, n_in):
    a_refs, w_refs = refs[:n_in], refs[n_in:2 * n_in]
    r_ref, y_ref, wb_ref = refs[2 * n_in:]
    kk = a_refs[0].shape[1]
    acc = r_ref[...]
    for idx, (a_ref, w_ref) in enumerate(zip(a_refs, w_refs)):
        wb = w_ref[...].astype(BF16)
        wb_ref[idx * kk:(idx + 1) * kk, :] = wb
        acc = acc + _dot(a_ref[...], wb)
    y_ref[...] = acc


def round_and_proj_out(a_list, w, layer, residual, tn):
    m, kk = a_list[0].shape
    n = w.shape[2]
    assert all(a.shape == (m, kk) for a in a_list) and w.shape[1] == kk * len(a_list) and n % tn == 0
    in_specs = [pl.BlockSpec((m, kk), lambda j: (0, 0)) for _ in a_list]
    in_specs += [pl.BlockSpec((None, kk, tn), functools.partial(lambda j, r: (layer, r, j), r=r))
                 for r in range(len(a_list))]
    in_specs.append(pl.BlockSpec((m, tn), lambda j: (0, j)))
    return pl.pallas_call(
        functools.partial(_round_proj_out_body, n_in=len(a_list)),
        grid=(n // tn,),
        in_specs=in_specs,
        out_specs=[pl.BlockSpec((m, tn), lambda j: (0, j)), pl.BlockSpec((kk * len(a_list), tn), lambda j: (0, j))],
        out_shape=[jax.ShapeDtypeStruct((m, n), F32), jax.ShapeDtypeStruct((kk * len(a_list), n), BF16)],
        compiler_params=_params("parallel"),
        name="round_and_proj_out",
    )(*a_list, *([w] * len(a_list)), residual)


def _proj_body(*refs, n_in, has_residual, outs, w_transposed):
    a_refs = refs[:n_in]
    w_refs = refs[n_in:2 * n_in]
    pos = 2 * n_in
    r_ref = refs[pos] if has_residual else None
    o_refs = refs[pos + int(has_residual):]
    acc = None
    for a_ref, w_ref in zip(a_refs, w_refs):
        d = _dot_nt(a_ref[...], w_ref[...]) if w_transposed else _dot(a_ref[...], w_ref[...])
        acc = d if acc is None else acc + d
    if has_residual:
        acc = r_ref[...] + acc
    for o_ref, (_, scale) in zip(o_refs, outs):
        val = acc if scale == 1.0 else acc * scale
        o_ref[...] = val.astype(o_ref.dtype)


def proj(a_list, w, col0, ncols, tm, tn, outs, residual=None, w_transposed=False):
    m = a_list[0].shape[0]
    kk = a_list[0].shape[1]
    assert all(a.shape == (m, kk) for a in a_list)
    assert m % tm == 0 and ncols % tn == 0 and col0 % tn == 0
    cb = col0 // tn
    in_specs = [pl.BlockSpec((tm, kk), lambda i, j: (i, 0)) for _ in a_list]
    if w_transposed:
        assert len(a_list) == 1 and w.shape[1] == kk
        in_specs.append(pl.BlockSpec((tn, kk), lambda i, j: (cb + j, 0)))
    else:
        assert w.shape[0] == kk * len(a_list)
        in_specs += [pl.BlockSpec((kk, tn), functools.partial(lambda i, j, r: (r, cb + j), r=r))
                     for r in range(len(a_list))]
    args = list(a_list) + [w] * len(a_list)
    if residual is not None:
        in_specs.append(pl.BlockSpec((tm, tn), lambda i, j: (i, j)))
        args.append(residual)
    out_specs = [pl.BlockSpec((tm, tn), lambda i, j: (i, j)) for _ in outs]
    out_shape = [jax.ShapeDtypeStruct((m, ncols), dtype) for dtype, _ in outs]
    return pl.pallas_call(
        functools.partial(_proj_body, n_in=len(a_list), has_residual=residual is not None, outs=tuple(outs),
                          w_transposed=w_transposed),
        grid=(m // tm, ncols // tn),
        in_specs=in_specs,
        out_specs=out_specs,
        out_shape=out_shape,
        compiler_params=_params("parallel", "parallel"),
        name="proj",
    )(*args)


def _hgrn_maps(c):
    levels = int(np.log2(c))
    assert 2 ** levels == c
    t = np.arange(c)[:, None]
    s = np.arange(c)[None, :]
    mats = [(s <= t), (s > t)]
    for l in range(levels):
        b = 2 ** l
        start = (t // (2 * b)) * (2 * b)
        upper = (t // b) % 2 == 1
        mats.append((upper & (s >= start + b) & (s <= t)) | ((~upper) & (s > t) & (s <= start + b - 1)))
    return np.concatenate(mats, axis=0).astype(np.float32), levels


def _hgrn_body(q_ref, z_ref, v_ref, g_ref, lb_ref, nw_ref, a_ref, s0_ref, o_ref, sout_ref, st_scr,
               *, c, n_chunks, levels, heads):
    tb = pl.program_id(2)

    @pl.when(tb == 0)
    def _():
        for hh in range(heads):
            st_scr[hh] = s0_ref[0, hh].T

    nw = nw_ref[...]
    amat = a_ref[...]
    row = lax.broadcasted_iota(jnp.int32, (c, c), 0)
    col = lax.broadcasted_iota(jnp.int32, (c, c), 1)
    xor = row ^ col

    units = [(slice(ci * c, (ci + 1) * c), slice(hh * HEAD_DIM, (hh + 1) * HEAD_DIM))
             for hh in range(heads) for ci in range(n_chunks)]
    nu = len(units)
    lbs = [lb_ref[:, cols] for _, cols in units]
    qs = [_silu(q_ref[rows, cols]) for rows, cols in units]
    sigs = [_sigmoid_pair(z_ref[rows, cols]) for rows, cols in units]
    gs = [jnp.log(lbs[u] + (1.0 - lbs[u]) * sigs[u][0]) for u in range(nu)]
    ks = [(1.0 - lbs[u]) * sigs[u][1] for u in range(nu)]
    vs = [v_ref[rows, cols] for rows, cols in units]
    vbs = [v.astype(BF16) for v in vs]
    es = [jnp.exp(_dot_exact_lhs01x3(amat, g)) for g in gs]
    atts = [None] * nu
    for l in range(levels - 1, -1, -1):
        for u in range(nu):
            el = es[u][(2 + l) * c:(3 + l) * c]
            al = _dot_nt((qs[u] * el).astype(BF16), (ks[u] * el).astype(BF16))
            atts[u] = al if atts[u] is None else jnp.where(xor < 2 ** (l + 1), al, atts[u])
    atts = [jnp.where(row > col, att, 0.0).astype(BF16) for att in atts]
    o_intras = [_dot(atts[u], vbs[u]) + jnp.sum(qs[u] * ks[u], axis=-1, keepdims=True) * vs[u] for u in range(nu)]
    upds = [_dot_tn(vbs[u], (ks[u] * es[u][c:2 * c]).astype(BF16)) for u in range(nu)]
    qgs = [(qs[u] * es[u][0:c]).astype(BF16) for u in range(nu)]

    for hh in range(heads):
        st = st_scr[hh]
        for ci in range(n_chunks):
            u = hh * n_chunks + ci
            rows, cols = units[u]
            o = _dot_nt(qgs[u], st.astype(BF16)) + o_intras[u]
            st = st * es[u][c - 1:c] + upds[u]
            ms = jnp.mean(o * o, axis=-1, keepdims=True)
            y = o * lax.rsqrt(ms + EPS) * nw
            o_ref[rows, cols] = (y * _silu(g_ref[rows, cols])).astype(o_ref.dtype)
        st_scr[hh] = st

    @pl.when(tb == pl.num_programs(2) - 1)
    def _():
        for hh in range(heads):
            sout_ref[0, hh] = st_scr[hh].T


def hgrn2(qa, fa, ia, ga, lb, a_norm_w, s0, batch, seq, rows_per_step, c, heads_per_step, out_dtype,
          col0=(0, 0, 0, 0)):
    n = qa.shape[0]
    width = lb.shape[0]
    h = width // HEAD_DIM
    assert n == batch * seq and seq % rows_per_step == 0 and rows_per_step % c == 0 and h % heads_per_step == 0
    nb = seq // rows_per_step
    gw = heads_per_step * HEAD_DIM
    amat_np, levels = _hgrn_maps(c)
    amat = jnp.asarray(np.concatenate([amat_np] * 3, axis=1), BF16)
    row_spec = pl.BlockSpec((rows_per_step, gw), lambda b, hh, t: (b * nb + t, hh))
    in_row_specs = [pl.BlockSpec((rows_per_step, gw), functools.partial(lambda b, hh, t, off: (b * nb + t, off + hh), off=c0 // gw))
                    for c0 in col0]
    assert all(c0 % gw == 0 for c0 in col0)
    state_spec = pl.BlockSpec((1, heads_per_step, HEAD_DIM, HEAD_DIM), lambda b, hh, t: (b, hh, 0, 0))
    body = functools.partial(_hgrn_body, c=c, n_chunks=rows_per_step // c, levels=levels, heads=heads_per_step)
    return pl.pallas_call(
        body,
        grid=(batch, h // heads_per_step, nb),
        in_specs=in_row_specs + [
                  pl.BlockSpec((1, gw), lambda b, hh, t: (0, hh)),
                  pl.BlockSpec((1, HEAD_DIM), lambda b, hh, t: (0, 0)),
                  pl.BlockSpec(amat.shape, lambda b, hh, t: (0, 0)),
                  state_spec],
        out_specs=[row_spec, state_spec],
        out_shape=[jax.ShapeDtypeStruct((n, width), out_dtype),
                   jax.ShapeDtypeStruct((batch, h, HEAD_DIM, HEAD_DIM), F32)],
        scratch_shapes=[pltpu.VMEM((heads_per_step, HEAD_DIM, HEAD_DIM), F32)],
        compiler_params=_params("parallel", "parallel", "arbitrary"),
        name="hgrn2",
    )(qa, fa, ia, ga, lb.reshape(1, width), a_norm_w.reshape(1, HEAD_DIM), amat, s0)


def _sb_prompt_body(q_ref, k_ref, v_ref, g_ref, uu_ref, o_ref, run_scr, acc_scr, w_scr, *, tq, tk, rc):
    i = pl.program_id(1)
    nd = tq // tk
    nl = tk // LANES
    q0 = pl.multiple_of(i * tq, tq)
    run_scr[...] = jnp.zeros(run_scr.shape, F32)
    acc_scr[...] = jnp.zeros(acc_scr.shape, F32)

    def apply_pending(slot, k_prev, first_row):
        vt = v_ref[pl.ds(pl.multiple_of(k_prev, tk), tk), :]
        for r0 in range(first_row, tq, rc):
            rows = slice(r0, min(r0 + rc, tq))
            acc_scr[rows, :] = acc_scr[rows, :] + _dot(w_scr[slot, rows, :], vt)

    def score(slot, k0, first_row, masked):
        kt = k_ref[pl.ds(pl.multiple_of(k0, tk), tk), :]
        for r0 in range(first_row, tq, rc):
            r1 = min(r0 + rc, tq)
            rows = slice(r0, r1)
            z = _dot_nt(q_ref[rows, :], kt)
            if masked:
                qpos = lax.broadcasted_iota(jnp.int32, (r1 - r0, LANES), 0) + (q0 + r0)
                kpos = lax.broadcasted_iota(jnp.int32, (r1 - r0, LANES), 1) + k0
            zs, sps, his, los, valids = [], [], [], [], []
            rowsum = None
            for c in range(nl):
                zc = z[:, c * LANES:(c + 1) * LANES]
                sp = jnp.maximum(zc, 0.0) + jnp.log2(1.0 + jnp.exp2(-jnp.abs(zc)))
                if masked:
                    valid = (kpos + c * LANES) < qpos
                    sp = jnp.where(valid, sp, 0.0)
                    valids.append(valid)
                hi = sp.astype(BF16)
                lo = (sp - hi.astype(F32)).astype(BF16)
                zs.append(zc); sps.append(sp); his.append(hi); los.append(lo)
                rowsum = sp if rowsum is None else rowsum + sp
            tail = _dot(jnp.concatenate(his + los, axis=1), uu_ref[...])
            run = run_scr[rows, :]
            for c in range(nl):
                w = jnp.exp2(zs[c] - (sps[c] + tail[:, c * LANES:(c + 1) * LANES] + run))
                if masked:
                    w = jnp.where(valids[c], w, 0.0)
                w_scr[slot, rows, c * LANES:(c + 1) * LANES] = w.astype(BF16)
            run_scr[rows, :] = run + jnp.sum(rowsum, axis=-1, keepdims=True)

    assert nd % 2 == 0
    for d in range(nd - 1, 0, -2):
        score(0, q0 + d * tk, d * tk, True)
        if d + 1 < nd:
            apply_pending(1, q0 + (d + 1) * tk, (d + 1) * tk)
        score(1, q0 + (d - 1) * tk, (d - 1) * tk, True)
        apply_pending(0, q0 + d * tk, d * tk)

    def trip(carry):
        jj, kp, _ = carry
        k_a = q0 - (2 * jj + 1) * tk
        score(0, k_a, 0, False)
        apply_pending(1, kp, 0)
        score(1, k_a - tk, 0, False)
        more = jnp.min(run_scr[...]) < F32_UNDERFLOW_LOG2
        apply_pending(0, k_a, 0)
        return jj + 1, k_a - tk, more

    _, k_pending, _ = lax.while_loop(lambda carry: (carry[0] < (i * nd) // 2) & carry[2], trip,
                                     (jnp.int32(0), q0, jnp.bool_(True)))
    apply_pending(1, k_pending, 0)
    o_ref[...] = (acc_scr[...] * _silu(g_ref[...])).astype(o_ref.dtype)


def _tail_matrix(n):
    sp = np.arange(n)[:, None]
    s = np.arange(n)[None, :]
    return (sp > s).astype(np.float32)


def sb_prompt(q, k, v, gate, tq, tk, rc, out_dtype):
    t, width = q.shape
    h = width // HEAD_DIM
    u = _tail_matrix(tk)
    uu = jnp.asarray(np.concatenate([u, u], axis=0), BF16)
    q_spec = pl.BlockSpec((tq, HEAD_DIM), lambda hh, i: (i, hh))
    kv_spec = pl.BlockSpec((t, HEAD_DIM), lambda hh, i: (0, hh))
    return pl.pallas_call(
        functools.partial(_sb_prompt_body, tq=tq, tk=tk, rc=rc),
        grid=(h, t // tq),
        in_specs=[q_spec, kv_spec, kv_spec, q_spec, pl.BlockSpec(uu.shape, lambda hh, i: (0, 0))],
        out_specs=q_spec,
        out_shape=jax.ShapeDtypeStruct((t, width), out_dtype),
        scratch_shapes=[pltpu.VMEM((tq, LANES), F32), pltpu.VMEM((tq, HEAD_DIM), F32),
                        pltpu.VMEM((2, tq, tk), BF16)],
        compiler_params=_params("parallel", "arbitrary"),
        name="sb_prompt",
    )(q, k, v, gate, uu)


def _logf_cumsum_body(fl_ref, b_ref, tri_ref, lf_ref, f2_ref, carry_scr):
    @pl.when(pl.program_id(0) == 0)
    def _():
        carry_scr[...] = jnp.zeros_like(carry_scr)

    lf = _log_sigmoid(fl_ref[...] + b_ref[...])
    lf_ref[...] = lf
    f = carry_scr[...] + _dot_exact_lhs01(tri_ref[...], lf)
    f2_ref[...] = f * LOG2E
    carry_scr[...] = f[f.shape[0] - 1:, :]


def logf_cumsum(fl, b_forget, blk):
    t, h = fl.shape
    tri = jnp.asarray(np.tril(np.ones((blk, blk), np.float32)), BF16)
    spec = pl.BlockSpec((blk, h), lambda i: (i, 0))
    return pl.pallas_call(
        _logf_cumsum_body,
        grid=(t // blk,),
        in_specs=[spec, pl.BlockSpec((1, h), lambda i: (0, 0)), pl.BlockSpec((blk, blk), lambda i: (0, 0))],
        out_specs=[spec, spec],
        out_shape=[jax.ShapeDtypeStruct((t, h), F32), jax.ShapeDtypeStruct((t, h), F32)],
        scratch_shapes=[pltpu.VMEM((1, h), F32)],
        compiler_params=_params("arbitrary"),
        name="logf_cumsum",
    )(fl, b_forget.reshape(1, h), tri)


def _logf_body(fl_ref, b_ref, lf_ref):
    lf_ref[...] = _log_sigmoid(fl_ref[...] + b_ref[...])


def logf_only(fl, b_forget):
    t, h = fl.shape
    return pl.pallas_call(
        _logf_body,
        grid=(1,),
        in_specs=[pl.BlockSpec((t, h), lambda i: (0, 0)), pl.BlockSpec((1, h), lambda i: (0, 0))],
        out_specs=pl.BlockSpec((t, h), lambda i: (0, 0)),
        out_shape=jax.ShapeDtypeStruct((t, h), F32),
        name="logf",
    )(fl, b_forget.reshape(1, h))


_BIAS_PIECES = 3


def _bias_selectors(h):
    sel = np.zeros((2, h, _BIAS_PIECES, h, LANES), np.float32)
    for hh in range(h):
        for p in range(_BIAS_PIECES):
            sel[0, hh, p, hh, p] = 1.0
            sel[1, hh, p, hh, _BIAS_PIECES + p] = -1.0
    return sel


def _bias_columns(f, sel_ref, query_side):
    out = None
    for p, piece in enumerate(_split3(f)):
        d = _dot(piece, sel_ref[0, p])
        out = d if out is None else out + d
    lane = lax.broadcasted_iota(jnp.int32, out.shape, 1)
    ones_at = (lane >= _BIAS_PIECES) & (lane < 2 * _BIAS_PIECES) if query_side else lane < _BIAS_PIECES
    return jnp.where(ones_at, 1.0, out).astype(BF16)


def _fox_prompt_body(q_ref, k_ref, v_ref, g_ref, f_ref, selq_ref, selk_ref, o_ref,
                     kx_scr, vx_scr, m_scr, acc_scr, p_scr, alpha_scr, kmax_scr, ub_scr, *, tq, tk, rc):
    i = pl.program_id(1)
    nd = tq // tk
    nl = tk // LANES
    t_all = k_ref.shape[0]

    head = pl.program_id(0)

    def own_column(f_rows):
        lane = lax.broadcasted_iota(jnp.int32, f_rows.shape, 1)
        return jnp.sum(jnp.where(lane == head, f_rows, 0.0), axis=-1, keepdims=True)

    @pl.when(i == 0)
    def _():
        kb = k_ref[...].astype(BF16)
        kx_scr[:, 0:HEAD_DIM] = kb
        kx_scr[:, HEAD_DIM:] = _bias_columns(f_ref[...], selk_ref, False)
        vx_scr[:, 0:HEAD_DIM] = v_ref[...].astype(BF16)
        vx_scr[:, HEAD_DIM:] = jnp.ones((t_all, LANES), BF16)
        kf = kb.astype(F32)
        knorm2 = jnp.max(jnp.sum(kf * kf, axis=-1, keepdims=True), axis=0, keepdims=True)
        kmax_scr[...] = jnp.broadcast_to(jnp.sqrt(knorm2), kmax_scr.shape)

    q0 = pl.multiple_of(i * tq, tq)
    qx = jnp.concatenate([q_ref[...], _bias_columns(f_ref[pl.ds(q0, tq), :], selq_ref, True)], axis=1)
    m_scr[...] = jnp.full(m_scr.shape, -jnp.inf, F32)
    acc_scr[...] = jnp.zeros(acc_scr.shape, F32)
    qf = q_ref[...].astype(F32)
    qnorm = jnp.sqrt(jnp.sum(qf * qf, axis=-1, keepdims=True))
    ub_scr[...] = qnorm * kmax_scr[...] * (1.0 + 2.0 ** -10) + own_column(f_ref[pl.ds(q0, tq), :])

    def apply_pending(slot, k_prev, first_row):
        vxt = vx_scr[pl.ds(pl.multiple_of(k_prev, tk), tk), :]
        for r0 in range(first_row, tq, rc):
            rows = slice(r0, r0 + rc)
            pv = _dot(p_scr[slot, rows, :], vxt)
            alpha = alpha_scr[slot, rows, :]
            acc_scr[rows, 0:HEAD_DIM] = alpha * acc_scr[rows, 0:HEAD_DIM] + pv[:, 0:HEAD_DIM]
            acc_scr[rows, HEAD_DIM:] = alpha * acc_scr[rows, HEAD_DIM:] + pv[:, HEAD_DIM:]

    def score(slot, k0, first_row, masked):
        kxt = kx_scr[pl.ds(pl.multiple_of(k0, tk), tk), :]
        for r0 in range(first_row, tq, rc):
            rows = slice(r0, r0 + rc)
            s = _dot_nt(qx[r0:r0 + rc, :], kxt)
            sb = [s[:, c * LANES:(c + 1) * LANES] for c in range(nl)]
            if masked:
                qpos = lax.broadcasted_iota(jnp.int32, (rc, LANES), 0) + (q0 + r0)
                kpos = lax.broadcasted_iota(jnp.int32, (rc, LANES), 1) + k0
                sb = [jnp.where((kpos + c * LANES) <= qpos, sb[c], -jnp.inf) for c in range(nl)]
            mx = sb[0]
            for c in range(1, nl):
                mx = jnp.maximum(mx, sb[c])
            m_old = m_scr[rows, :]
            m_new = jnp.maximum(m_old, jnp.max(mx, axis=-1, keepdims=True))
            alpha_scr[slot, rows, :] = jnp.exp2(m_old - m_new)
            for c in range(nl):
                p_scr[slot, rows, c * LANES:(c + 1) * LANES] = jnp.exp2(sb[c] - m_new).astype(BF16)
            m_scr[rows, :] = m_new

    assert nd % 2 == 0 and tk % rc == 0
    for d in range(0, nd, 2):
        score(0, q0 + d * tk, d * tk, True)
        if d > 0:
            apply_pending(1, q0 + (d - 1) * tk, (d - 1) * tk)
        score(1, q0 + (d + 1) * tk, (d + 1) * tk, True)
        apply_pending(0, q0 + d * tk, d * tk)

    p_scr[1, 0:(nd - 1) * tk, :] = jnp.zeros(((nd - 1) * tk, tk), BF16)
    alpha_scr[1, 0:(nd - 1) * tk, :] = jnp.ones(((nd - 1) * tk, LANES), F32)

    def pair(k_a, kp):
        score(0, k_a, 0, False)
        apply_pending(1, kp, 0)
        score(1, k_a - tk, 0, False)
        more = visible(k_a - tk - 1)
        apply_pending(0, k_a, 0)
        return k_a - tk, more

    def visible(k_hi):
        f_hi = own_column(f_ref[pl.ds(jnp.maximum(k_hi, 0), 1), :])
        return jnp.max(ub_scr[...] - f_hi - m_scr[...]) > -(F32_UNDERFLOW_LOG2 + 2.0)

    def trip(carry):
        jj, kp, _ = carry
        kp, more = pair(q0 - (2 * jj + 1) * tk, kp)
        return jj + 1, kp, more

    n_pairs = (i * nd) // 2
    _, k_pending, _ = lax.while_loop(lambda carry: (carry[0] < n_pairs) & carry[2], trip,
                                     (jnp.int32(0), q0 + (nd - 1) * tk, visible(q0 - 1)))
    apply_pending(1, k_pending, 0)
    o_ref[...] = (acc_scr[:, 0:HEAD_DIM] / acc_scr[:, HEAD_DIM:] * _silu(g_ref[...])).astype(o_ref.dtype)


def fox_prompt(q, k, v, gate, f2, tq, tk, rc, out_dtype):
    t, width = q.shape
    h = width // HEAD_DIM
    sel = jnp.asarray(_bias_selectors(h), BF16)
    q_spec = pl.BlockSpec((tq, HEAD_DIM), lambda hh, i: (i, hh))
    kv_spec = pl.BlockSpec((t, HEAD_DIM), lambda hh, i: (0, hh))
    sel_spec = pl.BlockSpec((1, _BIAS_PIECES, h, LANES), lambda hh, i: (hh, 0, 0, 0))
    return pl.pallas_call(
        functools.partial(_fox_prompt_body, tq=tq, tk=tk, rc=rc),
        grid=(h, t // tq),
        in_specs=[q_spec, kv_spec, kv_spec, q_spec, pl.BlockSpec((t, h), lambda hh, i: (0, 0)),
                  sel_spec, sel_spec],
        out_specs=q_spec,
        out_shape=jax.ShapeDtypeStruct((t, width), out_dtype),
        scratch_shapes=[pltpu.VMEM((t, 2 * HEAD_DIM), BF16), pltpu.VMEM((t, 2 * HEAD_DIM), BF16),
                        pltpu.VMEM((tq, LANES), F32), pltpu.VMEM((tq, 2 * HEAD_DIM), F32),
                        pltpu.VMEM((2, tq, tk), BF16), pltpu.VMEM((2, tq, LANES), F32),
                        pltpu.VMEM((1, LANES), F32), pltpu.VMEM((tq, LANES), F32)],
        compiler_params=_params("parallel", "arbitrary"),
        name="fox_prompt",
    )(q, k, v, gate, f2, sel[0], sel[1])


HEADS_PER_GROUP = 8


def _expand_queries(q, tq):
    gw = q.shape[1]
    rep = jnp.concatenate([q] * HEADS_PER_GROUP, axis=0)
    r = lax.broadcasted_iota(jnp.int32, (HEADS_PER_GROUP * tq, gw), 0) // tq
    cidx = lax.broadcasted_iota(jnp.int32, (HEADS_PER_GROUP * tq, gw), 1) // HEAD_DIM
    return jnp.where(r == cidx, rep, 0.0)


def _collect_heads(full, tq):
    return jnp.concatenate(
        [full[hh * tq:(hh + 1) * tq, hh * HEAD_DIM:(hh + 1) * HEAD_DIM] for hh in range(HEADS_PER_GROUP)],
        axis=1)


def _gather_keys(cache_ref, new_ref, scr, past, tk):
    g = HEADS_PER_GROUP
    x = cache_ref[0].reshape(past // g, g, g, HEAD_DIM)
    x = jnp.swapaxes(x, 1, 2)
    for hh in range(g):
        scr[0:past, hh * HEAD_DIM:(hh + 1) * HEAD_DIM] = x[:, hh].reshape(past, HEAD_DIM).astype(BF16)
    scr[past:tk, :] = new_ref[...].astype(BF16)


def _row_blocks(n):
    return [(r0, min(r0 + LANES, n)) for r0 in range(0, n, LANES)]


def _cumsum_rows(pieces, tri_ref):
    n = pieces[0].shape[0]
    out, carry = [], None
    for r0, r1 in _row_blocks(n):
        tri = tri_ref[0:r1 - r0, 0:r1 - r0]
        local = None
        for piece in pieces:
            d = _dot(tri, piece[r0:r1])
            local = d if local is None else local + d
        if carry is not None:
            local = local + carry
        carry = local[r1 - r0 - 1:r1 - r0]
        out.append(local)
    return jnp.concatenate(out, axis=0)


def _tailsum_rows(pieces, tri_ref):
    n = pieces[0].shape[0]
    out, carry = [], None
    for r0, r1 in reversed(_row_blocks(n)):
        tri = tri_ref[0:r1 - r0, 0:r1 - r0]
        local, total = None, None
        for piece in pieces:
            blk = piece[r0:r1]
            d = _dot_tn(tri, blk)
            local = d if local is None else local + d
            t = blk.astype(F32)
            total = t if total is None else total + t
        local = local - total
        if carry is not None:
            local = local + carry
        carry = local[0:1] + total[0:1]
        out.append(local)
    return jnp.concatenate(out[::-1], axis=0)


def _sb_decode_body(q_ref, kn_ref, vn_ref, kc_ref, vc_ref, g_ref, tri_ref, o_ref, k_scr, v_scr, *, past, tq):
    tk = past + tq
    _gather_keys(kc_ref, kn_ref, k_scr, past, tk)
    _gather_keys(vc_ref, vn_ref, v_scr, past, tk)

    lanes = HEADS_PER_GROUP * tq
    qx = _expand_queries(q_ref[...] * (HEAD_DIM ** -0.5), tq).astype(BF16)
    z = _dot_nt(k_scr[...], qx)
    kpos = lax.broadcasted_iota(jnp.int32, (tk, lanes), 0)
    qpos = past + lax.broadcasted_iota(jnp.int32, (tk, lanes), 1) % tq
    valid = kpos < qpos
    ls_neg = -(jnp.maximum(z, 0.0) + jnp.log(1.0 + jnp.exp(-jnp.abs(z))))
    lm = jnp.where(valid, ls_neg, 0.0)
    hi = lm.astype(BF16)
    lo = (lm - hi.astype(F32)).astype(BF16)
    tail = _tailsum_rows((hi, lo), tri_ref)
    w = jnp.where(valid, jnp.exp(z + ls_neg + tail), 0.0)
    full = _dot_tn(w.astype(BF16), v_scr[...])
    o_ref[...] = (_collect_heads(full, tq) * _silu(g_ref[...])).astype(o_ref.dtype)


def _cache_spec(layer, past):
    return pl.BlockSpec((None, 1, past, HEADS_PER_GROUP, HEAD_DIM), lambda b, g: (layer, b, 0, g, 0))


def _decode_row_specs(tq, gw, col0):
    assert all(c0 % gw == 0 for c0 in col0)
    return [pl.BlockSpec((tq, gw), functools.partial(lambda b, g, off: (b, off + g), off=c0 // gw)) for c0 in col0]


def sb_decode(q, k_new, v_new, k_cache, v_cache, layer, gate, batch, tq, out_dtype, col0=(0, 0, 0, 0)):
    n = q.shape[0]
    width = k_cache.shape[3] * HEAD_DIM
    past = k_cache.shape[2]
    tk = past + tq
    gw = HEADS_PER_GROUP * HEAD_DIM
    groups = width // gw
    tri = jnp.asarray(np.tril(np.ones((LANES, LANES), np.float32)), BF16)
    row_spec = pl.BlockSpec((tq, gw), lambda b, g: (b, g))
    rs = _decode_row_specs(tq, gw, col0)
    cache_spec = _cache_spec(layer, past)
    return pl.pallas_call(
        functools.partial(_sb_decode_body, past=past, tq=tq),
        grid=(batch, groups),
        in_specs=[rs[0], rs[1], rs[2], cache_spec, cache_spec, rs[3],
                  pl.BlockSpec((LANES, LANES), lambda b, g: (0, 0))],
        out_specs=row_spec,
        out_shape=jax.ShapeDtypeStruct((n, width), out_dtype),
        scratch_shapes=[pltpu.VMEM((tk, gw), BF16), pltpu.VMEM((tk, gw), BF16)],
        compiler_params=_params("parallel", "parallel"),
        name="sb_decode",
    )(q, k_new, v_new, k_cache, v_cache, gate, tri)


def _fox_decode_body(q_ref, kn_ref, vn_ref, kc_ref, vc_ref, g_ref, lfn_ref, lfc_ref, ex_ref, tri_ref, o_ref,
                     k_scr, v_scr, lf_scr, *, past, tq):
    tk = past + tq
    _gather_keys(kc_ref, kn_ref, k_scr, past, tk)
    _gather_keys(vc_ref, vn_ref, v_scr, past, tk)
    lf_scr[0:past, :] = lfc_ref[0]
    lf_scr[past:tk, :] = lfn_ref[...]

    lanes = HEADS_PER_GROUP * tq
    hi, mid, lo = _split3(lf_scr[...])
    ex = ex_ref[0]
    lfx = _dot(hi, ex) + _dot(mid, ex) + _dot(lo, ex)
    f_key = _cumsum_rows(_split3(lfx), tri_ref)
    kpos = lax.broadcasted_iota(jnp.int32, (tk, lanes), 0)
    qpos = past + lax.broadcasted_iota(jnp.int32, (tk, lanes), 1) % tq
    f_query = jnp.sum(jnp.where(kpos == qpos, f_key, 0.0), axis=0, keepdims=True)

    qx = _expand_queries(q_ref[...] * (HEAD_DIM ** -0.5), tq).astype(BF16)
    s = _dot_nt(k_scr[...], qx) + (f_query - f_key)
    s = jnp.where(kpos <= qpos, s, -jnp.inf)
    p = jnp.exp(s - jnp.max(s, axis=0, keepdims=True))
    p = p / jnp.sum(p, axis=0, keepdims=True)
    full = _dot_tn(p.astype(BF16), v_scr[...])
    o_ref[...] = (_collect_heads(full, tq) * _silu(g_ref[...])).astype(o_ref.dtype)


def fox_decode(q, k_new, v_new, k_cache, v_cache, layer, gate, lf_new, lf_cache, batch, tq, out_dtype,
               col0=(0, 0, 0, 0)):
    n = q.shape[0]
    h = k_cache.shape[3]
    width = h * HEAD_DIM
    past = k_cache.shape[2]
    tk = past + tq
    gw = HEADS_PER_GROUP * HEAD_DIM
    groups = width // gw
    lanes = HEADS_PER_GROUP * tq
    tri = jnp.asarray(np.tril(np.ones((LANES, LANES), np.float32)), BF16)
    head_of_lane = np.arange(lanes)[None, None, :] // tq + HEADS_PER_GROUP * np.arange(groups)[:, None, None]
    expand = jnp.asarray((np.arange(h)[None, :, None] == head_of_lane).astype(np.float32), BF16)
    row_spec = pl.BlockSpec((tq, gw), lambda b, g: (b, g))
    rs = _decode_row_specs(tq, gw, col0)
    cache_spec = _cache_spec(layer, past)
    return pl.pallas_call(
        functools.partial(_fox_decode_body, past=past, tq=tq),
        grid=(batch, groups),
        in_specs=[rs[0], rs[1], rs[2], cache_spec, cache_spec, rs[3],
                  pl.BlockSpec((tq, h), lambda b, g: (b, 0)),
                  pl.BlockSpec((None, 1, past, h), lambda b, g: (layer, b, 0, 0)),
                  pl.BlockSpec((1, h, lanes), lambda b, g: (g, 0, 0)),
                  pl.BlockSpec((LANES, LANES), lambda b, g: (0, 0))],
        out_specs=row_spec,
        out_shape=jax.ShapeDtypeStruct((n, width), out_dtype),
        scratch_shapes=[pltpu.VMEM((tk, gw), BF16), pltpu.VMEM((tk, gw), BF16), pltpu.VMEM((tk, h), F32)],
        compiler_params=_params("parallel", "parallel"),
        name="fox_decode",
    )(q, k_new, v_new, k_cache, v_cache, gate, lf_new, lf_cache, expand, tri)


QK_SCALE_LOG2 = LOG2E * HEAD_DIM ** -0.5
FLAT32 = (F32, 1.0)
FLAT16 = (BF16, 1.0)
QUERY16 = (BF16, QK_SCALE_LOG2)


def _even_prompt(x, norm_w, w_in, w_out, lb, a_norm_w, s0, seq):
    n, d = x.shape
    half = d // 2
    hn = rmsnorm(x, norm_w, BF16, NORM_ROWS)

    def p(group, *outs):
        return proj([hn], w_in, group * half, half, PROJ_TILE, PROJ_TILE, outs)

    (rec,) = proj([hn], w_in, 0, 4 * half, PROJ_TILE, PROJ_TILE, (FLAT32,))
    (qb,), (gb,) = p(4, QUERY16), p(7, FLAT32)
    kb_leaf, kb = p(5, FLAT32, FLAT16)
    vb_leaf, vb = p(6, FLAT32, FLAT16)
    oa, s_new = hgrn2(rec, rec, rec, rec, lb, a_norm_w, s0, 1, seq, HGRN_ROWS, HGRN_CHUNK, HGRN_HEADS, BF16,
                      col0=(0, half, 2 * half, 3 * half))
    ob = sb_prompt(qb, kb, vb, gb, ATTN_ROWS, SB_KEYS, ATTN_ROWS, BF16)
    (y,) = proj([oa, ob], w_out, 0, d, PROJ_TILE, PROJ_TILE, (FLAT32,), residual=x)
    return y, s_new, kb_leaf, vb_leaf


def _even_decode(x, norm_w, w_in_f32, layer, w_out_f32, lb, a_norm_w, s0, k_cache, v_cache, batch, seq):
    n, d = x.shape
    half = d // 2
    hn = rmsnorm(x, norm_w, BF16, n)
    cols, w_in = round_and_proj(hn, w_in_f32, layer, 8 * half, DECODE_PROJ_COLS)
    oa, s_new = hgrn2(cols, cols, cols, cols, lb, a_norm_w, s0, batch, seq, seq, seq, half // HEAD_DIM, BF16,
                      col0=(0, half, 2 * half, 3 * half))
    ob = sb_decode(cols, cols, cols, k_cache, v_cache, layer, cols, batch, seq, BF16,
                   col0=(4 * half, 5 * half, 6 * half, 7 * half))
    y, w_out = round_and_proj_out([oa, ob], w_out_f32, layer, x, DECODE_PROJ_COLS)
    return y, s_new, cols[:, 5 * half:6 * half], cols[:, 6 * half:7 * half], w_in, w_out


def _odd_prompt(x, norm_w, w_in, w_fl, b_forget, w_out, seq):
    n, d = x.shape
    heads = d // HEAD_DIM
    hn = rmsnorm(x, norm_w, BF16, NORM_ROWS)

    def p(group, *outs):
        return proj([hn], w_in, group * d, d, PROJ_TILE, PROJ_TILE, outs, w_transposed=True)

    (q,), (gate,) = p(0, QUERY16), p(3, FLAT32)
    (fl,) = proj([hn], w_fl, 0, heads, PROJ_TILE, heads, (FLAT32,), w_transposed=True)
    logf, f2 = logf_cumsum(fl, b_forget, CUMSUM_ROWS)
    (k,), (v,) = p(1, FLAT32), p(2, FLAT32)
    o = fox_prompt(q, k, v, gate, f2, ATTN_ROWS, FOX_KEYS, FOX_ROW_CHUNK, BF16)
    (y,) = proj([o], w_out, 0, d, PROJ_TILE, PROJ_TILE, (FLAT32,), residual=x)
    return y, k, v, logf


def _odd_decode(x, norm_w, w_in_t_f32, layer, b_forget, w_out_f32, k_cache, v_cache, lf_cache, batch, seq):
    n, d = x.shape
    heads = d // HEAD_DIM
    hn = rmsnorm(x, norm_w, BF16, n)
    cols, w_in = round_and_proj(hn, w_in_t_f32, layer, 4 * d + heads, DECODE_PROJ_COLS, w_transposed=True)
    logf = logf_only(cols[:, 4 * d:], b_forget)
    o = fox_decode(cols, cols, cols, k_cache, v_cache, layer, cols, logf, lf_cache, batch, seq, BF16,
                   col0=(0, d, 2 * d, 3 * d))
    y, w_out = round_and_proj_out([o], w_out_f32, layer, x, DECODE_PROJ_COLS)
    return y, cols[:, d:2 * d], cols[:, 2 * d:3 * d], logf, w_in, w_out


def kernel(x_prompt, x_sample, state_a_hgrn, cache_b_k, cache_b_v, cache_c_k, cache_c_v, cache_c_logf,
           norm_w, final_norm_w, w_in_even, w_out_even, lb_logits, a_norm_w, w_in_odd, b_forget, w_out_odd):
    bp, tp, d = x_prompt.shape
    bs, ts, _ = x_sample.shape
    assert bp == 1
    depth = norm_w.shape[0]
    n_even = w_in_even.shape[0]
    lb_all = jnp.cumsum(jax.nn.softmax(lb_logits.astype(F32), axis=0), axis=0)[:n_even]

    hp = x_prompt.reshape(bp * tp, d)
    hs = x_sample.reshape(bs * ts, d)
    outs = {name: [] for name in ("sa_p", "sa_s", "bk_p", "bv_p", "bk_s", "bv_s",
                                  "ck_p", "cv_p", "cf_p", "ck_s", "cv_s", "cf_s")}
    for layer in range(depth):
        j = layer // 2
        if layer % 2 == 0:
            a_heads = state_a_hgrn.shape[2]
            b_heads = cache_b_k.shape[3]
            zeros = jnp.zeros((bp, a_heads) + state_a_hgrn.shape[3:], F32)
            hs, ss, ksm, vsm, w_in, w_out = _even_decode(hs, norm_w[layer], w_in_even, j, w_out_even, lb_all[j],
                                                         a_norm_w[j], state_a_hgrn[j], cache_b_k, cache_b_v, bs, ts)
            hp, sp, kp, vp = _even_prompt(hp, norm_w[layer], w_in, w_out, lb_all[j], a_norm_w[j], zeros, tp)
            outs["sa_p"].append(sp); outs["sa_s"].append(ss)
            outs["bk_p"].append(kp.reshape(bp, tp, b_heads, HEAD_DIM))
            outs["bv_p"].append(vp.reshape(bp, tp, b_heads, HEAD_DIM))
            outs["bk_s"].append(ksm.reshape(bs, ts, b_heads, HEAD_DIM))
            outs["bv_s"].append(vsm.reshape(bs, ts, b_heads, HEAD_DIM))
        else:
            c_heads = cache_c_k.shape[3]
            w_in_t = jnp.swapaxes(w_in_odd, 1, 2)
            hs, ksm, vsm, fsm, w_in, w_out = _odd_decode(hs, norm_w[layer], w_in_t, j, b_forget[j], w_out_odd,
                                                         cache_c_k, cache_c_v, cache_c_logf, bs, ts)
            w_fl = w_in[4 * c_heads * HEAD_DIM:, :]
            hp, kp, vp, fp = _odd_prompt(hp, norm_w[layer], w_in, w_fl, b_forget[j], w_out, tp)
            outs["ck_p"].append(kp.reshape(bp, tp, c_heads, HEAD_DIM))
            outs["cv_p"].append(vp.reshape(bp, tp, c_heads, HEAD_DIM))
            outs["cf_p"].append(fp.reshape(bp, tp, c_heads))
            outs["ck_s"].append(ksm.reshape(bs, ts, c_heads, HEAD_DIM))
            outs["cv_s"].append(vsm.reshape(bs, ts, c_heads, HEAD_DIM))
            outs["cf_s"].append(fsm.reshape(bs, ts, c_heads))
    y_prompt = rmsnorm(hp, final_norm_w, F32, NORM_ROWS).reshape(bp, tp, d)
    y_sample = rmsnorm(hs, final_norm_w, F32, bs * ts).reshape(bs, ts, d)
    st = {k: jnp.stack(v) for k, v in outs.items()}
    return (y_prompt, y_sample, st["sa_p"], st["sa_s"], st["bk_p"], st["bv_p"], st["bk_s"], st["bv_s"],
            st["ck_p"], st["cv_p"], st["cf_p"], st["ck_s"], st["cv_s"], st["cf_s"])
```

```python
import functools

import numpy as np
import jax
import jax.numpy as jnp
from jax import lax
from jax.experimental import pallas as pl
from jax.experimental.pallas import tpu as pltpu

F32 = jnp.float32
BF16 = jnp.bfloat16

EPS = 1e-6
HEAD_DIM = 128
LANES = 128
LOG2E = 1.4426950408889634
F32_UNDERFLOW_LOG2 = 150.0
HGRN_CHUNK = 64
VMEM_LIMIT_BYTES = 56 * 1024 * 1024

PROJ_TILE = 1024
DECODE_PROJ_COLS = 512
NORM_ROWS = 256
NORM_INPUT_BUFFERS = 3
ATTN_ROWS = 1024
SB_KEYS = 256
FOX_KEYS = 512
FOX_ROW_CHUNK = 256
HGRN_ROWS = 1024
HGRN_HEADS = 2
CUMSUM_ROWS = 512

_NT = (((1,), (1,)), ((), ()))
_TN = (((0,), (0,)), ((), ()))


def _params(*sem):
    return pltpu.CompilerParams(dimension_semantics=sem, vmem_limit_bytes=VMEM_LIMIT_BYTES)


def _dot(a, b):
    return jnp.dot(a, b, preferred_element_type=F32)


def _dot_nt(a, b):
    return lax.dot_general(a, b, _NT, preferred_element_type=F32)


def _dot_tn(a, b):
    return lax.dot_general(a, b, _TN, preferred_element_type=F32)


def _split3(x):
    hi = x.astype(BF16)
    r1 = x - hi.astype(F32)
    mid = r1.astype(BF16)
    lo = (r1 - mid.astype(F32)).astype(BF16)
    return hi, mid, lo


def _dot_exact_lhs01(a01, x):
    hi, mid, lo = _split3(x)
    return _dot(a01, hi) + _dot(a01, mid) + _dot(a01, lo)


def _dot_exact_lhs01x3(a01x3, x):
    return _dot(a01x3, jnp.concatenate(_split3(x), axis=0))


def _sigmoid_pair(z):
    e = jnp.exp(-jnp.abs(z))
    r = 1.0 / (1.0 + e)
    er = e * r
    pos = z >= 0
    return jnp.where(pos, r, er), jnp.where(pos, er, r)


def _silu(x):
    return x * _sigmoid_pair(x)[0]


def _log_sigmoid(x):
    return jnp.minimum(x, 0.0) - jnp.log(1.0 + jnp.exp(-jnp.abs(x)))


def _rmsnorm_rows(w_ref, x_ref, o_ref):
    x = x_ref[...]
    ms = jnp.mean(x * x, axis=-1, keepdims=True)
    o_ref[...] = (x * lax.rsqrt(ms + EPS) * w_ref[...]).astype(o_ref.dtype)


def _rmsnorm_body(x_hbm, w_ref, o_hbm, *, tm):
    m, d = x_hbm.shape
    rows = lambda i: (i, 0)
    pltpu.emit_pipeline(
        functools.partial(_rmsnorm_rows, w_ref),
        grid=(m // tm,),
        in_specs=[pl.BlockSpec((tm, d), rows, pipeline_mode=pl.Buffered(NORM_INPUT_BUFFERS))],
        out_specs=[pl.BlockSpec((tm, d), rows)],
    )(x_hbm, o_hbm)


def rmsnorm(x, w, out_dtype, tm):
    m, d = x.shape
    return pl.pallas_call(
        functools.partial(_rmsnorm_body, tm=tm),
        in_specs=[pl.BlockSpec(memory_space=pl.ANY), pl.BlockSpec(memory_space=pltpu.VMEM)],
        out_specs=pl.BlockSpec(memory_space=pl.ANY),
        out_shape=jax.ShapeDtypeStruct((m, d), out_dtype),
        compiler_params=pltpu.CompilerParams(vmem_limit_bytes=VMEM_LIMIT_BYTES),
        name="rmsnorm",
    )(x, w.reshape(1, d))


def _round_proj_body(a_ref, w_ref, y_ref, wb_ref, *, w_transposed):
    wb = w_ref[...].astype(BF16)
    wb_ref[...] = wb
    y_ref[...] = _dot_nt(a_ref[...], wb) if w_transposed else _dot(a_ref[...], wb)


def round_and_proj(a, w, layer, ncols, tn, w_transposed=False):
    m, kk = a.shape
    if w_transposed:
        w_spec = pl.BlockSpec((None, tn, kk), lambda j: (layer, j, 0))
        wb_spec, wb_shape = pl.BlockSpec((tn, kk), lambda j: (j, 0)), (ncols, kk)
    else:
        w_spec = pl.BlockSpec((None, kk, tn), lambda j: (layer, 0, j))
        wb_spec, wb_shape = pl.BlockSpec((kk, tn), lambda j: (0, j)), (kk, ncols)
    return pl.pallas_call(
        functools.partial(_round_proj_body, w_transposed=w_transposed),
        grid=(pl.cdiv(ncols, tn),),
        in_specs=[pl.BlockSpec((m, kk), lambda j: (0, 0)), w_spec],
        out_specs=[pl.BlockSpec((m, tn), lambda j: (0, j)), wb_spec],
        out_shape=[jax.ShapeDtypeStruct((m, ncols), F32), jax.ShapeDtypeStruct(wb_shape, BF16)],
        compiler_params=_params("parallel"),
        name="round_and_proj",
    )(a, w)


def _round_proj_out_body(*refs, n_in):
    a_refs, w_refs = refs[:n_in], refs[n_in:2 * n_in]
    r_ref, y_ref, wb_ref = refs[2 * n_in:]
    kk = a_refs[0].shape[1]
    acc = r_ref[...]
    for idx, (a_ref, w_ref) in enumerate(zip(a_refs, w_refs)):
        wb = w_ref[...].astype(BF16)
        wb_ref[idx * kk:(idx + 1) * kk, :] = wb
        acc = acc + _dot(a_ref[...], wb)
    y_ref[...] = acc


def round_and_proj_out(a_list, w, layer, residual, tn):
    m, kk = a_list[0].shape
    n = w.shape[2]
    assert all(a.shape == (m, kk) for a in a_list) and w.shape[1] == kk * len(a_list) and n % tn == 0
    in_specs = [pl.BlockSpec((m, kk), lambda j: (0, 0)) for _ in a_list]
    in_specs += [pl.BlockSpec((None, kk, tn), functools.partial(lambda j, r: (layer, r, j), r=r))
                 for r in range(len(a_list))]
    in_specs.append(pl.BlockSpec((m, tn), lambda j: (0, j)))
    return pl.pallas_call(
        functools.partial(_round_proj_out_body, n_in=len(a_list)),
        grid=(n // tn,),
        in_specs=in_specs,
        out_specs=[pl.BlockSpec((m, tn), lambda j: (0, j)), pl.BlockSpec((kk * len(a_list), tn), lambda j: (0, j))],
        out_shape=[jax.ShapeDtypeStruct((m, n), F32), jax.ShapeDtypeStruct((kk * len(a_list), n), BF16)],
        compiler_params=_params("parallel"),
        name="round_and_proj_out",
    )(*a_list, *([w] * len(a_list)), residual)


def _proj_body(*refs, n_in, has_residual, outs, w_transposed):
    a_refs = refs[:n_in]
    w_refs = refs[n_in:2 * n_in]
    pos = 2 * n_in
    r_ref = refs[pos] if has_residual else None
    o_refs = refs[pos + int(has_residual):]
    acc = None
    for a_ref, w_ref in zip(a_refs, w_refs):
        d = _dot_nt(a_ref[...], w_ref[...]) if w_transposed else _dot(a_ref[...], w_ref[...])
        acc = d if acc is None else acc + d
    if has_residual:
        acc = r_ref[...] + acc
    for o_ref, (_, scale) in zip(o_refs, outs):
        val = acc if scale == 1.0 else acc * scale
        o_ref[...] = val.astype(o_ref.dtype)


def proj(a_list, w, col0, ncols, tm, tn, outs, residual=None, w_transposed=False):
    m = a_list[0].shape[0]
    kk = a_list[0].shape[1]
    assert all(a.shape == (m, kk) for a in a_list)
    assert m % tm == 0 and ncols % tn == 0 and col0 % tn == 0
    cb = col0 // tn
    in_specs = [pl.BlockSpec((tm, kk), lambda i, j: (i, 0)) for _ in a_list]
    if w_transposed:
        assert len(a_list) == 1 and w.shape[1] == kk
        in_specs.append(pl.BlockSpec((tn, kk), lambda i, j: (cb + j, 0)))
    else:
        assert w.shape[0] == kk * len(a_list)
        in_specs += [pl.BlockSpec((kk, tn), functools.partial(lambda i, j, r: (r, cb + j), r=r))
                     for r in range(len(a_list))]
    args = list(a_list) + [w] * len(a_list)
    if residual is not None:
        in_specs.append(pl.BlockSpec((tm, tn), lambda i, j: (i, j)))
        args.append(residual)
    out_specs = [pl.BlockSpec((tm, tn), lambda i, j: (i, j)) for _ in outs]
    out_shape = [jax.ShapeDtypeStruct((m, ncols), dtype) for dtype, _ in outs]
    return pl.pallas_call(
        functools.partial(_proj_body, n_in=len(a_list), has_residual=residual is not None, outs=tuple(outs),
                          w_transposed=w_transposed),
        grid=(m // tm, ncols // tn),
        in_specs=in_specs,
        out_specs=out_specs,
        out_shape=out_shape,
        compiler_params=_params("parallel", "parallel"),
        name="proj",
    )(*args)


def _hgrn_maps(c):
    levels = int(np.log2(c))
    assert 2 ** levels == c
    t = np.arange(c)[:, None]
    s = np.arange(c)[None, :]
    mats = [(s <= t), (s > t)]
    for l in range(levels):
        b = 2 ** l
        start = (t // (2 * b)) * (2 * b)
        upper = (t // b) % 2 == 1
        mats.append((upper & (s >= start + b) & (s <= t)) | ((~upper) & (s > t) & (s <= start + b - 1)))
    return np.concatenate(mats, axis=0).astype(np.float32), levels


def _hgrn_body(q_ref, z_ref, v_ref, g_ref, lb_ref, nw_ref, a_ref, s0_ref, o_ref, sout_ref, st_scr,
               *, c, n_chunks, levels, heads):
    tb = pl.program_id(2)

    @pl.when(tb == 0)
    def _():
        for hh in range(heads):
            st_scr[hh] = s0_ref[0, hh].T

    nw = nw_ref[...]
    amat = a_ref[...]
    row = lax.broadcasted_iota(jnp.int32, (c, c), 0)
    col = lax.broadcasted_iota(jnp.int32, (c, c), 1)
    xor = row ^ col

    units = [(slice(ci * c, (ci + 1) * c), slice(hh * HEAD_DIM, (hh + 1) * HEAD_DIM))
             for hh in range(heads) for ci in range(n_chunks)]
    nu = len(units)
    lbs = [lb_ref[:, cols] for _, cols in units]
    qs = [_silu(q_ref[rows, cols]) for rows, cols in units]
    sigs = [_sigmoid_pair(z_ref[rows, cols]) for rows, cols in units]
    gs = [jnp.log(lbs[u] + (1.0 - lbs[u]) * sigs[u][0]) for u in range(nu)]
    ks = [(1.0 - lbs[u]) * sigs[u][1] for u in range(nu)]
    vs = [v_ref[rows, cols] for rows, cols in units]
    vbs = [v.astype(BF16) for v in vs]
    es = [jnp.exp(_dot_exact_lhs01x3(amat, g)) for g in gs]
    atts = [None] * nu
    for l in range(levels - 1, -1, -1):
        for u in range(nu):
            el = es[u][(2 + l) * c:(3 + l) * c]
            al = _dot_nt((qs[u] * el).astype(BF16), (ks[u] * el).astype(BF16))
            atts[u] = al if atts[u] is None else jnp.where(xor < 2 ** (l + 1), al, atts[u])
    atts = [jnp.where(row > col, att, 0.0).astype(BF16) for att in atts]
    o_intras = [_dot(atts[u], vbs[u]) + jnp.sum(qs[u] * ks[u], axis=-1, keepdims=True) * vs[u] for u in range(nu)]
    upds = [_dot_tn(vbs[u], (ks[u] * es[u][c:2 * c]).astype(BF16)) for u in range(nu)]
    qgs = [(qs[u] * es[u][0:c]).astype(BF16) for u in range(nu)]

    for hh in range(heads):
        st = st_scr[hh]
        for ci in range(n_chunks):
            u = hh * n_chunks + ci
            rows, cols = units[u]
            o = _dot_nt(qgs[u], st.astype(BF16)) + o_intras[u]
            st = st * es[u][c - 1:c] + upds[u]
            ms = jnp.mean(o * o, axis=-1, keepdims=True)
            y = o * lax.rsqrt(ms + EPS) * nw
            o_ref[rows, cols] = (y * _silu(g_ref[rows, cols])).astype(o_ref.dtype)
        st_scr[hh] = st

    @pl.when(tb == pl.num_programs(2) - 1)
    def _():
        for hh in range(heads):
            sout_ref[0, hh] = st_scr[hh].T


def hgrn2(qa, fa, ia, ga, lb, a_norm_w, s0, batch, seq, rows_per_step, c, heads_per_step, out_dtype,
          col0=(0, 0, 0, 0)):
    n = qa.shape[0]
    width = lb.shape[0]
    h = width // HEAD_DIM
    assert n == batch * seq and seq % rows_per_step == 0 and rows_per_step % c == 0 and h % heads_per_step == 0
    nb = seq // rows_per_step
    gw = heads_per_step * HEAD_DIM
    amat_np, levels = _hgrn_maps(c)
    amat = jnp.asarray(np.concatenate([amat_np] * 3, axis=1), BF16)
    row_spec = pl.BlockSpec((rows_per_step, gw), lambda b, hh, t: (b * nb + t, hh))
    in_row_specs = [pl.BlockSpec((rows_per_step, gw), functools.partial(lambda b, hh, t, off: (b * nb + t, off + hh), off=c0 // gw))
                    for c0 in col0]
    assert all(c0 % gw == 0 for c0 in col0)
    state_spec = pl.BlockSpec((1, heads_per_step, HEAD_DIM, HEAD_DIM), lambda b, hh, t: (b, hh, 0, 0))
    body = functools.partial(_hgrn_body, c=c, n_chunks=rows_per_step // c, levels=levels, heads=heads_per_step)
    return pl.pallas_call(
        body,
        grid=(batch, h // heads_per_step, nb),
        in_specs=in_row_specs + [
                  pl.BlockSpec((1, gw), lambda b, hh, t: (0, hh)),
                  pl.BlockSpec((1, HEAD_DIM), lambda b, hh, t: (0, 0)),
                  pl.BlockSpec(amat.shape, lambda b, hh, t: (0, 0)),
                  state_spec],
        out_specs=[row_spec, state_spec],
        out_shape=[jax.ShapeDtypeStruct((n, width), out_dtype),
                   jax.ShapeDtypeStruct((batch, h, HEAD_DIM, HEAD_DIM), F32)],
        scratch_shapes=[pltpu.VMEM((heads_per_step, HEAD_DIM, HEAD_DIM), F32)],
        compiler_params=_params("parallel", "parallel", "arbitrary"),
        name="hgrn2",
    )(qa, fa, ia, ga, lb.reshape(1, width), a_norm_w.reshape(1, HEAD_DIM), amat, s0)


def _sb_prompt_body(q_ref, k_ref, v_ref, g_ref, uu_ref, o_ref, run_scr, acc_scr, w_scr, *, tq, tk, rc):
    i = pl.program_id(1)
    nd = tq // tk
    nl = tk // LANES
    q0 = pl.multiple_of(i * tq, tq)
    run_scr[...] = jnp.zeros(run_scr.shape, F32)
    acc_scr[...] = jnp.zeros(acc_scr.shape, F32)

    def apply_pending(slot, k_prev, first_row):
        vt = v_ref[pl.ds(pl.multiple_of(k_prev, tk), tk), :]
        for r0 in range(first_row, tq, rc):
            rows = slice(r0, min(r0 + rc, tq))
            acc_scr[rows, :] = acc_scr[rows, :] + _dot(w_scr[slot, rows, :], vt)

    def score(slot, k0, first_row, masked):
        kt = k_ref[pl.ds(pl.multiple_of(k0, tk), tk), :]
        for r0 in range(first_row, tq, rc):
            r1 = min(r0 + rc, tq)
            rows = slice(r0, r1)
            z = _dot_nt(q_ref[rows, :], kt)
            if masked:
                qpos = lax.broadcasted_iota(jnp.int32, (r1 - r0, LANES), 0) + (q0 + r0)
                kpos = lax.broadcasted_iota(jnp.int32, (r1 - r0, LANES), 1) + k0
            zs, sps, his, los, valids = [], [], [], [], []
            rowsum = None
            for c in range(nl):
                zc = z[:, c * LANES:(c + 1) * LANES]
                sp = jnp.maximum(zc, 0.0) + jnp.log2(1.0 + jnp.exp2(-jnp.abs(zc)))
                if masked:
                    valid = (kpos + c * LANES) < qpos
                    sp = jnp.where(valid, sp, 0.0)
                    valids.append(valid)
                hi = sp.astype(BF16)
                lo = (sp - hi.astype(F32)).astype(BF16)
                zs.append(zc); sps.append(sp); his.append(hi); los.append(lo)
                rowsum = sp if rowsum is None else rowsum + sp
            tail = _dot(jnp.concatenate(his + los, axis=1), uu_ref[...])
            run = run_scr[rows, :]
            for c in range(nl):
                w = jnp.exp2(zs[c] - (sps[c] + tail[:, c * LANES:(c + 1) * LANES] + run))
                if masked:
                    w = jnp.where(valids[c], w, 0.0)
                w_scr[slot, rows, c * LANES:(c + 1) * LANES] = w.astype(BF16)
            run_scr[rows, :] = run + jnp.sum(rowsum, axis=-1, keepdims=True)

    assert nd % 2 == 0
    for d in range(nd - 1, 0, -2):
        score(0, q0 + d * tk, d * tk, True)
        if d + 1 < nd:
            apply_pending(1, q0 + (d + 1) * tk, (d + 1) * tk)
        score(1, q0 + (d - 1) * tk, (d - 1) * tk, True)
        apply_pending(0, q0 + d * tk, d * tk)

    def trip(carry):
        jj, kp, _ = carry
        k_a = q0 - (2 * jj + 1) * tk
        score(0, k_a, 0, False)
        apply_pending(1, kp, 0)
        score(1, k_a - tk, 0, False)
        more = jnp.min(run_scr[...]) < F32_UNDERFLOW_LOG2
        apply_pending(0, k_a, 0)
        return jj + 1, k_a - tk, more

    _, k_pending, _ = lax.while_loop(lambda carry: (carry[0] < (i * nd) // 2) & carry[2], trip,
                                     (jnp.int32(0), q0, jnp.bool_(True)))
    apply_pending(1, k_pending, 0)
    o_ref[...] = (acc_scr[...] * _silu(g_ref[...])).astype(o_ref.dtype)


def _tail_matrix(n):
    sp = np.arange(n)[:, None]
    s = np.arange(n)[None, :]
    return (sp > s).astype(np.float32)


def sb_prompt(q, k, v, gate, tq, tk, rc, out_dtype):
    t, width = q.shape
    h = width // HEAD_DIM
    u = _tail_matrix(tk)
    uu = jnp.asarray(np.concatenate([u, u], axis=0), BF16)
    q_spec = pl.BlockSpec((tq, HEAD_DIM), lambda hh, i: (i, hh))
    kv_spec = pl.BlockSpec((t, HEAD_DIM), lambda hh, i: (0, hh))
    return pl.pallas_call(
        functools.partial(_sb_prompt_body, tq=tq, tk=tk, rc=rc),
        grid=(h, t // tq),
        in_specs=[q_spec, kv_spec, kv_spec, q_spec, pl.BlockSpec(uu.shape, lambda hh, i: (0, 0))],
        out_specs=q_spec,
        out_shape=jax.ShapeDtypeStruct((t, width), out_dtype),
        scratch_shapes=[pltpu.VMEM((tq, LANES), F32), pltpu.VMEM((tq, HEAD_DIM), F32),
                        pltpu.VMEM((2, tq, tk), BF16)],
        compiler_params=_params("parallel", "arbitrary"),
        name="sb_prompt",
    )(q, k, v, gate, uu)


def _logf_cumsum_body(fl_ref, b_ref, tri_ref, lf_ref, f2_ref, carry_scr):
    @pl.when(pl.program_id(0) == 0)
    def _():
        carry_scr[...] = jnp.zeros_like(carry_scr)

    lf = _log_sigmoid(fl_ref[...] + b_ref[...])
    lf_ref[...] = lf
    f = carry_scr[...] + _dot_exact_lhs01(tri_ref[...], lf)
    f2_ref[...] = f * LOG2E
    carry_scr[...] = f[f.shape[0] - 1:, :]


def logf_cumsum(fl, b_forget, blk):
    t, h = fl.shape
    tri = jnp.asarray(np.tril(np.ones((blk, blk), np.float32)), BF16)
    spec = pl.BlockSpec((blk, h), lambda i: (i, 0))
    return pl.pallas_call(
        _logf_cumsum_body,
        grid=(t // blk,),
        in_specs=[spec, pl.BlockSpec((1, h), lambda i: (0, 0)), pl.BlockSpec((blk, blk), lambda i: (0, 0))],
        out_specs=[spec, spec],
        out_shape=[jax.ShapeDtypeStruct((t, h), F32), jax.ShapeDtypeStruct((t, h), F32)],
        scratch_shapes=[pltpu.VMEM((1, h), F32)],
        compiler_params=_params("arbitrary"),
        name="logf_cumsum",
    )(fl, b_forget.reshape(1, h), tri)


def _logf_body(fl_ref, b_ref, lf_ref):
    lf_ref[...] = _log_sigmoid(fl_ref[...] + b_ref[...])


def logf_only(fl, b_forget):
    t, h = fl.shape
    return pl.pallas_call(
        _logf_body,
        grid=(1,),
        in_specs=[pl.BlockSpec((t, h), lambda i: (0, 0)), pl.BlockSpec((1, h), lambda i: (0, 0))],
        out_specs=pl.BlockSpec((t, h), lambda i: (0, 0)),
        out_shape=jax.ShapeDtypeStruct((t, h), F32),
        name="logf",
    )(fl, b_forget.reshape(1, h))


_BIAS_PIECES = 3


def _bias_selectors(h):
    sel = np.zeros((2, h, _BIAS_PIECES, h, LANES), np.float32)
    for hh in range(h):
        for p in range(_BIAS_PIECES):
            sel[0, hh, p, hh, p] = 1.0
            sel[1, hh, p, hh, _BIAS_PIECES + p] = -1.0
    return sel


def _bias_columns(f, sel_ref, query_side):
    out = None
    for p, piece in enumerate(_split3(f)):
        d = _dot(piece, sel_ref[0, p])
        out = d if out is None else out + d
    lane = lax.broadcasted_iota(jnp.int32, out.shape, 1)
    ones_at = (lane >= _BIAS_PIECES) & (lane < 2 * _BIAS_PIECES) if query_side else lane < _BIAS_PIECES
    return jnp.where(ones_at, 1.0, out).astype(BF16)


def _fox_prompt_body(q_ref, k_ref, v_ref, g_ref, f_ref, selq_ref, selk_ref, o_ref,
                     kx_scr, vx_scr, m_scr, acc_scr, p_scr, alpha_scr, kmax_scr, ub_scr, *, tq, tk, rc):
    i = pl.program_id(1)
    nd = tq // tk
    nl = tk // LANES
    t_all = k_ref.shape[0]

    head = pl.program_id(0)

    def own_column(f_rows):
        lane = lax.broadcasted_iota(jnp.int32, f_rows.shape, 1)
        return jnp.sum(jnp.where(lane == head, f_rows, 0.0), axis=-1, keepdims=True)

    @pl.when(i == 0)
    def _():
        kb = k_ref[...].astype(BF16)
        kx_scr[:, 0:HEAD_DIM] = kb
        kx_scr[:, HEAD_DIM:] = _bias_columns(f_ref[...], selk_ref, False)
        vx_scr[:, 0:HEAD_DIM] = v_ref[...].astype(BF16)
        vx_scr[:, HEAD_DIM:] = jnp.ones((t_all, LANES), BF16)
        kf = kb.astype(F32)
        knorm2 = jnp.max(jnp.sum(kf * kf, axis=-1, keepdims=True), axis=0, keepdims=True)
        kmax_scr[...] = jnp.broadcast_to(jnp.sqrt(knorm2), kmax_scr.shape)

    q0 = pl.multiple_of(i * tq, tq)
    qx = jnp.concatenate([q_ref[...], _bias_columns(f_ref[pl.ds(q0, tq), :], selq_ref, True)], axis=1)
    m_scr[...] = jnp.full(m_scr.shape, -jnp.inf, F32)
    acc_scr[...] = jnp.zeros(acc_scr.shape, F32)
    qf = q_ref[...].astype(F32)
    qnorm = jnp.sqrt(jnp.sum(qf * qf, axis=-1, keepdims=True))
    ub_scr[...] = qnorm * kmax_scr[...] * (1.0 + 2.0 ** -10) + own_column(f_ref[pl.ds(q0, tq), :])

    def apply_pending(slot, k_prev, first_row):
        vxt = vx_scr[pl.ds(pl.multiple_of(k_prev, tk), tk), :]
        for r0 in range(first_row, tq, rc):
            rows = slice(r0, r0 + rc)
            pv = _dot(p_scr[slot, rows, :], vxt)
            alpha = alpha_scr[slot, rows, :]
            acc_scr[rows, 0:HEAD_DIM] = alpha * acc_scr[rows, 0:HEAD_DIM] + pv[:, 0:HEAD_DIM]
            acc_scr[rows, HEAD_DIM:] = alpha * acc_scr[rows, HEAD_DIM:] + pv[:, HEAD_DIM:]

    def score(slot, k0, first_row, masked):
        kxt = kx_scr[pl.ds(pl.multiple_of(k0, tk), tk), :]
        for r0 in range(first_row, tq, rc):
            rows = slice(r0, r0 + rc)
            s = _dot_nt(qx[r0:r0 + rc, :], kxt)
            sb = [s[:, c * LANES:(c + 1) * LANES] for c in range(nl)]
            if masked:
                qpos = lax.broadcasted_iota(jnp.int32, (rc, LANES), 0) + (q0 + r0)
                kpos = lax.broadcasted_iota(jnp.int32, (rc, LANES), 1) + k0
                sb = [jnp.where((kpos + c * LANES) <= qpos, sb[c], -jnp.inf) for c in range(nl)]
            mx = sb[0]
            for c in range(1, nl):
                mx = jnp.maximum(mx, sb[c])
            m_old = m_scr[rows, :]
            m_new = jnp.maximum(m_old, jnp.max(mx, axis=-1, keepdims=True))
            alpha_scr[slot, rows, :] = jnp.exp2(m_old - m_new)
            for c in range(nl):
                p_scr[slot, rows, c * LANES:(c + 1) * LANES] = jnp.exp2(sb[c] - m_new).astype(BF16)
            m_scr[rows, :] = m_new

    assert nd % 2 == 0 and tk % rc == 0
    for d in range(0, nd, 2):
        score(0, q0 + d * tk, d * tk, True)
        if d > 0:
            apply_pending(1, q0 + (d - 1) * tk, (d - 1) * tk)
        score(1, q0 + (d + 1) * tk, (d + 1) * tk, True)
        apply_pending(0, q0 + d * tk, d * tk)

    p_scr[1, 0:(nd - 1) * tk, :] = jnp.zeros(((nd - 1) * tk, tk), BF16)
    alpha_scr[1, 0:(nd - 1) * tk, :] = jnp.ones(((nd - 1) * tk, LANES), F32)

    def pair(k_a, kp):
        score(0, k_a, 0, False)
        apply_pending(1, kp, 0)
        score(1, k_a - tk, 0, False)
        more = visible(k_a - tk - 1)
        apply_pending(0, k_a, 0)
        return k_a - tk, more

    def visible(k_hi):
        f_hi = own_column(f_ref[pl.ds(jnp.maximum(k_hi, 0), 1), :])
        return jnp.max(ub_scr[...] - f_hi - m_scr[...]) > -(F32_UNDERFLOW_LOG2 + 2.0)

    def trip(carry):
        jj, kp, _ = carry
        kp, more = pair(q0 - (2 * jj + 1) * tk, kp)
        return jj + 1, kp, more

    n_pairs = (i * nd) // 2
    _, k_pending, _ = lax.while_loop(lambda carry: (carry[0] < n_pairs) & carry[2], trip,
                                     (jnp.int32(0), q0 + (nd - 1) * tk, visible(q0 - 1)))
    apply_pending(1, k_pending, 0)
    o_ref[...] = (acc_scr[:, 0:HEAD_DIM] / acc_scr[:, HEAD_DIM:] * _silu(g_ref[...])).astype(o_ref.dtype)


def fox_prompt(q, k, v, gate, f2, tq, tk, rc, out_dtype):
    t, width = q.shape
    h = width // HEAD_DIM
    sel = jnp.asarray(_bias_selectors(h), BF16)
    q_spec = pl.BlockSpec((tq, HEAD_DIM), lambda hh, i: (i, hh))
    kv_spec = pl.BlockSpec((t, HEAD_DIM), lambda hh, i: (0, hh))
    sel_spec = pl.BlockSpec((1, _BIAS_PIECES, h, LANES), lambda hh, i: (hh, 0, 0, 0))
    return pl.pallas_call(
        functools.partial(_fox_prompt_body, tq=tq, tk=tk, rc=rc),
        grid=(h, t // tq),
        in_specs=[q_spec, kv_spec, kv_spec, q_spec, pl.BlockSpec((t, h), lambda hh, i: (0, 0)),
                  sel_spec, sel_spec],
        out_specs=q_spec,
        out_shape=jax.ShapeDtypeStruct((t, width), out_dtype),
        scratch_shapes=[pltpu.VMEM((t, 2 * HEAD_DIM), BF16), pltpu.VMEM((t, 2 * HEAD_DIM), BF16),
                        pltpu.VMEM((tq, LANES), F32), pltpu.VMEM((tq, 2 * HEAD_DIM), F32),
                        pltpu.VMEM((2, tq, tk), BF16), pltpu.VMEM((2, tq, LANES), F32),
                        pltpu.VMEM((1, LANES), F32), pltpu.VMEM((tq, LANES), F32)],
        compiler_params=_params("parallel", "arbitrary"),
        name="fox_prompt",
    )(q, k, v, gate, f2, sel[0], sel[1])


HEADS_PER_GROUP = 8


def _expand_queries(q, tq):
    gw = q.shape[1]
    rep = jnp.concatenate([q] * HEADS_PER_GROUP, axis=0)
    r = lax.broadcasted_iota(jnp.int32, (HEADS_PER_GROUP * tq, gw), 0) // tq
    cidx = lax.broadcasted_iota(jnp.int32, (HEADS_PER_GROUP * tq, gw), 1) // HEAD_DIM
    return jnp.where(r == cidx, rep, 0.0)


def _collect_heads(full, tq):
    return jnp.concatenate(
        [full[hh * tq:(hh + 1) * tq, hh * HEAD_DIM:(hh + 1) * HEAD_DIM] for hh in range(HEADS_PER_GROUP)],
        axis=1)


def _gather_keys(cache_ref, new_ref, scr, past, tk):
    g = HEADS_PER_GROUP
    x = cache_ref[0].reshape(past // g, g, g, HEAD_DIM)
    x = jnp.swapaxes(x, 1, 2)
    for hh in range(g):
        scr[0:past, hh * HEAD_DIM:(hh + 1) * HEAD_DIM] = x[:, hh].reshape(past, HEAD_DIM).astype(BF16)
    scr[past:tk, :] = new_ref[...].astype(BF16)


def _row_blocks(n):
    return [(r0, min(r0 + LANES, n)) for r0 in range(0, n, LANES)]


def _cumsum_rows(pieces, tri_ref):
    n = pieces[0].shape[0]
    out, carry = [], None
    for r0, r1 in _row_blocks(n):
        tri = tri_ref[0:r1 - r0, 0:r1 - r0]
        local = None
        for piece in pieces:
            d = _dot(tri, piece[r0:r1])
            local = d if local is None else local + d
        if carry is not None:
            local = local + carry
        carry = local[r1 - r0 - 1:r1 - r0]
        out.append(local)
    return jnp.concatenate(out, axis=0)


def _tailsum_rows(pieces, tri_ref):
    n = pieces[0].shape[0]
    out, carry = [], None
    for r0, r1 in reversed(_row_blocks(n)):
        tri = tri_ref[0:r1 - r0, 0:r1 - r0]
        local, total = None, None
        for piece in pieces:
            blk = piece[r0:r1]
            d = _dot_tn(tri, blk)
            local = d if local is None else local + d
            t = blk.astype(F32)
            total = t if total is None else total + t
        local = local - total
        if carry is not None:
            local = local + carry
        carry = local[0:1] + total[0:1]
        out.append(local)
    return jnp.concatenate(out[::-1], axis=0)


def _sb_decode_body(q_ref, kn_ref, vn_ref, kc_ref, vc_ref, g_ref, tri_ref, o_ref, k_scr, v_scr, *, past, tq):
    tk = past + tq
    _gather_keys(kc_ref, kn_ref, k_scr, past, tk)
    _gather_keys(vc_ref, vn_ref, v_scr, past, tk)

    lanes = HEADS_PER_GROUP * tq
    qx = _expand_queries(q_ref[...] * (HEAD_DIM ** -0.5), tq).astype(BF16)
    z = _dot_nt(k_scr[...], qx)
    kpos = lax.broadcasted_iota(jnp.int32, (tk, lanes), 0)
    qpos = past + lax.broadcasted_iota(jnp.int32, (tk, lanes), 1) % tq
    valid = kpos < qpos
    ls_neg = -(jnp.maximum(z, 0.0) + jnp.log(1.0 + jnp.exp(-jnp.abs(z))))
    lm = jnp.where(valid, ls_neg, 0.0)
    hi = lm.astype(BF16)
    lo = (lm - hi.astype(F32)).astype(BF16)
    tail = _tailsum_rows((hi, lo), tri_ref)
    w = jnp.where(valid, jnp.exp(z + ls_neg + tail), 0.0)
    full = _dot_tn(w.astype(BF16), v_scr[...])
    o_ref[...] = (_collect_heads(full, tq) * _silu(g_ref[...])).astype(o_ref.dtype)


def _cache_spec(layer, past):
    return pl.BlockSpec((None, 1, past, HEADS_PER_GROUP, HEAD_DIM), lambda b, g: (layer, b, 0, g, 0))


def _decode_row_specs(tq, gw, col0):
    assert all(c0 % gw == 0 for c0 in col0)
    return [pl.BlockSpec((tq, gw), functools.partial(lambda b, g, off: (b, off + g), off=c0 // gw)) for c0 in col0]


def sb_decode(q, k_new, v_new, k_cache, v_cache, layer, gate, batch, tq, out_dtype, col0=(0, 0, 0, 0)):
    n = q.shape[0]
    width = k_cache.shape[3] * HEAD_DIM
    past = k_cache.shape[2]
    tk = past + tq
    gw = HEADS_PER_GROUP * HEAD_DIM
    groups = width // gw
    tri = jnp.asarray(np.tril(np.ones((LANES, LANES), np.float32)), BF16)
    row_spec = pl.BlockSpec((tq, gw), lambda b, g: (b, g))
    rs = _decode_row_specs(tq, gw, col0)
    cache_spec = _cache_spec(layer, past)
    return pl.pallas_call(
        functools.partial(_sb_decode_body, past=past, tq=tq),
        grid=(batch, groups),
        in_specs=[rs[0], rs[1], rs[2], cache_spec, cache_spec, rs[3],
                  pl.BlockSpec((LANES, LANES), lambda b, g: (0, 0))],
        out_specs=row_spec,
        out_shape=jax.ShapeDtypeStruct((n, width), out_dtype),
        scratch_shapes=[pltpu.VMEM((tk, gw), BF16), pltpu.VMEM((tk, gw), BF16)],
        compiler_params=_params("parallel", "parallel"),
        name="sb_decode",
    )(q, k_new, v_new, k_cache, v_cache, gate, tri)


def _fox_decode_body(q_ref, kn_ref, vn_ref, kc_ref, vc_ref, g_ref, lfn_ref, lfc_ref, ex_ref, tri_ref, o_ref,
                     k_scr, v_scr, lf_scr, *, past, tq):
    tk = past + tq
    _gather_keys(kc_ref, kn_ref, k_scr, past, tk)
    _gather_keys(vc_ref, vn_ref, v_scr, past, tk)
    lf_scr[0:past, :] = lfc_ref[0]
    lf_scr[past:tk, :] = lfn_ref[...]

    lanes = HEADS_PER_GROUP * tq
    hi, mid, lo = _split3(lf_scr[...])
    ex = ex_ref[0]
    lfx = _dot(hi, ex) + _dot(mid, ex) + _dot(lo, ex)
    f_key = _cumsum_rows(_split3(lfx), tri_ref)
    kpos = lax.broadcasted_iota(jnp.int32, (tk, lanes), 0)
    qpos = past + lax.broadcasted_iota(jnp.int32, (tk, lanes), 1) % tq
    f_query = jnp.sum(jnp.where(kpos == qpos, f_key, 0.0), axis=0, keepdims=True)

    qx = _expand_queries(q_ref[...] * (HEAD_DIM ** -0.5), tq).astype(BF16)
    s = _dot_nt(k_scr[...], qx) + (f_query - f_key)
    s = jnp.where(kpos <= qpos, s, -jnp.inf)
    p = jnp.exp(s - jnp.max(s, axis=0, keepdims=True))
    p = p / jnp.sum(p, axis=0, keepdims=True)
    full = _dot_tn(p.astype(BF16), v_scr[...])
    o_ref[...] = (_collect_heads(full, tq) * _silu(g_ref[...])).astype(o_ref.dtype)


def fox_decode(q, k_new, v_new, k_cache, v_cache, layer, gate, lf_new, lf_cache, batch, tq, out_dtype,
               col0=(0, 0, 0, 0)):
    n = q.shape[0]
    h = k_cache.shape[3]
    width = h * HEAD_DIM
    past = k_cache.shape[2]
    tk = past + tq
    gw = HEADS_PER_GROUP * HEAD_DIM
    groups = width // gw
    lanes = HEADS_PER_GROUP * tq
    tri = jnp.asarray(np.tril(np.ones((LANES, LANES), np.float32)), BF16)
    head_of_lane = np.arange(lanes)[None, None, :] // tq + HEADS_PER_GROUP * np.arange(groups)[:, None, None]
    expand = jnp.asarray((np.arange(h)[None, :, None] == head_of_lane).astype(np.float32), BF16)
    row_spec = pl.BlockSpec((tq, gw), lambda b, g: (b, g))
    rs = _decode_row_specs(tq, gw, col0)
    cache_spec = _cache_spec(layer, past)
    return pl.pallas_call(
        functools.partial(_fox_decode_body, past=past, tq=tq),
        grid=(batch, groups),
        in_specs=[rs[0], rs[1], rs[2], cache_spec, cache_spec, rs[3],
                  pl.BlockSpec((tq, h), lambda b, g: (b, 0)),
                  pl.BlockSpec((None, 1, past, h), lambda b, g: (layer, b, 0, 0)),
                  pl.BlockSpec((1, h, lanes), lambda b, g: (g, 0, 0)),
                  pl.BlockSpec((LANES, LANES), lambda b, g: (0, 0))],
        out_specs=row_spec,
        out_shape=jax.ShapeDtypeStruct((n, width), out_dtype),
        scratch_shapes=[pltpu.VMEM((tk, gw), BF16), pltpu.VMEM((tk, gw), BF16), pltpu.VMEM((tk, h), F32)],
        compiler_params=_params("parallel", "parallel"),
        name="fox_decode",
    )(q, k_new, v_new, k_cache, v_cache, gate, lf_new, lf_cache, expand, tri)


QK_SCALE_LOG2 = LOG2E * HEAD_DIM ** -0.5
FLAT32 = (F32, 1.0)
FLAT16 = (BF16, 1.0)
QUERY16 = (BF16, QK_SCALE_LOG2)


def _even_prompt(x, norm_w, w_in, w_out, lb, a_norm_w, s0, seq):
    n, d = x.shape
    half = d // 2
    hn = rmsnorm(x, norm_w, BF16, NORM_ROWS)

    def p(group, *outs):
        return proj([hn], w_in, group * half, half, PROJ_TILE, PROJ_TILE, outs)

    (rec,) = proj([hn], w_in, 0, 4 * half, PROJ_TILE, PROJ_TILE, (FLAT32,))
    (qb,), (gb,) = p(4, QUERY16), p(7, FLAT32)
    kb_leaf, kb = p(5, FLAT32, FLAT16)
    vb_leaf, vb = p(6, FLAT32, FLAT16)
    oa, s_new = hgrn2(rec, rec, rec, rec, lb, a_norm_w, s0, 1, seq, HGRN_ROWS, HGRN_CHUNK, HGRN_HEADS, BF16,
                      col0=(0, half, 2 * half, 3 * half))
    ob = sb_prompt(qb, kb, vb, gb, ATTN_ROWS, SB_KEYS, ATTN_ROWS, BF16)
    (y,) = proj([oa, ob], w_out, 0, d, PROJ_TILE, PROJ_TILE, (FLAT32,), residual=x)
    return y, s_new, kb_leaf, vb_leaf


def _even_decode(x, norm_w, w_in_f32, layer, w_out_f32, lb, a_norm_w, s0, k_cache, v_cache, batch, seq):
    n, d = x.shape
    half = d // 2
    hn = rmsnorm(x, norm_w, BF16, n)
    cols, w_in = round_and_proj(hn, w_in_f32, layer, 8 * half, DECODE_PROJ_COLS)
    oa, s_new = hgrn2(cols, cols, cols, cols, lb, a_norm_w, s0, batch, seq, seq, seq, half // HEAD_DIM, BF16,
                      col0=(0, half, 2 * half, 3 * half))
    ob = sb_decode(cols, cols, cols, k_cache, v_cache, layer, cols, batch, seq, BF16,
                   col0=(4 * half, 5 * half, 6 * half, 7 * half))
    y, w_out = round_and_proj_out([oa, ob], w_out_f32, layer, x, DECODE_PROJ_COLS)
    return y, s_new, cols[:, 5 * half:6 * half], cols[:, 6 * half:7 * half], w_in, w_out


def _odd_prompt(x, norm_w, w_in, w_fl, b_forget, w_out, seq):
    n, d = x.shape
    heads = d // HEAD_DIM
    hn = rmsnorm(x, norm_w, BF16, NORM_ROWS)

    def p(group, *outs):
        return proj([hn], w_in, group * d, d, PROJ_TILE, PROJ_TILE, outs, w_transposed=True)

    (q,), (gate,) = p(0, QUERY16), p(3, FLAT32)
    (fl,) = proj([hn], w_fl, 0, heads, PROJ_TILE, heads, (FLAT32,), w_transposed=True)
    logf, f2 = logf_cumsum(fl, b_forget, CUMSUM_ROWS)
    (k,), (v,) = p(1, FLAT32), p(2, FLAT32)
    o = fox_prompt(q, k, v, gate, f2, ATTN_ROWS, FOX_KEYS, FOX_ROW_CHUNK, BF16)
    (y,) = proj([o], w_out, 0, d, PROJ_TILE, PROJ_TILE, (FLAT32,), residual=x)
    return y, k, v, logf


def _odd_decode(x, norm_w, w_in_t_f32, layer, b_forget, w_out_f32, k_cache, v_cache, lf_cache, batch, seq):
    n, d = x.shape
    heads = d // HEAD_DIM
    hn = rmsnorm(x, norm_w, BF16, n)
    cols, w_in = round_and_proj(hn, w_in_t_f32, layer, 4 * d + heads, DECODE_PROJ_COLS, w_transposed=True)
    logf = logf_only(cols[:, 4 * d:], b_forget)
    o = fox_decode(cols, cols, cols, k_cache, v_cache, layer, cols, logf, lf_cache, batch, seq, BF16,
                   col0=(0, d, 2 * d, 3 * d))
    y, w_out = round_and_proj_out([o], w_out_f32, layer, x, DECODE_PROJ_COLS)
    return y, cols[:, d:2 * d], cols[:, 2 * d:3 * d], logf, w_in, w_out


def kernel(x_prompt, x_sample, state_a_hgrn, cache_b_k, cache_b_v, cache_c_k, cache_c_v, cache_c_logf,
           norm_w, final_norm_w, w_in_even, w_out_even, lb_logits, a_norm_w, w_in_odd, b_forget, w_out_odd):
    bp, tp, d = x_prompt.shape
    bs, ts, _ = x_sample.shape
    assert bp == 1
    depth = norm_w.shape[0]
    n_even = w_in_even.shape[0]
    lb_all = jnp.cumsum(jax.nn.softmax(lb_logits.astype(F32), axis=0), axis=0)[:n_even]

    hp = x_prompt.reshape(bp * tp, d)
    hs = x_sample.reshape(bs * ts, d)
    outs = {name: [] for name in ("sa_p", "sa_s", "bk_p", "bv_p", "bk_s", "bv_s",
                                  "ck_p", "cv_p", "cf_p", "ck_s", "cv_s", "cf_s")}
    for layer in range(depth):
        j = layer // 2
        if layer % 2 == 0:
            a_heads = state_a_hgrn.shape[2]
            b_heads = cache_b_k.shape[3]
            zeros = jnp.zeros((bp, a_heads) + state_a_hgrn.shape[3:], F32)
            hs, ss, ksm, vsm, w_in, w_out = _even_decode(hs, norm_w[layer], w_in_even, j, w_out_even, lb_all[j],
                                                         a_norm_w[j], state_a_hgrn[j], cache_b_k, cache_b_v, bs, ts)
            hp, sp, kp, vp = _even_prompt(hp, norm_w[layer], w_in, w_out, lb_all[j], a_norm_w[j], zeros, tp)
            outs["sa_p"].append(sp); outs["sa_s"].append(ss)
            outs["bk_p"].append(kp.reshape(bp, tp, b_heads, HEAD_DIM))
            outs["bv_p"].append(vp.reshape(bp, tp, b_heads, HEAD_DIM))
            outs["bk_s"].append(ksm.reshape(bs, ts, b_heads, HEAD_DIM))
            outs["bv_s"].append(vsm.reshape(bs, ts, b_heads, HEAD_DIM))
        else:
            c_heads = cache_c_k.shape[3]
            w_in_t = jnp.swapaxes(w_in_odd, 1, 2)
            hs, ksm, vsm, fsm, w_in, w_out = _odd_decode(hs, norm_w[layer], w_in_t, j, b_forget[j], w_out_odd,
                                                         cache_c_k, cache_c_v, cache_c_logf, bs, ts)
            w_fl = w_in[4 * c_heads * HEAD_DIM:, :]
            hp, kp, vp, fp = _odd_prompt(hp, norm_w[layer], w_in, w_fl, b_forget[j], w_out, tp)
            outs["ck_p"].append(kp.reshape(bp, tp, c_heads, HEAD_DIM))
            outs["cv_p"].append(vp.reshape(bp, tp, c_heads, HEAD_DIM))
            outs["cf_p"].append(fp.reshape(bp, tp, c_heads))
            outs["ck_s"].append(ksm.reshape(bs, ts, c_heads, HEAD_DIM))
            outs["cv_s"].append(vsm.reshape(bs, ts, c_heads, HEAD_DIM))
            outs["cf_s"].append(fsm.reshape(bs, ts, c_heads))
    y_prompt = rmsnorm(hp, final_norm_w, F32, NORM_ROWS).reshape(bp, tp, d)
    y_sample = rmsnorm(hs, final_norm_w, F32, bs * ts).reshape(bs, ts, d)
    st = {k: jnp.stack(v) for k, v in outs.items()}
    return (y_prompt, y_sample, st["sa_p"], st["sa_s"], st["bk_p"], st["bv_p"], st["bk_s"], st["bv_s"],
            st["ck_p"], st["cv_p"], st["cf_p"], st["ck_s"], st["cv_s"], st["cf_s"])
```
